```python
import math
import jax, jax.numpy as jnp
from jax import lax
import numpy as np

D_MODEL = 1024
BATCH = 4
SEQ = 8192
DEPTH = 2

CHUNK = 64
HEAD_DIM = 64
ROPE_THETA = 10000.0
EPS = 1e-6
NEG_INF = -1e30
A_Q_HEADS = 8
A_KV_HEADS = 2
A_WINDOW = 128
A_WIN_CHUNKS = A_WINDOW // CHUNK
B_HEADS = 8
B_PREV_CHUNKS = 8
B_MAX_REL = 128
C_HEADS = 8
C_V_DIM = 2 * HEAD_DIM
Q_BLOCK = 128

A_Q = A_Q_HEADS * HEAD_DIM
A_KV = A_KV_HEADS * HEAD_DIM
A_W = A_Q
B_W = B_HEADS * HEAD_DIM
EVEN_IN = A_Q + 2 * A_KV + A_W + 4 * B_W
EVEN_MIX = A_W + B_W
C_QK = C_HEADS * 2 * HEAD_DIM
C_W = C_HEADS * C_V_DIM
ODD_IN = 2 * C_QK + 2 * C_W
N_EVEN = (DEPTH + 1) // 2
N_ODD = DEPTH // 2
EVEN_SIZES = (A_Q, A_KV, A_KV, A_W, B_W, B_W, B_W, B_W)

kernel_name = "chunk_causal_hybrid_swa_relbias_diffattn"


def rms_norm(x, g):
    xf = x.astype(jnp.float32)
    y = xf * lax.rsqrt(jnp.mean(xf * xf, axis=-1, keepdims=True) + EPS)
    return (y * g.astype(jnp.float32)).astype(x.dtype)


def rope_tables(seq):
    inv = 1.0 / (ROPE_THETA ** (jnp.arange(0, HEAD_DIM, 2, dtype=jnp.float32) / HEAD_DIM))
    ang = jnp.arange(seq, dtype=jnp.float32)[:, None] * inv[None, :]
    return jnp.cos(ang), jnp.sin(ang)


def apply_rope(x, cos, sin):
    x1, x2 = jnp.split(x.astype(jnp.float32), 2, axis=-1)
    c = cos[None, :, None, :]
    s = sin[None, :, None, :]
    return jnp.concatenate([x1 * c - x2 * s, x2 * c + x1 * s], axis=-1).astype(x.dtype)


def to_chunks(t):
    b, s = t.shape[0], t.shape[1]
    return t.reshape(b, s // CHUNK, CHUNK, *t.shape[2:])


def chunk_band(t, n_prev):
    nc = t.shape[1]
    tp = jnp.pad(t, ((0, 0), (n_prev, 0), (0, 0), (0, 0), (0, 0)))
    band = jnp.stack([tp[:, j:j + nc] for j in range(n_prev + 1)], axis=2)
    return band.reshape(t.shape[0], nc, (n_prev + 1) * CHUNK, *t.shape[3:])


def band_valid(nc, n_prev):
    c = jnp.arange(nc)[:, None]
    j = jnp.arange(n_prev + 1)[None, :]
    return jnp.repeat((c - n_prev + j) >= 0, CHUNK, axis=1)


def sliding_window_sink_attention(q, k, v, sinks):
    b, s = q.shape[0], q.shape[1]
    nc = s // CHUNK
    g = A_Q_HEADS // A_KV_HEADS
    qc = to_chunks(q).reshape(b, nc, CHUNK, A_KV_HEADS, g, HEAD_DIM)
    kb = chunk_band(to_chunks(k), A_WIN_CHUNKS)
    vb = chunk_band(to_chunks(v), A_WIN_CHUNKS)
    scores = jnp.einsum('bcqhgd,bclhd->bchgql', qc, kb).astype(jnp.float32) / math.sqrt(HEAD_DIM)
    valid = band_valid(nc, A_WIN_CHUNKS)[None, :, None, None, None, :]
    scores = jnp.where(valid, scores, NEG_INF)
    sink = sinks.astype(jnp.float32).reshape(A_KV_HEADS, g)[None, None, :, :, None, None]
    sink = jnp.broadcast_to(sink, scores.shape[:-1] + (1,))
    probs = jax.nn.softmax(jnp.concatenate([scores, sink], axis=-1), axis=-1)[..., :-1]
    out = jnp.einsum('bchgql,bclhd->bcqhgd', probs.astype(v.dtype), vb)
    return out.reshape(b, s, A_Q_HEADS * HEAD_DIM)


def chunked_relbias_attention(q, k, v, rel_table):
    b, s = q.shape[0], q.shape[1]
    nc = s // CHUNK
    band_len = (B_PREV_CHUNKS + 1) * CHUNK
    qc = to_chunks(q)
    kb = chunk_band(to_chunks(k), B_PREV_CHUNKS)
    vb = chunk_band(to_chunks(v), B_PREV_CHUNKS)
    scores = jnp.einsum('bcqhd,bclhd->bhcql', qc, kb).astype(jnp.float32) / math.sqrt(HEAD_DIM)
    qpos = B_PREV_CHUNKS * CHUNK + jnp.arange(CHUNK)
    kpos = jnp.arange(band_len)
    rel = jnp.clip(qpos[:, None] - kpos[None, :], -B_MAX_REL, B_MAX_REL) + B_MAX_REL
    bias = rel_table.astype(jnp.float32)[:, rel]
    scores = scores + bias[None, :, None, :, :]
    valid = band_valid(nc, B_PREV_CHUNKS)[None, None, :, None, :]
    probs = jax.nn.softmax(jnp.where(valid, scores, NEG_INF), axis=-1)
    out = jnp.einsum('bhcql,bclhd->bcqhd', probs.astype(v.dtype), vb)
    return out.reshape(b, s, B_HEADS * HEAD_DIM)


def differential_attention(q1, q2, k1, k2, v, lam):
    b, s = q1.shape[0], q1.shape[1]
    nb = s // Q_BLOCK
    scale = 1.0 / math.sqrt(HEAD_DIM)
    kchunk = jnp.arange(s) // CHUNK

    def block(i):
        start = i * Q_BLOCK
        qb1 = lax.dynamic_slice_in_dim(q1, start, Q_BLOCK, axis=1)
        qb2 = lax.dynamic_slice_in_dim(q2, start, Q_BLOCK, axis=1)
        qchunk = (start + jnp.arange(Q_BLOCK)) // CHUNK
        mask = (kchunk[None, :] <= qchunk[:, None])[None, None]
        s1 = jnp.einsum('bqhd,bkhd->bhqk', qb1, k1).astype(jnp.float32) * scale
        s2 = jnp.einsum('bqhd,bkhd->bhqk', qb2, k2).astype(jnp.float32) * scale
        p1 = jax.nn.softmax(jnp.where(mask, s1, NEG_INF), axis=-1)
        p2 = jax.nn.softmax(jnp.where(mask, s2, NEG_INF), axis=-1)
        a = p1 - lam * p2
        return jnp.einsum('bhqk,bkhd->bqhd', a.astype(v.dtype), v)

    out = lax.map(block, jnp.arange(nb))
    return jnp.transpose(out, (1, 0, 2, 3, 4)).reshape(b, s, C_HEADS, C_V_DIM)


def even_layer(x, norm_g, w_in, w_out, a_qn, a_kn, a_sinks, b_qn, b_kn, b_rel, cos, sin):
    b, s, _ = x.shape
    h = rms_norm(x, norm_g)
    proj = h @ w_in
    cuts = [int(c) for c in np.cumsum(EVEN_SIZES)[:-1]]
    aq, ak, av, ag, bq, bk, bv, bg = jnp.split(proj, cuts, axis=-1)
    aq = apply_rope(rms_norm(aq.reshape(b, s, A_Q_HEADS, HEAD_DIM), a_qn), cos, sin)
    ak = apply_rope(rms_norm(ak.reshape(b, s, A_KV_HEADS, HEAD_DIM), a_kn), cos, sin)
    av = av.reshape(b, s, A_KV_HEADS, HEAD_DIM)
    ya = sliding_window_sink_attention(aq, ak, av, a_sinks) * jax.nn.silu(ag)
    bq = rms_norm(bq.reshape(b, s, B_HEADS, HEAD_DIM), b_qn)
    bk = rms_norm(bk.reshape(b, s, B_HEADS, HEAD_DIM), b_kn)
    bv = bv.reshape(b, s, B_HEADS, HEAD_DIM)
    yb = chunked_relbias_attention(bq, bk, bv, b_rel) * jax.nn.silu(bg)
    y = jnp.concatenate([ya, yb], axis=-1) @ w_out
    return x + y


def odd_layer(x, norm_g, w_in, w_out, qn, kn, lq1, lk1, lq2, lk2, subln_g, cos, sin, lambda_init):
    b, s, _ = x.shape
    h = rms_norm(x, norm_g)
    proj = h @ w_in
    q, k, v, gate = jnp.split(proj, [C_QK, 2 * C_QK, 2 * C_QK + C_W], axis=-1)
    q = rms_norm(q.reshape(b, s, C_HEADS, 2, HEAD_DIM), qn)
    k = rms_norm(k.reshape(b, s, C_HEADS, 2, HEAD_DIM), kn)
    q1 = apply_rope(q[:, :, :, 0], cos, sin)
    q2 = apply_rope(q[:, :, :, 1], cos, sin)
    k1 = apply_rope(k[:, :, :, 0], cos, sin)
    k2 = apply_rope(k[:, :, :, 1], cos, sin)
    v = v.reshape(b, s, C_HEADS, C_V_DIM)
    lam = (jnp.exp(jnp.sum(lq1.astype(jnp.float32) * lk1.astype(jnp.float32)))
           - jnp.exp(jnp.sum(lq2.astype(jnp.float32) * lk2.astype(jnp.float32)))
           + lambda_init)
    o = differential_attention(q1, q2, k1, k2, v, lam)
    o = rms_norm(o, subln_g) * (1.0 - lambda_init)
    y = (o.reshape(b, s, C_W) * jax.nn.silu(gate)) @ w_out
    return x + y


def setup_inputs(seed: int = 0) -> dict:
    key = jax.random.key(seed)
    ks = jax.random.split(key, 24)
    f32 = jnp.float32

    def gain(k, shape):
        return jnp.ones(shape, f32) + 0.02 * jax.random.normal(k, shape, f32)

    return {
        "x": jax.random.normal(ks[0], (BATCH, SEQ, D_MODEL), f32),
        "ev_norm": gain(ks[1], (N_EVEN, D_MODEL)),
        "ev_w_in": jax.random.normal(ks[2], (N_EVEN, D_MODEL, EVEN_IN), f32) * D_MODEL ** -0.5,
        "ev_w_out": jax.random.normal(ks[3], (N_EVEN, EVEN_MIX, D_MODEL), f32) * EVEN_MIX ** -0.5,
        "ev_a_q_norm": gain(ks[4], (N_EVEN, HEAD_DIM)),
        "ev_a_k_norm": gain(ks[5], (N_EVEN, HEAD_DIM)),
        "ev_a_sinks": jax.random.normal(ks[6], (N_EVEN, A_Q_HEADS), f32),
        "ev_b_q_norm": gain(ks[7], (N_EVEN, HEAD_DIM)),
        "ev_b_k_norm": gain(ks[8], (N_EVEN, HEAD_DIM)),
        "ev_b_rel_bias": 0.5 * jax.random.normal(ks[9], (N_EVEN, B_HEADS, 2 * B_MAX_REL + 1), f32),
        "od_norm": gain(ks[10], (N_ODD, D_MODEL)),
        "od_w_in": jax.random.normal(ks[11], (N_ODD, D_MODEL, ODD_IN), f32) * D_MODEL ** -0.5,
        "od_w_out": jax.random.normal(ks[12], (N_ODD, C_W, D_MODEL), f32) * C_W ** -0.5,
        "od_q_norm": gain(ks[13], (N_ODD, HEAD_DIM)),
        "od_k_norm": gain(ks[14], (N_ODD, HEAD_DIM)),
        "od_lambda_q1": 0.1 * jax.random.normal(ks[15], (N_ODD, HEAD_DIM), f32),
        "od_lambda_k1": 0.1 * jax.random.normal(ks[16], (N_ODD, HEAD_DIM), f32),
        "od_lambda_q2": 0.1 * jax.random.normal(ks[17], (N_ODD, HEAD_DIM), f32),
        "od_lambda_k2": 0.1 * jax.random.normal(ks[18], (N_ODD, HEAD_DIM), f32),
        "od_subln": gain(ks[19], (N_ODD, C_V_DIM)),
    }


def reference(x, ev_norm, ev_w_in, ev_w_out, ev_a_q_norm, ev_a_k_norm, ev_a_sinks,
              ev_b_q_norm, ev_b_k_norm, ev_b_rel_bias, od_norm, od_w_in, od_w_out,
              od_q_norm, od_k_norm, od_lambda_q1, od_lambda_k1, od_lambda_q2,
              od_lambda_k2, od_subln):
    cos, sin = rope_tables(x.shape[1])
    for layer in range(DEPTH):
        i = layer // 2
        if layer % 2 == 0:
            x = even_layer(x, ev_norm[i], ev_w_in[i], ev_w_out[i], ev_a_q_norm[i],
                           ev_a_k_norm[i], ev_a_sinks[i], ev_b_q_norm[i], ev_b_k_norm[i],
                           ev_b_rel_bias[i], cos, sin)
        else:
            lambda_init = 0.8 - 0.6 * math.exp(-0.3 * layer)
            x = odd_layer(x, od_norm[i], od_w_in[i], od_w_out[i], od_q_norm[i], od_k_norm[i],
                          od_lambda_q1[i], od_lambda_k1[i], od_lambda_q2[i], od_lambda_k2[i],
                          od_subln[i], cos, sin, lambda_init)
    return x
```

```python
import functools
import math

import jax
import jax.numpy as jnp
import numpy as np
from jax import lax
from jax.experimental import pallas as pl
from jax.experimental.pallas import tpu as pltpu

F32 = jnp.float32
BF16 = jnp.bfloat16

D_MODEL = 1024
CHUNK = 64
HEAD_DIM = 64
HALF_DIM = HEAD_DIM // 2
ROPE_THETA = 10000.0
EPS = 1e-6
NEG_INF = -1e30
QK_SCALE = 1.0 / math.sqrt(HEAD_DIM)

A_Q_HEADS = 8
A_KV_HEADS = 2
A_PREV_CHUNKS = 2
B_HEADS = 8
B_PREV_CHUNKS = 8
B_MAX_REL = 128
C_HEADS = 8
C_V_DIM = 2 * HEAD_DIM

A_Q = A_Q_HEADS * HEAD_DIM
A_KV = A_KV_HEADS * HEAD_DIM
B_W = B_HEADS * HEAD_DIM
EVEN_MIX = A_Q + B_W
C_QK = C_HEADS * 2 * HEAD_DIM
C_W = C_HEADS * C_V_DIM

V7X_VMEM_LIMIT_BYTES = 56 * 1024 * 1024

PROJ_TOKENS = 512
MIX_TOKENS = 256
A_PREV_ROWS = A_PREV_CHUNKS * CHUNK
B_PREV_ROWS = B_PREV_CHUNKS * CHUNK
DIFF_Q = 256
DIFF_K = 256


def _params(semantics):
    return pltpu.CompilerParams(dimension_semantics=semantics,
                                vmem_limit_bytes=V7X_VMEM_LIMIT_BYTES)


def _rms_rows(x, gain_row):
    ms = jnp.mean(x * x, axis=-1, keepdims=True)
    return x * lax.rsqrt(ms + EPS) * gain_row


def _proj_t(w_t_ref, lo, width, h):
    return lax.dot_general(w_t_ref[lo:lo + width, :], h,
                           (((1,), (1,)), ((), ())),
                           preferred_element_type=F32)


def _head_norm_t(p, gain_tile, cos=None, sin=None, scale=1.0):
    width, tm = p.shape
    y = p.reshape(width // HEAD_DIM, HEAD_DIM, tm)
    ms = jnp.mean(y * y, axis=1, keepdims=True)
    y = y * lax.rsqrt(ms + EPS) * gain_tile[None]
    if cos is not None:
        y1 = y[:, :HALF_DIM]
        y2 = y[:, HALF_DIM:]
        c = cos[None]
        s = sin[None]
        y = jnp.concatenate([y1 * c - y2 * s, y2 * c + y1 * s], axis=1)
    if scale != 1.0:
        y = y * scale
    return y.reshape(width, tm)


def _silu(x):
    return x * (1.0 / (1.0 + jnp.exp(-x)))


def _even_proj_kernel(x_ref, g_ref, w_t_ref, cos_ref, sin_ref,
                      aqn_ref, akn_ref, bqn_ref, bkn_ref,
                      aq_ref, ak_ref, av_ref, bq_ref, bk_ref, bv_ref, gate_ref):
    h = _rms_rows(x_ref[...], g_ref[...]).astype(BF16)
    cos = cos_ref[...]
    sin = sin_ref[...]
    lo = 0
    aq = _proj_t(w_t_ref, lo, A_Q, h); lo += A_Q
    aq_ref[0] = _head_norm_t(aq, aqn_ref[...], cos, sin, QK_SCALE).astype(BF16)
    ak = _proj_t(w_t_ref, lo, A_KV, h); lo += A_KV
    ak = _head_norm_t(ak, akn_ref[...], cos, sin)
    ak_ref[...] = ak.T.astype(BF16)
    av = _proj_t(w_t_ref, lo, A_KV, h); lo += A_KV
    av_ref[0] = av.astype(BF16)
    ag = _proj_t(w_t_ref, lo, A_Q, h); lo += A_Q
    gate_ref[0, :A_Q, :] = _silu(ag)
    bq = _proj_t(w_t_ref, lo, B_W, h); lo += B_W
    bq_ref[0] = _head_norm_t(bq, bqn_ref[...], scale=QK_SCALE).astype(BF16)
    bk = _proj_t(w_t_ref, lo, B_W, h); lo += B_W
    bk = _head_norm_t(bk, bkn_ref[...])
    bk_ref[...] = bk.T.astype(BF16)
    bv = _proj_t(w_t_ref, lo, B_W, h); lo += B_W
    bv_ref[0] = bv.astype(BF16)
    bg = _proj_t(w_t_ref, lo, B_W, h); lo += B_W
    gate_ref[0, A_Q:, :] = _silu(bg)


def _even_proj(x2d, gain, w_t, cos_t, sin_t, aqn, akn, bqn, bkn, batch, seq):
    tm = PROJ_TOKENS
    nt = seq // tm
    tokens = batch * seq
    even_in = w_t.shape[0]

    def tok(b, i):
        return (b * nt + i, 0)

    def feat(b, i):
        return (b, 0, i)

    def const2(b, i):
        return (0, 0)

    def tab(b, i):
        return (0, i)

    in_specs = [
        pl.BlockSpec((tm, D_MODEL), tok),
        pl.BlockSpec((1, D_MODEL), const2),
        pl.BlockSpec((even_in, D_MODEL), const2),
        pl.BlockSpec((HALF_DIM, tm), tab),
        pl.BlockSpec((HALF_DIM, tm), tab),
        pl.BlockSpec((HEAD_DIM, tm), const2),
        pl.BlockSpec((HEAD_DIM, tm), const2),
        pl.BlockSpec((HEAD_DIM, tm), const2),
        pl.BlockSpec((HEAD_DIM, tm), const2),
    ]
    out_shape = [
        jax.ShapeDtypeStruct((batch, A_Q, seq), BF16),
        jax.ShapeDtypeStruct((tokens, A_KV), BF16),
        jax.ShapeDtypeStruct((batch, A_KV, seq), BF16),
        jax.ShapeDtypeStruct((batch, B_W, seq), BF16),
        jax.ShapeDtypeStruct((tokens, B_W), BF16),
        jax.ShapeDtypeStruct((batch, B_W, seq), BF16),
        jax.ShapeDtypeStruct((batch, EVEN_MIX, seq), F32),
    ]
    out_specs = [
        pl.BlockSpec((1, A_Q, tm), feat),
        pl.BlockSpec((tm, A_KV), tok),
        pl.BlockSpec((1, A_KV, tm), feat),
        pl.BlockSpec((1, B_W, tm), feat),
        pl.BlockSpec((tm, B_W), tok),
        pl.BlockSpec((1, B_W, tm), feat),
        pl.BlockSpec((1, EVEN_MIX, tm), feat),
    ]
    return pl.pallas_call(
        _even_proj_kernel,
        grid=(batch, nt),
        in_specs=in_specs,
        out_specs=out_specs,
        out_shape=out_shape,
        compiler_params=_params(("parallel", "parallel")),
        name="even_proj",
    )(x2d, gain, w_t, cos_t, sin_t, aqn, akn, bqn, bkn)


def _band_valid(n_keys, n_q, prev_rows, q_start):
    key = lax.broadcasted_iota(jnp.int32, (n_keys, n_q), 0)
    qry = lax.broadcasted_iota(jnp.int32, (n_keys, n_q), 1)
    key_chunk = key // CHUNK
    qry_chunk = qry // CHUNK + prev_rows // CHUNK
    return ((key_chunk <= qry_chunk)
            & (key_chunk >= qry_chunk - prev_rows // CHUNK)
            & (key >= prev_rows - q_start))


def _band_attend(k_win, q_pad, v_t, valid, bias, sink):
    s = jnp.dot(k_win, q_pad, preferred_element_type=F32)
    if bias is not None:
        s = s + bias
    s = jnp.where(valid, s, NEG_INF)
    m = jnp.max(s, axis=0, keepdims=True)
    if sink is not None:
        m = jnp.maximum(m, sink)
    p = jnp.exp(s - m)
    l = jnp.sum(p, axis=0, keepdims=True)
    if sink is not None:
        l = l + jnp.exp(sink - m)
    o = jnp.dot(v_t, p.astype(BF16), preferred_element_type=F32)
    return o * (1.0 / l)


def _pad_q(q_h, half):
    z = jnp.zeros_like(q_h)
    return jnp.concatenate([q_h, z] if half == 0 else [z, q_h], axis=0)


def _even_mix_kernel(sinks_ref, aq_ref, akp_ref, akc_ref, avp_ref, avc_ref,
                     bq_ref, bkp2_ref, bkp1_ref, bkc_ref, bvp2_ref, bvp1_ref, bvc_ref,
                     gate_ref, bias_ref, y_ref):
    tq = MIX_TOKENS
    q_start = pl.program_id(1) * tq

    ak = jnp.concatenate([akp_ref[0], akc_ref[0]], axis=0)
    av = jnp.concatenate([avp_ref[0], avc_ref[0]], axis=1)
    valid_a = _band_valid(A_PREV_ROWS + tq, tq, A_PREV_ROWS, q_start)
    group = A_Q_HEADS // A_KV_HEADS
    for h in range(A_Q_HEADS):
        kv = h // group
        q_pad = _pad_q(aq_ref[0, h * HEAD_DIM:(h + 1) * HEAD_DIM, :], kv)
        o = _band_attend(ak, q_pad, av[kv * HEAD_DIM:(kv + 1) * HEAD_DIM, :],
                         valid_a, None, sinks_ref[h])
        rows = slice(h * HEAD_DIM, (h + 1) * HEAD_DIM)
        y_ref[0, rows, :] = (o * gate_ref[0, rows, :]).astype(BF16)

    bk = jnp.concatenate([bkp2_ref[0], bkp1_ref[0], bkc_ref[0]], axis=0)
    bv = jnp.concatenate([bvp2_ref[0], bvp1_ref[0], bvc_ref[0]], axis=1)
    valid_b = _band_valid(B_PREV_ROWS + tq, tq, B_PREV_ROWS, q_start)
    for h in range(B_HEADS):
        pair = h // 2
        q_pad = _pad_q(bq_ref[0, h * HEAD_DIM:(h + 1) * HEAD_DIM, :], h % 2)
        o = _band_attend(bk[:, pair * 128:(pair + 1) * 128], q_pad,
                         bv[h * HEAD_DIM:(h + 1) * HEAD_DIM, :],
                         valid_b, bias_ref[h], None)
        rows = slice(A_Q + h * HEAD_DIM, A_Q + (h + 1) * HEAD_DIM)
        y_ref[0, rows, :] = (o * gate_ref[0, rows, :]).astype(BF16)


def _even_mix(sinks, aq_t, ak, av_t, bq_t, bk, bv_t, gate_t, bias_t, batch, seq):
    tq = MIX_TOKENS
    nq = seq // tq
    ak3 = ak.reshape(batch, seq, A_KV)
    bk3 = bk.reshape(batch, seq, B_W)
    a_ratio = tq // A_PREV_ROWS

    def cur_t(b, i, *_):
        return (b, 0, i)

    def cur_n(b, i, *_):
        return (b, i, 0)

    def a_prev_n(b, i, *_):
        return (b, jnp.maximum(i * a_ratio - 1, 0), 0)

    def a_prev_t(b, i, *_):
        return (b, 0, jnp.maximum(i * a_ratio - 1, 0))

    def b_prev_n(back):
        return lambda b, i, *_: (b, jnp.maximum(i - back, 0), 0)

    def b_prev_t(back):
        return lambda b, i, *_: (b, 0, jnp.maximum(i - back, 0))

    in_specs = [
        pl.BlockSpec((1, A_Q, tq), cur_t),
        pl.BlockSpec((1, A_PREV_ROWS, A_KV), a_prev_n),
        pl.BlockSpec((1, tq, A_KV), cur_n),
        pl.BlockSpec((1, A_KV, A_PREV_ROWS), a_prev_t),
        pl.BlockSpec((1, A_KV, tq), cur_t),
        pl.BlockSpec((1, B_W, tq), cur_t),
        pl.BlockSpec((1, tq, B_W), b_prev_n(2)),
        pl.BlockSpec((1, tq, B_W), b_prev_n(1)),
        pl.BlockSpec((1, tq, B_W), cur_n),
        pl.BlockSpec((1, B_W, tq), b_prev_t(2)),
        pl.BlockSpec((1, B_W, tq), b_prev_t(1)),
        pl.BlockSpec((1, B_W, tq), cur_t),
        pl.BlockSpec((1, EVEN_MIX, tq), cur_t),
        pl.BlockSpec((B_HEADS, B_PREV_ROWS + tq, tq), lambda b, i, *_: (0, 0, 0)),
    ]
    grid_spec = pltpu.PrefetchScalarGridSpec(
        num_scalar_prefetch=1,
        grid=(batch, nq),
        in_specs=in_specs,
        out_specs=pl.BlockSpec((1, EVEN_MIX, tq), cur_t),
    )
    return pl.pallas_call(
        _even_mix_kernel,
        grid_spec=grid_spec,
        out_shape=jax.ShapeDtypeStruct((batch, EVEN_MIX, seq), BF16),
        compiler_params=_params(("parallel", "parallel")),
        name="even_mix",
    )(sinks, aq_t, ak3, ak3, av_t, av_t, bq_t, bk3, bk3, bk3, bv_t, bv_t, bv_t,
      gate_t, bias_t)


def _out_proj_kernel(x_ref, y_ref, w_t_ref, o_ref):
    out_t = jnp.dot(w_t_ref[...], y_ref[0], preferred_element_type=F32)
    o_ref[...] = x_ref[...] + out_t.T


def _out_proj(x2d, y_t, w_t, batch, seq):
    tm = PROJ_TOKENS
    nt = seq // tm
    width = y_t.shape[1]
    return pl.pallas_call(
        _out_proj_kernel,
        grid=(batch, nt),
        in_specs=[
            pl.BlockSpec((tm, D_MODEL), lambda b, i: (b * nt + i, 0)),
            pl.BlockSpec((1, width, tm), lambda b, i: (b, 0, i)),
            pl.BlockSpec((D_MODEL, width), lambda b, i: (0, 0)),
        ],
        out_specs=pl.BlockSpec((tm, D_MODEL), lambda b, i: (b * nt + i, 0)),
        out_shape=jax.ShapeDtypeStruct(x2d.shape, F32),
        compiler_params=_params(("parallel", "parallel")),
        name="out_proj",
    )(x2d, y_t, w_t)


def _odd_proj_kernel(x_ref, g_ref, w_t_ref, cos_ref, sin_ref, qn_ref, kn_ref,
                     q_ref, k_ref, v_ref, gate_ref):
    h = _rms_rows(x_ref[...], g_ref[...]).astype(BF16)
    cos = cos_ref[...]
    sin = sin_ref[...]
    q = _proj_t(w_t_ref, 0, C_QK, h)
    q_ref[0] = _head_norm_t(q, qn_ref[...], cos, sin, QK_SCALE).astype(BF16)
    k = _proj_t(w_t_ref, C_QK, C_QK, h)
    k = _head_norm_t(k, kn_ref[...], cos, sin)
    k_ref[...] = k.T.astype(BF16)
    v = _proj_t(w_t_ref, 2 * C_QK, C_W, h)
    v_ref[0] = v.astype(BF16)
    g = _proj_t(w_t_ref, 2 * C_QK + C_W, C_W, h)
    gate_ref[0] = _silu(g)


def _odd_proj(x2d, gain, w_t, cos_t, sin_t, qn, kn, batch, seq):
    tm = PROJ_TOKENS
    nt = seq // tm
    tokens = batch * seq
    odd_in = w_t.shape[0]

    def tok(b, i):
        return (b * nt + i, 0)

    def feat(b, i):
        return (b, 0, i)

    def const2(b, i):
        return (0, 0)

    def tab(b, i):
        return (0, i)

    return pl.pallas_call(
        _odd_proj_kernel,
        grid=(batch, nt),
        in_specs=[
            pl.BlockSpec((tm, D_MODEL), tok),
            pl.BlockSpec((1, D_MODEL), const2),
            pl.BlockSpec((odd_in, D_MODEL), const2),
            pl.BlockSpec((HALF_DIM, tm), tab),
            pl.BlockSpec((HALF_DIM, tm), tab),
            pl.BlockSpec((HEAD_DIM, tm), const2),
            pl.BlockSpec((HEAD_DIM, tm), const2),
        ],
        out_specs=[
            pl.BlockSpec((1, C_QK, tm), feat),
            pl.BlockSpec((tm, C_QK), tok),
            pl.BlockSpec((1, C_W, tm), feat),
            pl.BlockSpec((1, C_W, tm), feat),
        ],
        out_shape=[
            jax.ShapeDtypeStruct((batch, C_QK, seq), BF16),
            jax.ShapeDtypeStruct((tokens, C_QK), BF16),
            jax.ShapeDtypeStruct((batch, C_W, seq), BF16),
            jax.ShapeDtypeStruct((batch, C_W, seq), F32),
        ],
        compiler_params=_params(("parallel", "parallel")),
        name="odd_proj",
    )(x2d, gain, w_t, cos_t, sin_t, qn, kn)


def _diff_attn_kernel(lambda_init, q_ref, k_ref, v_ref, gate_ref, subln_ref,
                      lq1_ref, lk1_ref, lq2_ref, lk2_ref, y_ref, acc1_ref, acc2_ref):
    tq, tk = DIFF_Q, DIFF_K
    i = pl.program_id(2)
    q = q_ref[0]
    row = lax.broadcasted_iota(jnp.int32, q.shape, 0)
    zero = jnp.zeros_like(q)
    q_pads = (jnp.where(row < HEAD_DIM, q, zero), jnp.where(row >= HEAD_DIM, q, zero))
    acc_refs = (acc1_ref, acc2_ref)
    acc1_ref[...] = jnp.zeros_like(acc1_ref)
    acc2_ref[...] = jnp.zeros_like(acc2_ref)

    def step(j, carry, masked):
        start = pl.multiple_of(j * tk, tk)
        k = k_ref[0, pl.ds(start, tk), :]
        v_t = v_ref[0, :, pl.ds(start, tk)]
        if masked:
            key_chunk = lax.broadcasted_iota(jnp.int32, (tk, tq), 0) // CHUNK
            qry_chunk = lax.broadcasted_iota(jnp.int32, (tk, tq), 1) // CHUNK
            valid = key_chunk <= qry_chunk
        new = []
        for c in range(2):
            m, l = carry[2 * c], carry[2 * c + 1]
            s = jnp.dot(k, q_pads[c], preferred_element_type=F32)
            if masked:
                s = jnp.where(valid, s, NEG_INF)
            m_new = jnp.maximum(m, jnp.max(s, axis=0, keepdims=True))
            alpha = jnp.exp(m - m_new)
            p = jnp.exp(s - m_new)
            l_new = alpha * l + jnp.sum(p, axis=0, keepdims=True)
            pv = jnp.dot(v_t, p.astype(BF16), preferred_element_type=F32)
            acc_refs[c][...] = alpha * acc_refs[c][...] + pv
            new += [m_new, l_new]
        return tuple(new)

    init = (jnp.full((1, tq), NEG_INF, F32), jnp.zeros((1, tq), F32)) * 2
    carry = lax.fori_loop(0, i, lambda j, c: step(j, c, False), init)
    _, l1, _, l2 = step(i, carry, True)

    lam = (jnp.exp(jnp.sum(lq1_ref[...] * lk1_ref[...], axis=-1, keepdims=True))
           - jnp.exp(jnp.sum(lq2_ref[...] * lk2_ref[...], axis=-1, keepdims=True))
           + lambda_init)
    o = acc1_ref[...] * (1.0 / l1) - lam * (acc2_ref[...] * (1.0 / l2))
    ms = jnp.mean(o * o, axis=0, keepdims=True)
    o = o * lax.rsqrt(ms + EPS) * subln_ref[...] * (1.0 - lambda_init)
    y_ref[0] = (o * gate_ref[0]).astype(BF16)


def _diff_attn(q_t, k, v_t, gate_t, subln_tile, lq1, lk1, lq2, lk2, lambda_init, batch, seq):
    tq = DIFF_Q
    nq = seq // tq
    k3 = k.reshape(batch, seq, C_QK)
    head_w = 2 * HEAD_DIM

    def blk(b, h, i):
        return (b, h, i)

    def const2(b, h, i):
        return (0, 0)

    return pl.pallas_call(
        functools.partial(_diff_attn_kernel, lambda_init),
        grid=(batch, C_HEADS, nq),
        in_specs=[
            pl.BlockSpec((1, head_w, tq), blk),
            pl.BlockSpec((1, seq, head_w), lambda b, h, i: (b, 0, h)),
            pl.BlockSpec((1, C_V_DIM, seq), lambda b, h, i: (b, h, 0)),
            pl.BlockSpec((1, C_V_DIM, tq), blk),
            pl.BlockSpec((C_V_DIM, tq), const2),
            pl.BlockSpec((1, HEAD_DIM), const2),
            pl.BlockSpec((1, HEAD_DIM), const2),
            pl.BlockSpec((1, HEAD_DIM), const2),
            pl.BlockSpec((1, HEAD_DIM), const2),
        ],
        out_specs=pl.BlockSpec((1, C_V_DIM, tq), blk),
        out_shape=jax.ShapeDtypeStruct((batch, C_W, seq), BF16),
        scratch_shapes=[pltpu.VMEM((C_V_DIM, tq), F32), pltpu.VMEM((C_V_DIM, tq), F32)],
        compiler_params=_params(("parallel", "parallel", "arbitrary")),
        name="diff_attn",
    )(q_t, k3, v_t, gate_t, subln_tile, lq1, lk1, lq2, lk2)


def _rope_tables_t(seq):
    inv = 1.0 / (ROPE_THETA ** (jnp.arange(0, HEAD_DIM, 2, dtype=F32) / HEAD_DIM))
    ang = inv[:, None] * jnp.arange(seq, dtype=F32)[None, :]
    return jnp.cos(ang), jnp.sin(ang)


def _gain_tile(g, width):
    return jnp.broadcast_to(g.astype(F32)[:, None], (g.shape[0], width))


def _rel_bias_t(rel_table, n_q):
    key = np.arange(B_PREV_ROWS + n_q)[:, None]
    qry = np.arange(n_q)[None, :] + B_PREV_ROWS
    rel = np.clip(qry - key, -B_MAX_REL, B_MAX_REL) + B_MAX_REL
    return rel_table.astype(F32)[:, rel]


def kernel(x, ev_norm, ev_w_in, ev_w_out, ev_a_q_norm, ev_a_k_norm, ev_a_sinks,
           ev_b_q_norm, ev_b_k_norm, ev_b_rel_bias, od_norm, od_w_in, od_w_out,
           od_q_norm, od_k_norm, od_lambda_q1, od_lambda_k1, od_lambda_q2,
           od_lambda_k2, od_subln):
    batch, seq, d = x.shape
    x2d = x.reshape(batch * seq, d)
    cos_t, sin_t = _rope_tables_t(seq)
    depth = ev_norm.shape[0] + od_norm.shape[0]
    for layer in range(depth):
        i = layer // 2
        if layer % 2 == 0:
            w_in_t = ev_w_in[i].T.astype(BF16)
            w_out_t = ev_w_out[i].T.astype(BF16)
            aq_t, ak, av_t, bq_t, bk, bv_t, gate_t = _even_proj(
                x2d, ev_norm[i][None, :], w_in_t, cos_t, sin_t,
                _gain_tile(ev_a_q_norm[i], PROJ_TOKENS), _gain_tile(ev_a_k_norm[i], PROJ_TOKENS),
                _gain_tile(ev_b_q_norm[i], PROJ_TOKENS), _gain_tile(ev_b_k_norm[i], PROJ_TOKENS),
                batch, seq)
            y_t = _even_mix(ev_a_sinks[i].astype(F32), aq_t, ak, av_t, bq_t, bk, bv_t, gate_t,
                            _rel_bias_t(ev_b_rel_bias[i], MIX_TOKENS), batch, seq)
            x2d = _out_proj(x2d, y_t, w_out_t, batch, seq)
        else:
            lambda_init = 0.8 - 0.6 * math.exp(-0.3 * layer)
            w_in_t = od_w_in[i].T.astype(BF16)
            w_out_t = od_w_out[i].T.astype(BF16)
            q_t, k, v_t, gate_t = _odd_proj(
                x2d, od_norm[i][None, :], w_in_t, cos_t, sin_t,
                _gain_tile(od_q_norm[i], PROJ_TOKENS), _gain_tile(od_k_norm[i], PROJ_TOKENS),
                batch, seq)
            y_t = _diff_attn(q_t, k, v_t, gate_t, _gain_tile(od_subln[i], DIFF_Q),
                             od_lambda_q1[i][None, :], od_lambda_k1[i][None, :],
                             od_lambda_q2[i][None, :], od_lambda_k2[i][None, :],
                             lambda_init, batch, seq)
            x2d = _out_proj(x2d, y_t, w_out_t, batch, seq)
    return x2d.reshape(batch, seq, d)
```

```python
import functools
import math

import jax
import jax.numpy as jnp
import numpy as np
from jax import lax
from jax.experimental import pallas as pl
from jax.experimental.pallas import tpu as pltpu

F32 = jnp.float32
BF16 = jnp.bfloat16

D_MODEL = 1024
CHUNK = 64
HEAD_DIM = 64
HALF_DIM = HEAD_DIM // 2
ROPE_THETA = 10000.0
EPS = 1e-6
NEG_INF = -1e30
QK_SCALE = 1.0 / math.sqrt(HEAD_DIM)

A_Q_HEADS = 8
A_KV_HEADS = 2
A_PREV_CHUNKS = 2
B_HEADS = 8
B_PREV_CHUNKS = 8
B_MAX_REL = 128
C_HEADS = 8
C_V_DIM = 2 * HEAD_DIM

A_Q = A_Q_HEADS * HEAD_DIM
A_KV = A_KV_HEADS * HEAD_DIM
B_W = B_HEADS * HEAD_DIM
EVEN_MIX = A_Q + B_W
C_QK = C_HEADS * 2 * HEAD_DIM
C_W = C_HEADS * C_V_DIM

V7X_VMEM_LIMIT_BYTES = 56 * 1024 * 1024

PROJ_TOKENS = 512
MIX_TOKENS = 256
A_PREV_ROWS = A_PREV_CHUNKS * CHUNK
B_PREV_ROWS = B_PREV_CHUNKS * CHUNK
DIFF_Q = 512
DIFF_K = 512
DIFF_SAFE_SCORE = 40.0


def _params(semantics):
    return pltpu.CompilerParams(dimension_semantics=semantics,
                                vmem_limit_bytes=V7X_VMEM_LIMIT_BYTES)


def _rms_rows(x, gain_row):
    ms = jnp.mean(x * x, axis=-1, keepdims=True)
    return x * lax.rsqrt(ms + EPS) * gain_row


def _proj_t(w_t_ref, lo, width, h):
    return lax.dot_general(w_t_ref[lo:lo + width, :], h,
                           (((1,), (1,)), ((), ())),
                           preferred_element_type=F32)


def _head_norm_t(p, gain_tile, cos=None, sin=None, scale=1.0):
    width, tm = p.shape
    y = p.reshape(width // HEAD_DIM, HEAD_DIM, tm)
    ms = jnp.mean(y * y, axis=1, keepdims=True)
    y = y * lax.rsqrt(ms + EPS) * gain_tile[None]
    if cos is not None:
        y1 = y[:, :HALF_DIM]
        y2 = y[:, HALF_DIM:]
        c = cos[None]
        s = sin[None]
        y = jnp.concatenate([y1 * c - y2 * s, y2 * c + y1 * s], axis=1)
    if scale != 1.0:
        y = y * scale
    return y.reshape(width, tm)


def _silu(x):
    return x * (1.0 / (1.0 + jnp.exp(-x)))


def _even_proj_kernel(x_ref, g_ref, w_t_ref, cos_ref, sin_ref,
                      aqn_ref, akn_ref, bqn_ref, bkn_ref,
                      aq_ref, ak_ref, av_ref, bq_ref, bk_ref, bv_ref, gate_ref):
    h = _rms_rows(x_ref[...], g_ref[...]).astype(BF16)
    cos = cos_ref[...]
    sin = sin_ref[...]
    lo = 0
    aq = _proj_t(w_t_ref, lo, A_Q, h); lo += A_Q
    aq_ref[0] = _head_norm_t(aq, aqn_ref[...], cos, sin, QK_SCALE).astype(BF16)
    ak = _proj_t(w_t_ref, lo, A_KV, h); lo += A_KV
    ak = _head_norm_t(ak, akn_ref[...], cos, sin)
    ak_ref[...] = ak.T.astype(BF16)
    av = _proj_t(w_t_ref, lo, A_KV, h); lo += A_KV
    av_ref[0] = av.astype(BF16)
    ag = _proj_t(w_t_ref, lo, A_Q, h); lo += A_Q
    gate_ref[0, :A_Q, :] = _silu(ag)
    bq = _proj_t(w_t_ref, lo, B_W, h); lo += B_W
    bq_ref[0] = _head_norm_t(bq, bqn_ref[...], scale=QK_SCALE).astype(BF16)
    bk = _proj_t(w_t_ref, lo, B_W, h); lo += B_W
    bk = _head_norm_t(bk, bkn_ref[...])
    bk_ref[...] = bk.T.astype(BF16)
    bv = _proj_t(w_t_ref, lo, B_W, h); lo += B_W
    bv_ref[0] = bv.astype(BF16)
    bg = _proj_t(w_t_ref, lo, B_W, h); lo += B_W
    gate_ref[0, A_Q:, :] = _silu(bg)


def _even_proj(x2d, gain, w_t, cos_t, sin_t, aqn, akn, bqn, bkn, batch, seq):
    tm = PROJ_TOKENS
    nt = seq // tm
    tokens = batch * seq
    even_in = w_t.shape[0]

    def tok(b, i):
        return (b * nt + i, 0)

    def feat(b, i):
        return (b, 0, i)

    def const2(b, i):
        return (0, 0)

    def tab(b, i):
        return (0, i)

    in_specs = [
        pl.BlockSpec((tm, D_MODEL), tok),
        pl.BlockSpec((1, D_MODEL), const2),
        pl.BlockSpec((even_in, D_MODEL), const2),
        pl.BlockSpec((HALF_DIM, tm), tab),
        pl.BlockSpec((HALF_DIM, tm), tab),
        pl.BlockSpec((HEAD_DIM, tm), const2),
        pl.BlockSpec((HEAD_DIM, tm), const2),
        pl.BlockSpec((HEAD_DIM, tm), const2),
        pl.BlockSpec((HEAD_DIM, tm), const2),
    ]
    out_shape = [
        jax.ShapeDtypeStruct((batch, A_Q, seq), BF16),
        jax.ShapeDtypeStruct((tokens, A_KV), BF16),
        jax.ShapeDtypeStruct((batch, A_KV, seq), BF16),
        jax.ShapeDtypeStruct((batch, B_W, seq), BF16),
        jax.ShapeDtypeStruct((tokens, B_W), BF16),
        jax.ShapeDtypeStruct((batch, B_W, seq), BF16),
        jax.ShapeDtypeStruct((batch, EVEN_MIX, seq), F32),
    ]
    out_specs = [
        pl.BlockSpec((1, A_Q, tm), feat),
        pl.BlockSpec((tm, A_KV), tok),
        pl.BlockSpec((1, A_KV, tm), feat),
        pl.BlockSpec((1, B_W, tm), feat),
        pl.BlockSpec((tm, B_W), tok),
        pl.BlockSpec((1, B_W, tm), feat),
        pl.BlockSpec((1, EVEN_MIX, tm), feat),
    ]
    return pl.pallas_call(
        _even_proj_kernel,
        grid=(batch, nt),
        in_specs=in_specs,
        out_specs=out_specs,
        out_shape=out_shape,
        compiler_params=_params(("parallel", "parallel")),
        name="even_proj",
    )(x2d, gain, w_t, cos_t, sin_t, aqn, akn, bqn, bkn)


def _band_valid(n_keys, n_q, prev_rows, q_start):
    key = lax.broadcasted_iota(jnp.int32, (n_keys, n_q), 0)
    qry = lax.broadcasted_iota(jnp.int32, (n_keys, n_q), 1)
    key_chunk = key // CHUNK
    qry_chunk = qry // CHUNK + prev_rows // CHUNK
    return ((key_chunk <= qry_chunk)
            & (key_chunk >= qry_chunk - prev_rows // CHUNK)
            & (key >= prev_rows - q_start))


def _band_attend(k_win, q_pad, v_t, valid, bias, sink):
    s = jnp.dot(k_win, q_pad, preferred_element_type=F32)
    if bias is not None:
        s = s + bias
    s = jnp.where(valid, s, NEG_INF)
    m = jnp.max(s, axis=0, keepdims=True)
    if sink is not None:
        m = jnp.maximum(m, sink)
    p = jnp.exp(s - m)
    l = jnp.sum(p, axis=0, keepdims=True)
    if sink is not None:
        l = l + jnp.exp(sink - m)
    o = jnp.dot(v_t, p.astype(BF16), preferred_element_type=F32)
    return o * (1.0 / l)


def _pad_q(q_h, half):
    z = jnp.zeros_like(q_h)
    return jnp.concatenate([q_h, z] if half == 0 else [z, q_h], axis=0)


def _even_mix_kernel(sinks_ref, aq_ref, akp_ref, akc_ref, avp_ref, avc_ref,
                     bq_ref, bkp2_ref, bkp1_ref, bkc_ref, bvp2_ref, bvp1_ref, bvc_ref,
                     gate_ref, bias_ref, y_ref):
    tq = MIX_TOKENS
    q_start = pl.program_id(1) * tq

    ak = jnp.concatenate([akp_ref[0], akc_ref[0]], axis=0)
    av = jnp.concatenate([avp_ref[0], avc_ref[0]], axis=1)
    valid_a = _band_valid(A_PREV_ROWS + tq, tq, A_PREV_ROWS, q_start)
    group = A_Q_HEADS // A_KV_HEADS
    for h in range(A_Q_HEADS):
        kv = h // group
        q_pad = _pad_q(aq_ref[0, h * HEAD_DIM:(h + 1) * HEAD_DIM, :], kv)
        o = _band_attend(ak, q_pad, av[kv * HEAD_DIM:(kv + 1) * HEAD_DIM, :],
                         valid_a, None, sinks_ref[h])
        rows = slice(h * HEAD_DIM, (h + 1) * HEAD_DIM)
        y_ref[0, rows, :] = (o * gate_ref[0, rows, :]).astype(BF16)

    bk = jnp.concatenate([bkp2_ref[0], bkp1_ref[0], bkc_ref[0]], axis=0)
    bv = jnp.concatenate([bvp2_ref[0], bvp1_ref[0], bvc_ref[0]], axis=1)
    valid_b = _band_valid(B_PREV_ROWS + tq, tq, B_PREV_ROWS, q_start)
    for h in range(B_HEADS):
        pair = h // 2
        q_pad = _pad_q(bq_ref[0, h * HEAD_DIM:(h + 1) * HEAD_DIM, :], h % 2)
        o = _band_attend(bk[:, pair * 128:(pair + 1) * 128], q_pad,
                         bv[h * HEAD_DIM:(h + 1) * HEAD_DIM, :],
                         valid_b, bias_ref[h], None)
        rows = slice(A_Q + h * HEAD_DIM, A_Q + (h + 1) * HEAD_DIM)
        y_ref[0, rows, :] = (o * gate_ref[0, rows, :]).astype(BF16)


def _even_mix(sinks, aq_t, ak, av_t, bq_t, bk, bv_t, gate_t, bias_t, batch, seq):
    tq = MIX_TOKENS
    nq = seq // tq
    ak3 = ak.reshape(batch, seq, A_KV)
    bk3 = bk.reshape(batch, seq, B_W)
    a_ratio = tq // A_PREV_ROWS

    def cur_t(b, i, *_):
        return (b, 0, i)

    def cur_n(b, i, *_):
        return (b, i, 0)

    def a_prev_n(b, i, *_):
        return (b, jnp.maximum(i * a_ratio - 1, 0), 0)

    def a_prev_t(b, i, *_):
        return (b, 0, jnp.maximum(i * a_ratio - 1, 0))

    def b_prev_n(back):
        return lambda b, i, *_: (b, jnp.maximum(i - back, 0), 0)

    def b_prev_t(back):
        return lambda b, i, *_: (b, 0, jnp.maximum(i - back, 0))

    in_specs = [
        pl.BlockSpec((1, A_Q, tq), cur_t),
        pl.BlockSpec((1, A_PREV_ROWS, A_KV), a_prev_n),
        pl.BlockSpec((1, tq, A_KV), cur_n),
        pl.BlockSpec((1, A_KV, A_PREV_ROWS), a_prev_t),
        pl.BlockSpec((1, A_KV, tq), cur_t),
        pl.BlockSpec((1, B_W, tq), cur_t),
        pl.BlockSpec((1, tq, B_W), b_prev_n(2)),
        pl.BlockSpec((1, tq, B_W), b_prev_n(1)),
        pl.BlockSpec((1, tq, B_W), cur_n),
        pl.BlockSpec((1, B_W, tq), b_prev_t(2)),
        pl.BlockSpec((1, B_W, tq), b_prev_t(1)),
        pl.BlockSpec((1, B_W, tq), cur_t),
        pl.BlockSpec((1, EVEN_MIX, tq), cur_t),
        pl.BlockSpec((B_HEADS, B_PREV_ROWS + tq, tq), lambda b, i, *_: (0, 0, 0)),
    ]
    grid_spec = pltpu.PrefetchScalarGridSpec(
        num_scalar_prefetch=1,
        grid=(batch, nq),
        in_specs=in_specs,
        out_specs=pl.BlockSpec((1, EVEN_MIX, tq), cur_t),
    )
    return pl.pallas_call(
        _even_mix_kernel,
        grid_spec=grid_spec,
        out_shape=jax.ShapeDtypeStruct((batch, EVEN_MIX, seq), BF16),
        compiler_params=_params(("parallel", "parallel")),
        name="even_mix",
    )(sinks, aq_t, ak3, ak3, av_t, av_t, bq_t, bk3, bk3, bk3, bv_t, bv_t, bv_t,
      gate_t, bias_t)


def _out_proj_kernel(x_ref, y_ref, w_t_ref, o_ref):
    out_t = jnp.dot(w_t_ref[...], y_ref[0], preferred_element_type=F32)
    o_ref[...] = x_ref[...] + out_t.T


def _out_proj(x2d, y_t, w_t, batch, seq):
    tm = PROJ_TOKENS
    nt = seq // tm
    width = y_t.shape[1]
    return pl.pallas_call(
        _out_proj_kernel,
        grid=(batch, nt),
        in_specs=[
            pl.BlockSpec((tm, D_MODEL), lambda b, i: (b * nt + i, 0)),
            pl.BlockSpec((1, width, tm), lambda b, i: (b, 0, i)),
            pl.BlockSpec((D_MODEL, width), lambda b, i: (0, 0)),
        ],
        out_specs=pl.BlockSpec((tm, D_MODEL), lambda b, i: (b * nt + i, 0)),
        out_shape=jax.ShapeDtypeStruct(x2d.shape, F32),
        compiler_params=_params(("parallel", "parallel")),
        name="out_proj",
    )(x2d, y_t, w_t)


def _odd_proj_kernel(x_ref, g_ref, w_t_ref, cos_ref, sin_ref, qn_ref, kn_ref,
                     q_ref, k_ref, v_ref, gate_ref):
    h = _rms_rows(x_ref[...], g_ref[...]).astype(BF16)
    cos = cos_ref[...]
    sin = sin_ref[...]
    q = _proj_t(w_t_ref, 0, C_QK, h)
    q_ref[0] = _head_norm_t(q, qn_ref[...], cos, sin, QK_SCALE).astype(BF16)
    k = _proj_t(w_t_ref, C_QK, C_QK, h)
    k = _head_norm_t(k, kn_ref[...], cos, sin)
    k_ref[...] = k.T.astype(BF16)
    v = _proj_t(w_t_ref, 2 * C_QK, C_W, h)
    v_ref[0] = v.astype(BF16)
    g = _proj_t(w_t_ref, 2 * C_QK + C_W, C_W, h)
    gate_ref[0] = _silu(g)


def _odd_proj(x2d, gain, w_t, cos_t, sin_t, qn, kn, batch, seq):
    tm = PROJ_TOKENS
    nt = seq // tm
    tokens = batch * seq
    odd_in = w_t.shape[0]

    def tok(b, i):
        return (b * nt + i, 0)

    def feat(b, i):
        return (b, 0, i)

    def const2(b, i):
        return (0, 0)

    def tab(b, i):
        return (0, i)

    return pl.pallas_call(
        _odd_proj_kernel,
        grid=(batch, nt),
        in_specs=[
            pl.BlockSpec((tm, D_MODEL), tok),
            pl.BlockSpec((1, D_MODEL), const2),
            pl.BlockSpec((odd_in, D_MODEL), const2),
            pl.BlockSpec((HALF_DIM, tm), tab),
            pl.BlockSpec((HALF_DIM, tm), tab),
            pl.BlockSpec((HEAD_DIM, tm), const2),
            pl.BlockSpec((HEAD_DIM, tm), const2),
        ],
        out_specs=[
            pl.BlockSpec((1, C_QK, tm), feat),
            pl.BlockSpec((tm, C_QK), tok),
            pl.BlockSpec((1, C_W, tm), feat),
            pl.BlockSpec((1, C_W, tm), feat),
        ],
        out_shape=[
            jax.ShapeDtypeStruct((batch, C_QK, seq), BF16),
            jax.ShapeDtypeStruct((tokens, C_QK), BF16),
            jax.ShapeDtypeStruct((batch, C_W, seq), BF16),
            jax.ShapeDtypeStruct((batch, C_W, seq), F32),
        ],
        compiler_params=_params(("parallel", "parallel")),
        name="odd_proj",
    )(x2d, gain, w_t, cos_t, sin_t, qn, kn)


def _diff_attn_kernel(lambda_init, running_max, q_ref, k_ref, v_ref, gate_ref, subln_ref,
                      lq1_ref, lk1_ref, lq2_ref, lk2_ref, y_ref, acc1_ref, acc2_ref):
    tq, tk = DIFF_Q, DIFF_K
    i = pl.program_id(2)
    q = q_ref[0]
    row = lax.broadcasted_iota(jnp.int32, q.shape, 0)
    zero = jnp.zeros_like(q)
    q_pads = (jnp.where(row < HEAD_DIM, q, zero), jnp.where(row >= HEAD_DIM, q, zero))
    acc_refs = (acc1_ref, acc2_ref)
    acc1_ref[...] = jnp.zeros_like(acc1_ref)
    acc2_ref[...] = jnp.zeros_like(acc2_ref)

    def step(j, carry, masked):
        start = pl.multiple_of(j * tk, tk)
        k = k_ref[0, pl.ds(start, tk), :]
        v_t = v_ref[0, :, pl.ds(start, tk)]
        if masked:
            key_chunk = lax.broadcasted_iota(jnp.int32, (tk, tq), 0) // CHUNK
            qry_chunk = lax.broadcasted_iota(jnp.int32, (tk, tq), 1) // CHUNK
            valid = key_chunk <= qry_chunk
        new = []
        for c in range(2):
            s = jnp.dot(k, q_pads[c], preferred_element_type=F32)
            if running_max:
                m, l = carry[2 * c], carry[2 * c + 1]
                if masked:
                    s = jnp.where(valid, s, NEG_INF)
                m_new = jnp.maximum(m, jnp.max(s, axis=0, keepdims=True))
                alpha = jnp.exp(m - m_new)
                p = jnp.exp(s - m_new)
                l_new = alpha * l + jnp.sum(p, axis=0, keepdims=True)
                pv = jnp.dot(v_t, p.astype(BF16), preferred_element_type=F32)
                acc_refs[c][...] = alpha * acc_refs[c][...] + pv
                new += [m_new, l_new]
            else:
                p = jnp.exp(s)
                if masked:
                    p = jnp.where(valid, p, 0.0)
                l_new = carry[c] + jnp.sum(p, axis=0, keepdims=True)
                acc_refs[c][...] += jnp.dot(v_t, p.astype(BF16), preferred_element_type=F32)
                new.append(l_new)
        return tuple(new)

    if running_max:
        init = (jnp.full((1, tq), NEG_INF, F32), jnp.zeros((1, tq), F32)) * 2
    else:
        init = (jnp.zeros((1, tq), F32),) * 2
    carry = lax.fori_loop(0, i, lambda j, c: step(j, c, False), init)
    final = step(i, carry, True)
    l1, l2 = (final[1], final[3]) if running_max else final

    lam = (jnp.exp(jnp.sum(lq1_ref[...] * lk1_ref[...], axis=-1, keepdims=True))
           - jnp.exp(jnp.sum(lq2_ref[...] * lk2_ref[...], axis=-1, keepdims=True))
           + lambda_init)
    o = acc1_ref[...] * (1.0 / l1) - lam * (acc2_ref[...] * (1.0 / l2))
    ms = jnp.mean(o * o, axis=0, keepdims=True)
    o = o * lax.rsqrt(ms + EPS) * subln_ref[...] * (1.0 - lambda_init)
    y_ref[0] = (o * gate_ref[0]).astype(BF16)


def _diff_attn(running_max, q_t, k3, v_t, gate_t, subln_tile, lq1, lk1, lq2, lk2, lambda_init):
    batch, seq = k3.shape[0], k3.shape[1]
    tq = DIFF_Q
    nq = seq // tq
    head_w = 2 * HEAD_DIM

    def blk(b, h, i):
        return (b, h, i)

    def const2(b, h, i):
        return (0, 0)

    return pl.pallas_call(
        functools.partial(_diff_attn_kernel, lambda_init, running_max),
        grid=(batch, C_HEADS, nq),
        in_specs=[
            pl.BlockSpec((1, head_w, tq), blk),
            pl.BlockSpec((1, seq, head_w), lambda b, h, i: (b, 0, h)),
            pl.BlockSpec((1, C_V_DIM, seq), lambda b, h, i: (b, h, 0)),
            pl.BlockSpec((1, C_V_DIM, tq), blk),
            pl.BlockSpec((C_V_DIM, tq), const2),
            pl.BlockSpec((1, HEAD_DIM), const2),
            pl.BlockSpec((1, HEAD_DIM), const2),
            pl.BlockSpec((1, HEAD_DIM), const2),
            pl.BlockSpec((1, HEAD_DIM), const2),
        ],
        out_specs=pl.BlockSpec((1, C_V_DIM, tq), blk),
        out_shape=jax.ShapeDtypeStruct((batch, C_W, seq), BF16),
        scratch_shapes=[pltpu.VMEM((C_V_DIM, tq), F32), pltpu.VMEM((C_V_DIM, tq), F32)],
        compiler_params=_params(("parallel", "parallel", "arbitrary")),
        name="diff_attn_online" if running_max else "diff_attn_direct",
    )(q_t, k3, v_t, gate_t, subln_tile, lq1, lk1, lq2, lk2)


def _qk_score_bound(q_gain, k_gain):
    bound = HEAD_DIM * QK_SCALE * jnp.max(jnp.abs(q_gain)) * jnp.max(jnp.abs(k_gain))
    return bound.astype(F32) * (1.0 + 2.0 ** -6)


def _rope_tables_t(seq):
    inv = 1.0 / (ROPE_THETA ** (jnp.arange(0, HEAD_DIM, 2, dtype=F32) / HEAD_DIM))
    ang = inv[:, None] * jnp.arange(seq, dtype=F32)[None, :]
    return jnp.cos(ang), jnp.sin(ang)


def _gain_tile(g, width):
    return jnp.broadcast_to(g.astype(F32)[:, None], (g.shape[0], width))


def _rel_bias_t(rel_table, n_q):
    n_k = B_PREV_ROWS + n_q
    span = n_k + n_q - 1
    offset = np.arange(span) - (n_k - 1)
    rel = np.clip(B_PREV_ROWS + offset, -B_MAX_REL, B_MAX_REL) + B_MAX_REL
    by_offset = rel_table.astype(F32)[:, rel]
    heads = rel_table.shape[0]
    shifted = jnp.tile(by_offset, (1, n_k + 1))[:, :n_k * (span + 1)].reshape(heads, n_k, span + 1)
    return shifted[:, ::-1, :n_q]


def kernel(x, ev_norm, ev_w_in, ev_w_out, ev_a_q_norm, ev_a_k_norm, ev_a_sinks,
           ev_b_q_norm, ev_b_k_norm, ev_b_rel_bias, od_norm, od_w_in, od_w_out,
           od_q_norm, od_k_norm, od_lambda_q1, od_lambda_k1, od_lambda_q2,
           od_lambda_k2, od_subln):
    batch, seq, d = x.shape
    x2d = x.reshape(batch * seq, d)
    cos_t, sin_t = _rope_tables_t(seq)
    depth = ev_norm.shape[0] + od_norm.shape[0]
    for layer in range(depth):
        i = layer // 2
        if layer % 2 == 0:
            w_in_t = ev_w_in[i].T.astype(BF16)
            w_out_t = ev_w_out[i].T.astype(BF16)
            aq_t, ak, av_t, bq_t, bk, bv_t, gate_t = _even_proj(
                x2d, ev_norm[i][None, :], w_in_t, cos_t, sin_t,
                _gain_tile(ev_a_q_norm[i], PROJ_TOKENS), _gain_tile(ev_a_k_norm[i], PROJ_TOKENS),
                _gain_tile(ev_b_q_norm[i], PROJ_TOKENS), _gain_tile(ev_b_k_norm[i], PROJ_TOKENS),
                batch, seq)
            y_t = _even_mix(ev_a_sinks[i].astype(F32), aq_t, ak, av_t, bq_t, bk, bv_t, gate_t,
                            _rel_bias_t(ev_b_rel_bias[i], MIX_TOKENS), batch, seq)
            x2d = _out_proj(x2d, y_t, w_out_t, batch, seq)
        else:
            lambda_init = 0.8 - 0.6 * math.exp(-0.3 * layer)
            w_in_t = od_w_in[i].T.astype(BF16)
            w_out_t = od_w_out[i].T.astype(BF16)
            q_t, k, v_t, gate_t = _odd_proj(
                x2d, od_norm[i][None, :], w_in_t, cos_t, sin_t,
                _gain_tile(od_q_norm[i], PROJ_TOKENS), _gain_tile(od_k_norm[i], PROJ_TOKENS),
                batch, seq)
            diff_args = (q_t, k.reshape(batch, seq, C_QK), v_t, gate_t,
                         _gain_tile(od_subln[i], DIFF_Q),
                         od_lambda_q1[i][None, :], od_lambda_k1[i][None, :],
                         od_lambda_q2[i][None, :], od_lambda_k2[i][None, :])
            y_t = lax.cond(
                _qk_score_bound(od_q_norm[i], od_k_norm[i]) <= DIFF_SAFE_SCORE,
                lambda *a: _diff_attn(False, *a, lambda_init),
                lambda *a: _diff_attn(True, *a, lambda_init),
                *diff_args)
            x2d = _out_proj(x2d, y_t, w_out_t, batch, seq)
    return x2d.reshape(batch, seq, d)
```

```python
import functools
import math

import jax
import jax.numpy as jnp
import numpy as np
from jax import lax
from jax.experimental import pallas as pl
from jax.experimental.pallas import tpu as pltpu

F32 = jnp.float32
BF16 = jnp.bfloat16

D_MODEL = 1024
CHUNK = 64
HEAD_DIM = 64
HALF_DIM = HEAD_DIM // 2
ROPE_THETA = 10000.0
EPS = 1e-6
NEG_INF = -1e30
QK_SCALE = 1.0 / math.sqrt(HEAD_DIM)

A_Q_HEADS = 8
A_KV_HEADS = 2
A_PREV_CHUNKS = 2
B_HEADS = 8
B_PREV_CHUNKS = 8
B_MAX_REL = 128
C_HEADS = 8
C_V_DIM = 2 * HEAD_DIM

A_Q = A_Q_HEADS * HEAD_DIM
A_KV = A_KV_HEADS * HEAD_DIM
B_W = B_HEADS * HEAD_DIM
EVEN_MIX = A_Q + B_W
C_QK = C_HEADS * 2 * HEAD_DIM
C_W = C_HEADS * C_V_DIM

V7X_VMEM_LIMIT_BYTES = 56 * 1024 * 1024

PROJ_TOKENS = 512
MIX_TOKENS = 256
A_PREV_ROWS = A_PREV_CHUNKS * CHUNK
B_PREV_ROWS = B_PREV_CHUNKS * CHUNK
DIFF_Q = 1024
DIFF_K = 512
DIFF_SUB_K = 128
DIFF_SAFE_SCORE = 40.0


def _params(semantics, flags=None):
    return pltpu.CompilerParams(dimension_semantics=semantics,
                                vmem_limit_bytes=V7X_VMEM_LIMIT_BYTES, flags=flags)


def _rms_rows(x, gain_row):
    ms = jnp.mean(x * x, axis=-1, keepdims=True)
    return x * lax.rsqrt(ms + EPS) * gain_row


def _proj_t(w_t_ref, lo, width, h):
    return lax.dot_general(w_t_ref[lo:lo + width, :], h,
                           (((1,), (1,)), ((), ())),
                           preferred_element_type=F32)


def _head_norm_t(p, gain_tile, cos=None, sin=None, scale=1.0):
    width, tm = p.shape
    y = p.reshape(width // HEAD_DIM, HEAD_DIM, tm)
    ms = jnp.mean(y * y, axis=1, keepdims=True)
    y = y * lax.rsqrt(ms + EPS) * gain_tile[None]
    if cos is not None:
        y1 = y[:, :HALF_DIM]
        y2 = y[:, HALF_DIM:]
        c = cos[None]
        s = sin[None]
        y = jnp.concatenate([y1 * c - y2 * s, y2 * c + y1 * s], axis=1)
    if scale != 1.0:
        y = y * scale
    return y.reshape(width, tm)


def _silu(x):
    return x * (1.0 / (1.0 + jnp.exp(-x)))


def _even_proj_kernel(x_ref, g_ref, w_t_ref, cos_ref, sin_ref,
                      aqn_ref, akn_ref, bqn_ref, bkn_ref,
                      aq_ref, ak_ref, av_ref, bq_ref, bk_ref, bv_ref, gate_ref):
    h = _rms_rows(x_ref[...], g_ref[...]).astype(BF16)
    cos = cos_ref[...]
    sin = sin_ref[...]
    lo = 0
    aq = _proj_t(w_t_ref, lo, A_Q, h); lo += A_Q
    aq_ref[0] = _head_norm_t(aq, aqn_ref[...], cos, sin, QK_SCALE).astype(BF16)
    ak = _proj_t(w_t_ref, lo, A_KV, h); lo += A_KV
    ak = _head_norm_t(ak, akn_ref[...], cos, sin)
    ak_ref[...] = ak.T.astype(BF16)
    av = _proj_t(w_t_ref, lo, A_KV, h); lo += A_KV
    av_ref[0] = av.astype(BF16)
    ag = _proj_t(w_t_ref, lo, A_Q, h); lo += A_Q
    gate_ref[0, :A_Q, :] = _silu(ag)
    bq = _proj_t(w_t_ref, lo, B_W, h); lo += B_W
    bq_ref[0] = _head_norm_t(bq, bqn_ref[...], scale=QK_SCALE).astype(BF16)
    bk = _proj_t(w_t_ref, lo, B_W, h); lo += B_W
    bk = _head_norm_t(bk, bkn_ref[...])
    bk_ref[...] = bk.T.astype(BF16)
    bv = _proj_t(w_t_ref, lo, B_W, h); lo += B_W
    bv_ref[0] = bv.astype(BF16)
    bg = _proj_t(w_t_ref, lo, B_W, h); lo += B_W
    gate_ref[0, A_Q:, :] = _silu(bg)


def _even_proj(x2d, gain, w_t, cos_t, sin_t, aqn, akn, bqn, bkn, batch, seq):
    tm = PROJ_TOKENS
    nt = seq // tm
    tokens = batch * seq
    even_in = w_t.shape[0]

    def tok(b, i):
        return (b * nt + i, 0)

    def feat(b, i):
        return (b, 0, i)

    def const2(b, i):
        return (0, 0)

    def tab(b, i):
        return (0, i)

    in_specs = [
        pl.BlockSpec((tm, D_MODEL), tok),
        pl.BlockSpec((1, D_MODEL), const2),
        pl.BlockSpec((even_in, D_MODEL), const2),
        pl.BlockSpec((HALF_DIM, tm), tab),
        pl.BlockSpec((HALF_DIM, tm), tab),
        pl.BlockSpec((HEAD_DIM, tm), const2),
        pl.BlockSpec((HEAD_DIM, tm), const2),
        pl.BlockSpec((HEAD_DIM, tm), const2),
        pl.BlockSpec((HEAD_DIM, tm), const2),
    ]
    out_shape = [
        jax.ShapeDtypeStruct((batch, A_Q, seq), BF16),
        jax.ShapeDtypeStruct((tokens, A_KV), BF16),
        jax.ShapeDtypeStruct((batch, A_KV, seq), BF16),
        jax.ShapeDtypeStruct((batch, B_W, seq), BF16),
        jax.ShapeDtypeStruct((tokens, B_W), BF16),
        jax.ShapeDtypeStruct((batch, B_W, seq), BF16),
        jax.ShapeDtypeStruct((batch, EVEN_MIX, seq), F32),
    ]
    out_specs = [
        pl.BlockSpec((1, A_Q, tm), feat),
        pl.BlockSpec((tm, A_KV), tok),
        pl.BlockSpec((1, A_KV, tm), feat),
        pl.BlockSpec((1, B_W, tm), feat),
        pl.BlockSpec((tm, B_W), tok),
        pl.BlockSpec((1, B_W, tm), feat),
        pl.BlockSpec((1, EVEN_MIX, tm), feat),
    ]
    return pl.pallas_call(
        _even_proj_kernel,
        grid=(batch, nt),
        in_specs=in_specs,
        out_specs=out_specs,
        out_shape=out_shape,
        compiler_params=_params(("parallel", "parallel")),
        name="even_proj",
    )(x2d, gain, w_t, cos_t, sin_t, aqn, akn, bqn, bkn)


def _band_mask_np(prev_rows, n_q, q_block):
    key = np.arange(prev_rows + n_q)[:, None]
    qry = np.arange(n_q)[None, :] + prev_rows
    return ((key // CHUNK <= qry // CHUNK)
            & (key // CHUNK >= qry // CHUNK - prev_rows // CHUNK)
            & (key >= prev_rows - q_block * n_q))


def _band_mask_variants(prev_rows, n_q):
    n = -(-prev_rows // n_q) + 1
    return np.stack([_band_mask_np(prev_rows, n_q, v) for v in range(n)])


def _band_probs(k_win, q_pad, add_tile, sink):
    s = jnp.dot(k_win, q_pad, preferred_element_type=F32) + add_tile
    m = jnp.max(s, axis=0, keepdims=True)
    if sink is not None:
        m = jnp.maximum(m, sink)
    p = jnp.exp(s - m)
    l = jnp.sum(p, axis=0, keepdims=True)
    if sink is not None:
        l = l + jnp.exp(sink - m)
    return p.astype(BF16), 1.0 / l


def _pad_q(q_h, half):
    z = jnp.zeros_like(q_h)
    return jnp.concatenate([q_h, z] if half == 0 else [z, q_h], axis=0)


def _even_mix_kernel(sinks_ref, aq_ref, akp_ref, akc_ref, avp_ref, avc_ref,
                     bq_ref, bkp2_ref, bkp1_ref, bkc_ref, bvp2_ref, bvp1_ref, bvc_ref,
                     gate_ref, mask_ref, bias_ref, y_ref):
    ak = jnp.concatenate([akp_ref[0], akc_ref[0]], axis=0)
    av = jnp.concatenate([avp_ref[0], avc_ref[0]], axis=1)
    bk = jnp.concatenate([bkp2_ref[0], bkp1_ref[0], bkc_ref[0]], axis=0)
    bv = jnp.concatenate([bvp2_ref[0], bvp1_ref[0], bvc_ref[0]], axis=1)

    group = A_Q_HEADS // A_KV_HEADS
    heads = []
    for h in range(A_Q_HEADS):
        kv = h // group
        heads.append((ak, _pad_q(aq_ref[0, h * HEAD_DIM:(h + 1) * HEAD_DIM, :], kv),
                      av[kv * HEAD_DIM:(kv + 1) * HEAD_DIM, :], mask_ref[0], sinks_ref[h],
                      h * HEAD_DIM))
    for h in range(B_HEADS):
        pair = h // 2
        heads.append((bk[:, pair * 128:(pair + 1) * 128],
                      _pad_q(bq_ref[0, h * HEAD_DIM:(h + 1) * HEAD_DIM, :], h % 2),
                      bv[h * HEAD_DIM:(h + 1) * HEAD_DIM, :], bias_ref[0, h], None,
                      A_Q + h * HEAD_DIM))

    pending = []

    def emit_pv():
        v_t, p, inv_l, row0 = pending.pop(0)
        o = jnp.dot(v_t, p, preferred_element_type=F32) * inv_l
        rows = slice(row0, row0 + HEAD_DIM)
        y_ref[0, rows, :] = (o * gate_ref[0, rows, :]).astype(BF16)

    for k_win, q_pad, v_t, add_tile, sink, row0 in heads:
        p, inv_l = _band_probs(k_win, q_pad, add_tile, sink)
        pending.append((v_t, p, inv_l, row0))
        if len(pending) > 1:
            emit_pv()
    while pending:
        emit_pv()


def _even_mix(sinks, aq_t, ak, av_t, bq_t, bk, bv_t, gate_t, mask_a, bias_b, batch, seq):
    tq = MIX_TOKENS
    nq = seq // tq
    ak3 = ak.reshape(batch, seq, A_KV)
    bk3 = bk.reshape(batch, seq, B_W)
    a_ratio = tq // A_PREV_ROWS

    def cur_t(b, i, *_):
        return (b, 0, i)

    def cur_n(b, i, *_):
        return (b, i, 0)

    def a_prev_n(b, i, *_):
        return (b, jnp.maximum(i * a_ratio - 1, 0), 0)

    def a_prev_t(b, i, *_):
        return (b, 0, jnp.maximum(i * a_ratio - 1, 0))

    def b_prev_n(back):
        return lambda b, i, *_: (b, jnp.maximum(i - back, 0), 0)

    def b_prev_t(back):
        return lambda b, i, *_: (b, 0, jnp.maximum(i - back, 0))

    in_specs = [
        pl.BlockSpec((1, A_Q, tq), cur_t),
        pl.BlockSpec((1, A_PREV_ROWS, A_KV), a_prev_n),
        pl.BlockSpec((1, tq, A_KV), cur_n),
        pl.BlockSpec((1, A_KV, A_PREV_ROWS), a_prev_t),
        pl.BlockSpec((1, A_KV, tq), cur_t),
        pl.BlockSpec((1, B_W, tq), cur_t),
        pl.BlockSpec((1, tq, B_W), b_prev_n(2)),
        pl.BlockSpec((1, tq, B_W), b_prev_n(1)),
        pl.BlockSpec((1, tq, B_W), cur_n),
        pl.BlockSpec((1, B_W, tq), b_prev_t(2)),
        pl.BlockSpec((1, B_W, tq), b_prev_t(1)),
        pl.BlockSpec((1, B_W, tq), cur_t),
        pl.BlockSpec((1, EVEN_MIX, tq), cur_t),
        pl.BlockSpec((1,) + mask_a.shape[1:],
                     lambda b, i, *_: (jnp.minimum(i, mask_a.shape[0] - 1), 0, 0)),
        pl.BlockSpec((1,) + bias_b.shape[1:],
                     lambda b, i, *_: (jnp.minimum(i, bias_b.shape[0] - 1), 0, 0, 0)),
    ]
    grid_spec = pltpu.PrefetchScalarGridSpec(
        num_scalar_prefetch=1,
        grid=(batch, nq),
        in_specs=in_specs,
        out_specs=pl.BlockSpec((1, EVEN_MIX, tq), cur_t),
    )
    return pl.pallas_call(
        _even_mix_kernel,
        grid_spec=grid_spec,
        out_shape=jax.ShapeDtypeStruct((batch, EVEN_MIX, seq), BF16),
        compiler_params=_params(("parallel", "parallel")),
        name="even_mix",
    )(sinks, aq_t, ak3, ak3, av_t, av_t, bq_t, bk3, bk3, bk3, bv_t, bv_t, bv_t,
      gate_t, mask_a, bias_b)


def _out_proj_kernel(x_ref, y_ref, w_t_ref, o_ref):
    out_t = jnp.dot(w_t_ref[...], y_ref[0], preferred_element_type=F32)
    o_ref[...] = x_ref[...] + out_t.T


def _out_proj(x2d, y_t, w_t, batch, seq):
    tm = PROJ_TOKENS
    nt = seq // tm
    width = y_t.shape[1]
    return pl.pallas_call(
        _out_proj_kernel,
        grid=(batch, nt),
        in_specs=[
            pl.BlockSpec((tm, D_MODEL), lambda b, i: (b * nt + i, 0)),
            pl.BlockSpec((1, width, tm), lambda b, i: (b, 0, i)),
            pl.BlockSpec((D_MODEL, width), lambda b, i: (0, 0)),
        ],
        out_specs=pl.BlockSpec((tm, D_MODEL), lambda b, i: (b * nt + i, 0)),
        out_shape=jax.ShapeDtypeStruct(x2d.shape, F32),
        compiler_params=_params(("parallel", "parallel")),
        name="out_proj",
    )(x2d, y_t, w_t)


def _odd_proj_kernel(x_ref, g_ref, w_t_ref, cos_ref, sin_ref, qn_ref, kn_ref,
                     q_ref, k_ref, v_ref, gate_ref):
    h = _rms_rows(x_ref[...], g_ref[...]).astype(BF16)
    cos = cos_ref[...]
    sin = sin_ref[...]
    q = _proj_t(w_t_ref, 0, C_QK, h)
    q_ref[0] = _head_norm_t(q, qn_ref[...], cos, sin, QK_SCALE).astype(BF16)
    k = _proj_t(w_t_ref, C_QK, C_QK, h)
    k = _head_norm_t(k, kn_ref[...], cos, sin)
    k_ref[...] = k.T.astype(BF16)
    v = _proj_t(w_t_ref, 2 * C_QK, C_W, h)
    v_ref[0] = v.astype(BF16)
    g = _proj_t(w_t_ref, 2 * C_QK + C_W, C_W, h)
    gate_ref[0] = _silu(g)


def _odd_proj(x2d, gain, w_t, cos_t, sin_t, qn, kn, batch, seq):
    tm = PROJ_TOKENS
    nt = seq // tm
    tokens = batch * seq
    odd_in = w_t.shape[0]

    def tok(b, i):
        return (b * nt + i, 0)

    def feat(b, i):
        return (b, 0, i)

    def const2(b, i):
        return (0, 0)

    def tab(b, i):
        return (0, i)

    return pl.pallas_call(
        _odd_proj_kernel,
        grid=(batch, nt),
        in_specs=[
            pl.BlockSpec((tm, D_MODEL), tok),
            pl.BlockSpec((1, D_MODEL), const2),
            pl.BlockSpec((odd_in, D_MODEL), const2),
            pl.BlockSpec((HALF_DIM, tm), tab),
            pl.BlockSpec((HALF_DIM, tm), tab),
            pl.BlockSpec((HEAD_DIM, tm), const2),
            pl.BlockSpec((HEAD_DIM, tm), const2),
        ],
        out_specs=[
            pl.BlockSpec((1, C_QK, tm), feat),
            pl.BlockSpec((tm, C_QK), tok),
            pl.BlockSpec((1, C_W, tm), feat),
            pl.BlockSpec((1, C_W, tm), feat),
        ],
        out_shape=[
            jax.ShapeDtypeStruct((batch, C_QK, seq), BF16),
            jax.ShapeDtypeStruct((tokens, C_QK), BF16),
            jax.ShapeDtypeStruct((batch, C_W, seq), BF16),
            jax.ShapeDtypeStruct((batch, C_W, seq), F32),
        ],
        compiler_params=_params(("parallel", "parallel")),
        name="odd_proj",
    )(x2d, gain, w_t, cos_t, sin_t, qn, kn)


def _diff_attn_kernel(lambda_init, running_max, q_ref, k_ref, v_ref, gate_ref, subln_ref,
                      lq1_ref, lk1_ref, lq2_ref, lk2_ref, y_ref, acc1_ref, acc2_ref):
    tq, tk = DIFF_Q, DIFF_K
    i = pl.program_id(2)
    q = q_ref[0]
    row = lax.broadcasted_iota(jnp.int32, q.shape, 0)
    zero = jnp.zeros_like(q)
    q_pads = (jnp.where(row < HEAD_DIM, q, zero), jnp.where(row >= HEAD_DIM, q, zero))
    acc_refs = (acc1_ref, acc2_ref)
    acc1_ref[...] = jnp.zeros_like(acc1_ref)
    acc2_ref[...] = jnp.zeros_like(acc2_ref)

    def load_kv(block):
        start = pl.multiple_of(block * tk, tk)
        return k_ref[0, pl.ds(start, tk), :], v_ref[0, :, pl.ds(start, tk)]

    def diag_valid(d, n_cols):
        key_chunk = (lax.broadcasted_iota(jnp.int32, (tk, n_cols), 0) + d * tk) // CHUNK
        qry_chunk = (lax.broadcasted_iota(jnp.int32, (tk, n_cols), 1) + (tq - n_cols)) // CHUNK
        return key_chunk <= qry_chunk

    def online_step(k, v_t, carry, valid):
        new = []
        for c in range(2):
            m, l = carry[2 * c], carry[2 * c + 1]
            s = jnp.dot(k, q_pads[c], preferred_element_type=F32)
            if valid is not None:
                s = jnp.where(valid, s, NEG_INF)
            m_new = jnp.maximum(m, jnp.max(s, axis=0, keepdims=True))
            alpha = jnp.exp(m - m_new)
            p = jnp.exp(s - m_new)
            l_new = alpha * l + jnp.sum(p, axis=0, keepdims=True)
            pv = jnp.dot(v_t, p.astype(BF16), preferred_element_type=F32)
            acc_refs[c][...] = alpha * acc_refs[c][...] + pv
            new += [m_new, l_new]
        return tuple(new)

    def direct_probs(k, q_c, valid):
        parts, total = [], 0.0
        for r in range(tk // DIFF_SUB_K):
            rows = slice(r * DIFF_SUB_K, (r + 1) * DIFF_SUB_K)
            p = jnp.exp(jnp.dot(k[rows], q_c, preferred_element_type=F32))
            if valid is not None:
                p = jnp.where(valid[rows], p, 0.0)
            total = total + jnp.sum(p, axis=0, keepdims=True)
            parts.append(p.astype(BF16))
        return jnp.concatenate(parts, axis=0), total

    def direct_blocks(blocks, l_sums):
        l_sums = list(l_sums)
        pv_sum = {}
        pending = []

        def emit_pv():
            v_t, c, col0, p = pending.pop(0)
            pv = jnp.dot(v_t, p, preferred_element_type=F32)
            pv_sum[(c, col0)] = pv_sum[(c, col0)] + pv if (c, col0) in pv_sum else pv

        for k, v_t, col0, valid in blocks:
            for c in range(2):
                p, total = direct_probs(k, q_pads[c][:, col0:], valid)
                if col0:
                    total = jnp.concatenate([jnp.zeros((1, col0), F32), total], axis=1)
                l_sums[c] = l_sums[c] + total
                pending.append((v_t, c, col0, p))
                if len(pending) > 1:
                    emit_pv()
        while pending:
            emit_pv()
        for (c, col0), pv in pv_sum.items():
            acc_refs[c][:, col0:] += pv
        return tuple(l_sums)

    diag = range(tq // tk)
    first_diag = i * len(diag)
    if running_max:
        init = (jnp.full((1, tq), NEG_INF, F32), jnp.zeros((1, tq), F32)) * 2
        carry = lax.fori_loop(0, first_diag,
                              lambda j, c: online_step(*load_kv(j), c, None), init)
        for d in diag:
            carry = online_step(*load_kv(first_diag + d), carry, diag_valid(d, tq))
        l1, l2 = carry[1], carry[3]
    else:
        init = (jnp.zeros((1, tq), F32),) * 2

        def pair(t, l_sums):
            return direct_blocks([load_kv(t * len(diag) + u) + (0, None) for u in diag], l_sums)

        l_sums = lax.fori_loop(0, i, pair, init)
        l1, l2 = direct_blocks(
            [load_kv(first_diag + d) + (d * tk, diag_valid(d, tq - d * tk)) for d in diag], l_sums)

    lam = (jnp.exp(jnp.sum(lq1_ref[...] * lk1_ref[...], axis=-1, keepdims=True))
           - jnp.exp(jnp.sum(lq2_ref[...] * lk2_ref[...], axis=-1, keepdims=True))
           + lambda_init)
    o = acc1_ref[...] * (1.0 / l1) - lam * (acc2_ref[...] * (1.0 / l2))
    ms = jnp.mean(o * o, axis=0, keepdims=True)
    o = o * lax.rsqrt(ms + EPS) * subln_ref[...] * (1.0 - lambda_init)
    y_ref[0] = (o * gate_ref[0]).astype(BF16)


def _diff_attn(running_max, q_t, k3, v_t, gate_t, subln_tile, lq1, lk1, lq2, lk2, lambda_init):
    batch, seq = k3.shape[0], k3.shape[1]
    tq = DIFF_Q
    nq = seq // tq
    head_w = 2 * HEAD_DIM

    def blk(b, h, i):
        return (b, h, i)

    def const2(b, h, i):
        return (0, 0)

    return pl.pallas_call(
        functools.partial(_diff_attn_kernel, lambda_init, running_max),
        grid=(batch, C_HEADS, nq),
        in_specs=[
            pl.BlockSpec((1, head_w, tq), blk),
            pl.BlockSpec((1, seq, head_w), lambda b, h, i: (b, 0, h)),
            pl.BlockSpec((1, C_V_DIM, seq), lambda b, h, i: (b, h, 0)),
            pl.BlockSpec((1, C_V_DIM, tq), blk),
            pl.BlockSpec((C_V_DIM, tq), const2),
            pl.BlockSpec((1, HEAD_DIM), const2),
            pl.BlockSpec((1, HEAD_DIM), const2),
            pl.BlockSpec((1, HEAD_DIM), const2),
            pl.BlockSpec((1, HEAD_DIM), const2),
        ],
        out_specs=pl.BlockSpec((1, C_V_DIM, tq), blk),
        out_shape=jax.ShapeDtypeStruct((batch, C_W, seq), BF16),
        scratch_shapes=[pltpu.VMEM((C_V_DIM, tq), F32), pltpu.VMEM((C_V_DIM, tq), F32)],
        compiler_params=_params(("parallel", "parallel", "arbitrary")),
        name="diff_attn_online" if running_max else "diff_attn_direct",
    )(q_t, k3, v_t, gate_t, subln_tile, lq1, lk1, lq2, lk2)


def _qk_score_bound(q_gain, k_gain):
    bound = HEAD_DIM * QK_SCALE * jnp.max(jnp.abs(q_gain)) * jnp.max(jnp.abs(k_gain))
    return bound.astype(F32) * (1.0 + 2.0 ** -6)


def _rope_tables_t(seq):
    inv = 1.0 / (ROPE_THETA ** (jnp.arange(0, HEAD_DIM, 2, dtype=F32) / HEAD_DIM))
    ang = inv[:, None] * jnp.arange(seq, dtype=F32)[None, :]
    return jnp.cos(ang), jnp.sin(ang)


def _gain_tile(g, width):
    return jnp.broadcast_to(g.astype(F32)[:, None], (g.shape[0], width))


def _rel_bias_t(rel_table, n_q):
    n_k = B_PREV_ROWS + n_q
    width = n_k + n_q
    offset = np.arange(width) - (n_k - 1)
    rel = np.clip(B_PREV_ROWS + offset, -B_MAX_REL, B_MAX_REL) + B_MAX_REL
    by_offset = rel_table.astype(F32)[:, rel]
    heads = rel_table.shape[0]
    sub = 8

    def build(f_ref, o_ref):
        for h in range(heads):
            f = f_ref[h:h + 1, :]
            rows = jnp.concatenate(
                [pltpu.roll(f, (width + b - (sub - 1)) % width, axis=1) for b in range(sub)], axis=0)
            for blk in range(n_k // sub):
                lead = n_k - sub - sub * blk
                o_ref[h, sub * blk:sub * (blk + 1), :] = pltpu.roll(
                    rows, (width - lead) % width, axis=1)[:, :n_q]

    return pl.pallas_call(
        build,
        out_shape=jax.ShapeDtypeStruct((heads, n_k, n_q), F32),
        name="rel_bias_tile",
    )(by_offset)


def kernel(x, ev_norm, ev_w_in, ev_w_out, ev_a_q_norm, ev_a_k_norm, ev_a_sinks,
           ev_b_q_norm, ev_b_k_norm, ev_b_rel_bias, od_norm, od_w_in, od_w_out,
           od_q_norm, od_k_norm, od_lambda_q1, od_lambda_k1, od_lambda_q2,
           od_lambda_k2, od_subln):
    batch, seq, d = x.shape
    x2d = x.reshape(batch * seq, d)
    cos_t, sin_t = _rope_tables_t(seq)
    depth = ev_norm.shape[0] + od_norm.shape[0]
    for layer in range(depth):
        i = layer // 2
        if layer % 2 == 0:
            w_in_t = ev_w_in[i].T.astype(BF16)
            w_out_t = ev_w_out[i].T.astype(BF16)
            aq_t, ak, av_t, bq_t, bk, bv_t, gate_t = _even_proj(
                x2d, ev_norm[i][None, :], w_in_t, cos_t, sin_t,
                _gain_tile(ev_a_q_norm[i], PROJ_TOKENS), _gain_tile(ev_a_k_norm[i], PROJ_TOKENS),
                _gain_tile(ev_b_q_norm[i], PROJ_TOKENS), _gain_tile(ev_b_k_norm[i], PROJ_TOKENS),
                batch, seq)
            mask_a = jnp.where(_band_mask_variants(A_PREV_ROWS, MIX_TOKENS), 0.0, NEG_INF).astype(F32)
            bias_b = jnp.where(_band_mask_variants(B_PREV_ROWS, MIX_TOKENS)[:, None],
                               _rel_bias_t(ev_b_rel_bias[i], MIX_TOKENS)[None], NEG_INF)
            y_t = _even_mix(ev_a_sinks[i].astype(F32), aq_t, ak, av_t, bq_t, bk, bv_t, gate_t,
                            mask_a, bias_b, batch, seq)
            x2d = _out_proj(x2d, y_t, w_out_t, batch, seq)
        else:
            lambda_init = 0.8 - 0.6 * math.exp(-0.3 * layer)
            w_in_t = od_w_in[i].T.astype(BF16)
            w_out_t = od_w_out[i].T.astype(BF16)
            q_t, k, v_t, gate_t = _odd_proj(
                x2d, od_norm[i][None, :], w_in_t, cos_t, sin_t,
                _gain_tile(od_q_norm[i], PROJ_TOKENS), _gain_tile(od_k_norm[i], PROJ_TOKENS),
                batch, seq)
            diff_args = (q_t, k.reshape(batch, seq, C_QK), v_t, gate_t,
                         _gain_tile(od_subln[i], DIFF_Q),
                         od_lambda_q1[i][None, :], od_lambda_k1[i][None, :],
                         od_lambda_q2[i][None, :], od_lambda_k2[i][None, :])
            y_t = lax.cond(
                _qk_score_bound(od_q_norm[i], od_k_norm[i]) <= DIFF_SAFE_SCORE,
                lambda *a: _diff_attn(False, *a, lambda_init),
                lambda *a: _diff_attn(True, *a, lambda_init),
                *diff_args)
            x2d = _out_proj(x2d, y_t, w_out_t, batch, seq)
    return x2d.reshape(batch, seq, d)
```

```python
import functools
import math

import jax
import jax.numpy as jnp
import numpy as np
from jax import lax
from jax.experimental import pallas as pl
from jax.experimental.pallas import tpu as pltpu

F32 = jnp.float32
BF16 = jnp.bfloat16

D_MODEL = 1024
CHUNK = 64
HEAD_DIM = 64
HALF_DIM = HEAD_DIM // 2
ROPE_THETA = 10000.0
EPS = 1e-6
NEG_INF = -1e30
QK_SCALE = 1.0 / math.sqrt(HEAD_DIM)
LOG2E = math.log2(math.e)
Q_FOLD = QK_SCALE * LOG2E

A_Q_HEADS = 8
A_KV_HEADS = 2
A_PREV_CHUNKS = 2
B_HEADS = 8
B_PREV_CHUNKS = 8
B_MAX_REL = 128
C_HEADS = 8
C_V_DIM = 2 * HEAD_DIM

A_Q = A_Q_HEADS * HEAD_DIM
A_KV = A_KV_HEADS * HEAD_DIM
B_W = B_HEADS * HEAD_DIM
EVEN_MIX = A_Q + B_W
C_QK = C_HEADS * 2 * HEAD_DIM
C_W = C_HEADS * C_V_DIM
PV_SUM_ROWS = 16
BAND_V_ROWS = HEAD_DIM + PV_SUM_ROWS
A_V_ROWS = A_KV_HEADS * BAND_V_ROWS
B_V_ROWS = B_HEADS * BAND_V_ROWS

V7X_VMEM_LIMIT_BYTES = 56 * 1024 * 1024

PROJ_TOKENS = 512
MIX_TOKENS = 256
A_PREV_ROWS = A_PREV_CHUNKS * CHUNK
B_PREV_ROWS = B_PREV_CHUNKS * CHUNK
DIFF_Q = 1024
DIFF_K = 512
DIFF_SUB_K = 128
SAFE_SCORE = 40.0


def _params(semantics, flags=None):
    return pltpu.CompilerParams(dimension_semantics=semantics,
                                vmem_limit_bytes=V7X_VMEM_LIMIT_BYTES, flags=flags)


def _rms_rows(x, gain_row):
    ms = jnp.mean(x * x, axis=-1, keepdims=True)
    return x * lax.rsqrt(ms + EPS) * gain_row


def _proj_t(w_t_ref, lo, width, h):
    return lax.dot_general(w_t_ref[lo:lo + width, :], h,
                           (((1,), (1,)), ((), ())),
                           preferred_element_type=F32)


def _head_norm_t(p, gain_tile, cos=None, sin=None, scale=1.0):
    width, tm = p.shape
    y = p.reshape(width // HEAD_DIM, HEAD_DIM, tm)
    ms = jnp.mean(y * y, axis=1, keepdims=True)
    y = y * lax.rsqrt(ms + EPS) * gain_tile[None]
    if cos is not None:
        y1 = y[:, :HALF_DIM]
        y2 = y[:, HALF_DIM:]
        c = cos[None]
        s = sin[None]
        y = jnp.concatenate([y1 * c - y2 * s, y2 * c + y1 * s], axis=1)
    if scale != 1.0:
        y = y * scale
    return y.reshape(width, tm)


def _silu(x):
    return x * (1.0 / (1.0 + jnp.exp(-x)))


def _even_proj_kernel(x_ref, g_ref, w_t_ref, cos_ref, sin_ref,
                      aqn_ref, akn_ref, bqn_ref, bkn_ref,
                      aq_ref, ak_ref, av_ref, bq_ref, bk_ref, bv_ref, gate_ref):
    h = _rms_rows(x_ref[...], g_ref[...]).astype(BF16)
    cos = cos_ref[...]
    sin = sin_ref[...]
    lo = 0
    aq = _proj_t(w_t_ref, lo, A_Q, h); lo += A_Q
    aq_ref[0] = _head_norm_t(aq, aqn_ref[...], cos, sin, Q_FOLD).astype(BF16)
    ak = _proj_t(w_t_ref, lo, A_KV, h); lo += A_KV
    ak = _head_norm_t(ak, akn_ref[...], cos, sin)
    ak_ref[...] = ak.T.astype(BF16)
    av = _proj_t(w_t_ref, lo, A_KV, h); lo += A_KV
    _store_v_with_ones(av_ref, av)
    ag = _proj_t(w_t_ref, lo, A_Q, h); lo += A_Q
    gate_ref[0, :A_Q, :] = _silu(ag)
    bq = _proj_t(w_t_ref, lo, B_W, h); lo += B_W
    bq_ref[0] = _head_norm_t(bq, bqn_ref[...], scale=Q_FOLD).astype(BF16)
    bk = _proj_t(w_t_ref, lo, B_W, h); lo += B_W
    bk = _head_norm_t(bk, bkn_ref[...])
    bk_ref[...] = bk.T.astype(BF16)
    bv = _proj_t(w_t_ref, lo, B_W, h); lo += B_W
    _store_v_with_ones(bv_ref, bv)
    bg = _proj_t(w_t_ref, lo, B_W, h); lo += B_W
    gate_ref[0, A_Q:, :] = _silu(bg)


def _store_v_with_ones(v_ref, v):
    tm = v.shape[1]
    ones = jnp.ones((PV_SUM_ROWS, tm), BF16)
    for h in range(v.shape[0] // HEAD_DIM):
        v_ref[0, h * BAND_V_ROWS:h * BAND_V_ROWS + HEAD_DIM, :] = (
            v[h * HEAD_DIM:(h + 1) * HEAD_DIM].astype(BF16))
        v_ref[0, h * BAND_V_ROWS + HEAD_DIM:(h + 1) * BAND_V_ROWS, :] = ones


def _even_proj(x2d, gain, w_t, cos_t, sin_t, aqn, akn, bqn, bkn, batch, seq):
    tm = PROJ_TOKENS
    nt = seq // tm
    tokens = batch * seq
    even_in = w_t.shape[0]

    def tok(b, i):
        return (b * nt + i, 0)

    def feat(b, i):
        return (b, 0, i)

    def const2(b, i):
        return (0, 0)

    def tab(b, i):
        return (0, i)

    in_specs = [
        pl.BlockSpec((tm, D_MODEL), tok),
        pl.BlockSpec((1, D_MODEL), const2),
        pl.BlockSpec((even_in, D_MODEL), const2),
        pl.BlockSpec((HALF_DIM, tm), tab),
        pl.BlockSpec((HALF_DIM, tm), tab),
        pl.BlockSpec((HEAD_DIM, tm), const2),
        pl.BlockSpec((HEAD_DIM, tm), const2),
        pl.BlockSpec((HEAD_DIM, tm), const2),
        pl.BlockSpec((HEAD_DIM, tm), const2),
    ]
    out_shape = [
        jax.ShapeDtypeStruct((batch, A_Q, seq), BF16),
        jax.ShapeDtypeStruct((tokens, A_KV), BF16),
        jax.ShapeDtypeStruct((batch, A_V_ROWS, seq), BF16),
        jax.ShapeDtypeStruct((batch, B_W, seq), BF16),
        jax.ShapeDtypeStruct((tokens, B_W), BF16),
        jax.ShapeDtypeStruct((batch, B_V_ROWS, seq), BF16),
        jax.ShapeDtypeStruct((batch, EVEN_MIX, seq), F32),
    ]
    out_specs = [
        pl.BlockSpec((1, A_Q, tm), feat),
        pl.BlockSpec((tm, A_KV), tok),
        pl.BlockSpec((1, A_V_ROWS, tm), feat),
        pl.BlockSpec((1, B_W, tm), feat),
        pl.BlockSpec((tm, B_W), tok),
        pl.BlockSpec((1, B_V_ROWS, tm), feat),
        pl.BlockSpec((1, EVEN_MIX, tm), feat),
    ]
    return pl.pallas_call(
        _even_proj_kernel,
        grid=(batch, nt),
        in_specs=in_specs,
        out_specs=out_specs,
        out_shape=out_shape,
        compiler_params=_params(("parallel", "parallel")),
        name="even_proj",
    )(x2d, gain, w_t, cos_t, sin_t, aqn, akn, bqn, bkn)


def _band_mask_np(prev_rows, n_q, q_block):
    key = np.arange(prev_rows + n_q)[:, None]
    qry = np.arange(n_q)[None, :] + prev_rows
    return ((key // CHUNK <= qry // CHUNK)
            & (key // CHUNK >= qry // CHUNK - prev_rows // CHUNK)
            & (key >= prev_rows - q_block * n_q))


def _band_mask_variants(prev_rows, n_q):
    n = -(-prev_rows // n_q) + 1
    return np.stack([_band_mask_np(prev_rows, n_q, v) for v in range(n)])


def _band_probs(running_max, k_win, q_pad, add_tile, sink):
    s = jnp.dot(k_win, q_pad, preferred_element_type=F32) + add_tile
    if running_max:
        m = jnp.max(s, axis=0, keepdims=True)
        if sink is not None:
            m = jnp.maximum(m, sink)
        s = s - m
        sink = None if sink is None else sink - m
    sink_term = None if sink is None else jnp.exp2(jnp.broadcast_to(sink, (1, s.shape[1])))
    return jnp.exp2(s).astype(BF16), sink_term


def _pad_q(q_h, half):
    z = jnp.zeros_like(q_h)
    return jnp.concatenate([q_h, z] if half == 0 else [z, q_h], axis=0)


def _even_mix_kernel(running_max, sinks_ref, aq_ref, akp_ref, akc_ref, avp_ref, avc_ref,
                     bq_ref, bkp2_ref, bkp1_ref, bkc_ref, bvp2_ref, bvp1_ref, bvc_ref,
                     gate_ref, mask_ref, bias_ref, y_ref):
    ak = jnp.concatenate([akp_ref[0], akc_ref[0]], axis=0)
    av = jnp.concatenate([avp_ref[0], avc_ref[0]], axis=1)
    bk = jnp.concatenate([bkp2_ref[0], bkp1_ref[0], bkc_ref[0]], axis=0)
    bv = jnp.concatenate([bvp2_ref[0], bvp1_ref[0], bvc_ref[0]], axis=1)

    group = A_Q_HEADS // A_KV_HEADS
    heads = []
    for h in range(A_Q_HEADS):
        kv = h // group
        heads.append((ak, _pad_q(aq_ref[0, h * HEAD_DIM:(h + 1) * HEAD_DIM, :], kv),
                      av[kv * BAND_V_ROWS:(kv + 1) * BAND_V_ROWS, :], mask_ref[0], sinks_ref[h],
                      h * HEAD_DIM))
    for h in range(B_HEADS):
        pair = h // 2
        heads.append((bk[:, pair * 128:(pair + 1) * 128],
                      _pad_q(bq_ref[0, h * HEAD_DIM:(h + 1) * HEAD_DIM, :], h % 2),
                      bv[h * BAND_V_ROWS:(h + 1) * BAND_V_ROWS, :], bias_ref[0, h], None,
                      A_Q + h * HEAD_DIM))

    pending = []

    def emit_pv():
        v_t, p, sink_term, row0 = pending.pop(0)
        o = jnp.dot(v_t, p, preferred_element_type=F32)
        l = o[HEAD_DIM:HEAD_DIM + 1]
        if sink_term is not None:
            l = l + sink_term
        rows = slice(row0, row0 + HEAD_DIM)
        y_ref[0, rows, :] = (o[:HEAD_DIM] * (1.0 / l) * gate_ref[0, rows, :]).astype(BF16)

    for k_win, q_pad, v_t, add_tile, sink, row0 in heads:
        p, sink_term = _band_probs(running_max, k_win, q_pad, add_tile, sink)
        pending.append((v_t, p, sink_term, row0))
        if len(pending) > 1:
            emit_pv()
    while pending:
        emit_pv()


def _even_mix(running_max, sinks, aq_t, ak3, av_t, bq_t, bk3, bv_t, gate_t, mask_a, bias_b):
    batch, seq = ak3.shape[0], ak3.shape[1]
    tq = MIX_TOKENS
    nq = seq // tq
    a_ratio = tq // A_PREV_ROWS

    def cur_t(b, i, *_):
        return (b, 0, i)

    def cur_n(b, i, *_):
        return (b, i, 0)

    def a_prev_n(b, i, *_):
        return (b, jnp.maximum(i * a_ratio - 1, 0), 0)

    def a_prev_t(b, i, *_):
        return (b, 0, jnp.maximum(i * a_ratio - 1, 0))

    def b_prev_n(back):
        return lambda b, i, *_: (b, jnp.maximum(i - back, 0), 0)

    def b_prev_t(back):
        return lambda b, i, *_: (b, 0, jnp.maximum(i - back, 0))

    in_specs = [
        pl.BlockSpec((1, A_Q, tq), cur_t),
        pl.BlockSpec((1, A_PREV_ROWS, A_KV), a_prev_n),
        pl.BlockSpec((1, tq, A_KV), cur_n),
        pl.BlockSpec((1, A_V_ROWS, A_PREV_ROWS), a_prev_t),
        pl.BlockSpec((1, A_V_ROWS, tq), cur_t),
        pl.BlockSpec((1, B_W, tq), cur_t),
        pl.BlockSpec((1, tq, B_W), b_prev_n(2)),
        pl.BlockSpec((1, tq, B_W), b_prev_n(1)),
        pl.BlockSpec((1, tq, B_W), cur_n),
        pl.BlockSpec((1, B_V_ROWS, tq), b_prev_t(2)),
        pl.BlockSpec((1, B_V_ROWS, tq), b_prev_t(1)),
        pl.BlockSpec((1, B_V_ROWS, tq), cur_t),
        pl.BlockSpec((1, EVEN_MIX, tq), cur_t),
        pl.BlockSpec((1,) + mask_a.shape[1:],
                     lambda b, i, *_: (jnp.minimum(i, mask_a.shape[0] - 1), 0, 0)),
        pl.BlockSpec((1,) + bias_b.shape[1:],
                     lambda b, i, *_: (jnp.minimum(i, bias_b.shape[0] - 1), 0, 0, 0)),
    ]
    grid_spec = pltpu.PrefetchScalarGridSpec(
        num_scalar_prefetch=1,
        grid=(batch, nq),
        in_specs=in_specs,
        out_specs=pl.BlockSpec((1, EVEN_MIX, tq), cur_t),
    )
    return pl.pallas_call(
        functools.partial(_even_mix_kernel, running_max),
        grid_spec=grid_spec,
        out_shape=jax.ShapeDtypeStruct((batch, EVEN_MIX, seq), BF16),
        compiler_params=_params(("parallel", "parallel")),
        name="even_mix_online" if running_max else "even_mix_direct",
    )(sinks, aq_t, ak3, ak3, av_t, av_t, bq_t, bk3, bk3, bk3, bv_t, bv_t, bv_t,
      gate_t, mask_a, bias_b)


def _out_proj_kernel(x_ref, y_ref, w_t_ref, o_ref):
    out_t = jnp.dot(w_t_ref[...], y_ref[0], preferred_element_type=F32)
    o_ref[...] = x_ref[...] + out_t.T


def _out_proj(x2d, y_t, w_t, batch, seq):
    tm = PROJ_TOKENS
    nt = seq // tm
    width = y_t.shape[1]
    return pl.pallas_call(
        _out_proj_kernel,
        grid=(batch, nt),
        in_specs=[
            pl.BlockSpec((tm, D_MODEL), lambda b, i: (b * nt + i, 0)),
            pl.BlockSpec((1, width, tm), lambda b, i: (b, 0, i)),
            pl.BlockSpec((D_MODEL, width), lambda b, i: (0, 0)),
        ],
        out_specs=pl.BlockSpec((tm, D_MODEL), lambda b, i: (b * nt + i, 0)),
        out_shape=jax.ShapeDtypeStruct(x2d.shape, F32),
        compiler_params=_params(("parallel", "parallel")),
        name="out_proj",
    )(x2d, y_t, w_t)


def _odd_proj_kernel(x_ref, g_ref, w_t_ref, cos_ref, sin_ref, qn_ref, kn_ref,
                     q_ref, k_ref, v_ref, gate_ref):
    h = _rms_rows(x_ref[...], g_ref[...]).astype(BF16)
    cos = cos_ref[...]
    sin = sin_ref[...]
    q = _proj_t(w_t_ref, 0, C_QK, h)
    q_ref[0] = _head_norm_t(q, qn_ref[...], cos, sin, Q_FOLD).astype(BF16)
    k = _proj_t(w_t_ref, C_QK, C_QK, h)
    k = _head_norm_t(k, kn_ref[...], cos, sin)
    k_ref[...] = k.T.astype(BF16)
    v = _proj_t(w_t_ref, 2 * C_QK, C_W, h)
    v_ref[0] = v.astype(BF16)
    g = _proj_t(w_t_ref, 2 * C_QK + C_W, C_W, h)
    gate_ref[0] = _silu(g)


def _odd_proj(x2d, gain, w_t, cos_t, sin_t, qn, kn, batch, seq):
    tm = PROJ_TOKENS
    nt = seq // tm
    tokens = batch * seq
    odd_in = w_t.shape[0]

    def tok(b, i):
        return (b * nt + i, 0)

    def feat(b, i):
        return (b, 0, i)

    def const2(b, i):
        return (0, 0)

    def tab(b, i):
        return (0, i)

    return pl.pallas_call(
        _odd_proj_kernel,
        grid=(batch, nt),
        in_specs=[
            pl.BlockSpec((tm, D_MODEL), tok),
            pl.BlockSpec((1, D_MODEL), const2),
            pl.BlockSpec((odd_in, D_MODEL), const2),
            pl.BlockSpec((HALF_DIM, tm), tab),
            pl.BlockSpec((HALF_DIM, tm), tab),
            pl.BlockSpec((HEAD_DIM, tm), const2),
            pl.BlockSpec((HEAD_DIM, tm), const2),
        ],
        out_specs=[
            pl.BlockSpec((1, C_QK, tm), feat),
            pl.BlockSpec((tm, C_QK), tok),
            pl.BlockSpec((1, C_W, tm), feat),
            pl.BlockSpec((1, C_W, tm), feat),
        ],
        out_shape=[
            jax.ShapeDtypeStruct((batch, C_QK, seq), BF16),
            jax.ShapeDtypeStruct((tokens, C_QK), BF16),
            jax.ShapeDtypeStruct((batch, C_W, seq), BF16),
            jax.ShapeDtypeStruct((batch, C_W, seq), F32),
        ],
        compiler_params=_params(("parallel", "parallel")),
        name="odd_proj",
    )(x2d, gain, w_t, cos_t, sin_t, qn, kn)


def _diff_attn_kernel(lambda_init, running_max, q_ref, k_ref, v_ref, gate_ref, subln_ref,
                      lq1_ref, lk1_ref, lq2_ref, lk2_ref, y_ref, acc1_ref, acc2_ref):
    tq, tk = DIFF_Q, DIFF_K
    i = pl.program_id(2)
    q = q_ref[0]
    row = lax.broadcasted_iota(jnp.int32, q.shape, 0)
    zero = jnp.zeros_like(q)
    q_pads = (jnp.where(row < HEAD_DIM, q, zero), jnp.where(row >= HEAD_DIM, q, zero))
    acc_refs = (acc1_ref, acc2_ref)
    acc1_ref[...] = jnp.zeros_like(acc1_ref)
    acc2_ref[...] = jnp.zeros_like(acc2_ref)

    def load_kv(block):
        start = pl.multiple_of(block * tk, tk)
        return k_ref[0, pl.ds(start, tk), :], v_ref[0, :, pl.ds(start, tk)]

    def diag_valid(d, n_cols):
        key_chunk = (lax.broadcasted_iota(jnp.int32, (tk, n_cols), 0) + d * tk) // CHUNK
        qry_chunk = (lax.broadcasted_iota(jnp.int32, (tk, n_cols), 1) + (tq - n_cols)) // CHUNK
        return key_chunk <= qry_chunk

    def online_step(k, v_t, carry, valid):
        new = []
        for c in range(2):
            m, l = carry[2 * c], carry[2 * c + 1]
            s = jnp.dot(k, q_pads[c], preferred_element_type=F32)
            if valid is not None:
                s = jnp.where(valid, s, NEG_INF)
            m_new = jnp.maximum(m, jnp.max(s, axis=0, keepdims=True))
            alpha = jnp.exp2(m - m_new)
            p = jnp.exp2(s - m_new)
            l_new = alpha * l + jnp.sum(p, axis=0, keepdims=True)
            pv = jnp.dot(v_t, p.astype(BF16), preferred_element_type=F32)
            acc_refs[c][...] = alpha * acc_refs[c][...] + pv
            new += [m_new, l_new]
        return tuple(new)

    def direct_probs(k, q_c, valid):
        parts, total = [], 0.0
        for r in range(tk // DIFF_SUB_K):
            rows = slice(r * DIFF_SUB_K, (r + 1) * DIFF_SUB_K)
            p = jnp.exp2(jnp.dot(k[rows], q_c, preferred_element_type=F32))
            if valid is not None:
                p = jnp.where(valid[rows], p, 0.0)
            total = total + jnp.sum(p, axis=0, keepdims=True)
            parts.append(p.astype(BF16))
        return jnp.concatenate(parts, axis=0), total

    def direct_blocks(blocks, l_sums):
        l_sums = list(l_sums)
        pv_sum = {}
        pending = []

        def emit_pv():
            v_t, c, col0, p = pending.pop(0)
            pv = jnp.dot(v_t, p, preferred_element_type=F32)
            pv_sum[(c, col0)] = pv_sum[(c, col0)] + pv if (c, col0) in pv_sum else pv

        for k, v_t, col0, valid in blocks:
            for c in range(2):
                p, total = direct_probs(k, q_pads[c][:, col0:], valid)
                if col0:
                    total = jnp.concatenate([jnp.zeros((1, col0), F32), total], axis=1)
                l_sums[c] = l_sums[c] + total
                pending.append((v_t, c, col0, p))
                if len(pending) > 1:
                    emit_pv()
        while pending:
            emit_pv()
        for (c, col0), pv in pv_sum.items():
            acc_refs[c][:, col0:] += pv
        return tuple(l_sums)

    diag = range(tq // tk)
    first_diag = i * len(diag)
    if running_max:
        init = (jnp.full((1, tq), NEG_INF, F32), jnp.zeros((1, tq), F32)) * 2
        carry = lax.fori_loop(0, first_diag,
                              lambda j, c: online_step(*load_kv(j), c, None), init)
        for d in diag:
            carry = online_step(*load_kv(first_diag + d), carry, diag_valid(d, tq))
        l1, l2 = carry[1], carry[3]
    else:
        init = (jnp.zeros((1, tq), F32),) * 2

        def pair(t, l_sums):
            return direct_blocks([load_kv(t * len(diag) + u) + (0, None) for u in diag], l_sums)

        l_sums = lax.fori_loop(0, i, pair, init)
        l1, l2 = direct_blocks(
            [load_kv(first_diag + d) + (d * tk, diag_valid(d, tq - d * tk)) for d in diag], l_sums)

    lam = (jnp.exp(jnp.sum(lq1_ref[...] * lk1_ref[...], axis=-1, keepdims=True))
           - jnp.exp(jnp.sum(lq2_ref[...] * lk2_ref[...], axis=-1, keepdims=True))
           + lambda_init)
    o = acc1_ref[...] * (1.0 / l1) - lam * (acc2_ref[...] * (1.0 / l2))
    ms = jnp.mean(o * o, axis=0, keepdims=True)
    o = o * lax.rsqrt(ms + EPS) * subln_ref[...] * (1.0 - lambda_init)
    y_ref[0] = (o * gate_ref[0]).astype(BF16)


def _diff_attn(running_max, q_t, k3, v_t, gate_t, subln_tile, lq1, lk1, lq2, lk2, lambda_init):
    batch, seq = k3.shape[0], k3.shape[1]
    tq = DIFF_Q
    nq = seq // tq
    head_w = 2 * HEAD_DIM

    def blk(b, h, i):
        return (b, h, i)

    def const2(b, h, i):
        return (0, 0)

    return pl.pallas_call(
        functools.partial(_diff_attn_kernel, lambda_init, running_max),
        grid=(batch, C_HEADS, nq),
        in_specs=[
            pl.BlockSpec((1, head_w, tq), blk),
            pl.BlockSpec((1, seq, head_w), lambda b, h, i: (b, 0, h)),
            pl.BlockSpec((1, C_V_DIM, seq), lambda b, h, i: (b, h, 0)),
            pl.BlockSpec((1, C_V_DIM, tq), blk),
            pl.BlockSpec((C_V_DIM, tq), const2),
            pl.BlockSpec((1, HEAD_DIM), const2),
            pl.BlockSpec((1, HEAD_DIM), const2),
            pl.BlockSpec((1, HEAD_DIM), const2),
            pl.BlockSpec((1, HEAD_DIM), const2),
        ],
        out_specs=pl.BlockSpec((1, C_V_DIM, tq), blk),
        out_shape=jax.ShapeDtypeStruct((batch, C_W, seq), BF16),
        scratch_shapes=[pltpu.VMEM((C_V_DIM, tq), F32), pltpu.VMEM((C_V_DIM, tq), F32)],
        compiler_params=_params(("parallel", "parallel", "arbitrary")),
        name="diff_attn_online" if running_max else "diff_attn_direct",
    )(q_t, k3, v_t, gate_t, subln_tile, lq1, lk1, lq2, lk2)


def _qk_score_bound(q_gain, k_gain):
    bound = HEAD_DIM * QK_SCALE * jnp.max(jnp.abs(q_gain)) * jnp.max(jnp.abs(k_gain))
    return bound.astype(F32) * (1.0 + 2.0 ** -6)


def _rope_tables_t(seq):
    inv = 1.0 / (ROPE_THETA ** (jnp.arange(0, HEAD_DIM, 2, dtype=F32) / HEAD_DIM))
    ang = inv[:, None] * jnp.arange(seq, dtype=F32)[None, :]
    return jnp.cos(ang), jnp.sin(ang)


def _gain_tile(g, width):
    return jnp.broadcast_to(g.astype(F32)[:, None], (g.shape[0], width))


def _rel_bias_t(rel_table, n_q):
    n_k = B_PREV_ROWS + n_q
    width = n_k + n_q
    offset = np.arange(width) - (n_k - 1)
    rel = np.clip(B_PREV_ROWS + offset, -B_MAX_REL, B_MAX_REL) + B_MAX_REL
    by_offset = rel_table.astype(F32)[:, rel]
    heads = rel_table.shape[0]
    sub = 8

    def build(f_ref, o_ref):
        for h in range(heads):
            f = f_ref[h:h + 1, :]
            rows = jnp.concatenate(
                [pltpu.roll(f, (width + b - (sub - 1)) % width, axis=1) for b in range(sub)], axis=0)
            for blk in range(n_k // sub):
                lead = n_k - sub - sub * blk
                o_ref[h, sub * blk:sub * (blk + 1), :] = pltpu.roll(
                    rows, (width - lead) % width, axis=1)[:, :n_q]

    return pl.pallas_call(
        build,
        out_shape=jax.ShapeDtypeStruct((heads, n_k, n_q), F32),
        name="rel_bias_tile",
    )(by_offset)


def kernel(x, ev_norm, ev_w_in, ev_w_out, ev_a_q_norm, ev_a_k_norm, ev_a_sinks,
           ev_b_q_norm, ev_b_k_norm, ev_b_rel_bias, od_norm, od_w_in, od_w_out,
           od_q_norm, od_k_norm, od_lambda_q1, od_lambda_k1, od_lambda_q2,
           od_lambda_k2, od_subln):
    batch, seq, d = x.shape
    x2d = x.reshape(batch * seq, d)
    cos_t, sin_t = _rope_tables_t(seq)
    depth = ev_norm.shape[0] + od_norm.shape[0]
    for layer in range(depth):
        i = layer // 2
        if layer % 2 == 0:
            w_in_t = ev_w_in[i].T.astype(BF16)
            w_out_t = ev_w_out[i].T.astype(BF16)
            aq_t, ak, av_t, bq_t, bk, bv_t, gate_t = _even_proj(
                x2d, ev_norm[i][None, :], w_in_t, cos_t, sin_t,
                _gain_tile(ev_a_q_norm[i], PROJ_TOKENS), _gain_tile(ev_a_k_norm[i], PROJ_TOKENS),
                _gain_tile(ev_b_q_norm[i], PROJ_TOKENS), _gain_tile(ev_b_k_norm[i], PROJ_TOKENS),
                batch, seq)
            mask_a = jnp.where(_band_mask_variants(A_PREV_ROWS, MIX_TOKENS), 0.0, NEG_INF).astype(F32)
            bias_b = jnp.where(_band_mask_variants(B_PREV_ROWS, MIX_TOKENS)[:, None],
                               _rel_bias_t(ev_b_rel_bias[i] * LOG2E, MIX_TOKENS)[None], NEG_INF)
            sinks = ev_a_sinks[i].astype(F32)
            mix_args = (sinks * LOG2E, aq_t, ak.reshape(batch, seq, A_KV), av_t,
                        bq_t, bk.reshape(batch, seq, B_W), bv_t, gate_t, mask_a, bias_b)
            bound_a = jnp.maximum(_qk_score_bound(ev_a_q_norm[i], ev_a_k_norm[i]),
                                  jnp.max(jnp.abs(sinks)))
            bound_b = (_qk_score_bound(ev_b_q_norm[i], ev_b_k_norm[i])
                       + jnp.max(jnp.abs(ev_b_rel_bias[i])))
            y_t = lax.cond(jnp.maximum(bound_a, bound_b) <= SAFE_SCORE,
                           functools.partial(_even_mix, False),
                           functools.partial(_even_mix, True), *mix_args)
            x2d = _out_proj(x2d, y_t, w_out_t, batch, seq)
        else:
            lambda_init = 0.8 - 0.6 * math.exp(-0.3 * layer)
            w_in_t = od_w_in[i].T.astype(BF16)
            w_out_t = od_w_out[i].T.astype(BF16)
            q_t, k, v_t, gate_t = _odd_proj(
                x2d, od_norm[i][None, :], w_in_t, cos_t, sin_t,
                _gain_tile(od_q_norm[i], PROJ_TOKENS), _gain_tile(od_k_norm[i], PROJ_TOKENS),
                batch, seq)
            diff_args = (q_t, k.reshape(batch, seq, C_QK), v_t, gate_t,
                         _gain_tile(od_subln[i], DIFF_Q),
                         od_lambda_q1[i][None, :], od_lambda_k1[i][None, :],
                         od_lambda_q2[i][None, :], od_lambda_k2[i][None, :])
            y_t = lax.cond(
                _qk_score_bound(od_q_norm[i], od_k_norm[i]) <= SAFE_SCORE,
                lambda *a: _diff_attn(False, *a, lambda_init),
                lambda *a: _diff_attn(True, *a, lambda_init),
                *diff_args)
            x2d = _out_proj(x2d, y_t, w_out_t, batch, seq)
    return x2d.reshape(batch, seq, d)
```

```python
import functools
import math

import jax
import jax.numpy as jnp
import numpy as np
from jax import lax
from jax.experimental import pallas as pl
from jax.experimental.pallas import tpu as pltpu

F32 = jnp.float32
BF16 = jnp.bfloat16

D_MODEL = 1024
CHUNK = 64
HEAD_DIM = 64
HALF_DIM = HEAD_DIM // 2
ROPE_THETA = 10000.0
EPS = 1e-6
NEG_INF = -1e30
QK_SCALE = 1.0 / math.sqrt(HEAD_DIM)
LOG2E = math.log2(math.e)
Q_FOLD = QK_SCALE * LOG2E

A_Q_HEADS = 8
A_KV_HEADS = 2
A_PREV_CHUNKS = 2
B_HEADS = 8
B_PREV_CHUNKS = 8
B_MAX_REL = 128
C_HEADS = 8
C_V_DIM = 2 * HEAD_DIM

A_Q = A_Q_HEADS * HEAD_DIM
A_KV = A_KV_HEADS * HEAD_DIM
B_W = B_HEADS * HEAD_DIM
EVEN_MIX = A_Q + B_W
C_QK = C_HEADS * 2 * HEAD_DIM
C_W = C_HEADS * C_V_DIM
PV_SUM_ROWS = 16
BAND_V_ROWS = HEAD_DIM + PV_SUM_ROWS
A_V_ROWS = A_KV_HEADS * BAND_V_ROWS
B_V_ROWS = B_HEADS * BAND_V_ROWS

V7X_VMEM_LIMIT_BYTES = 56 * 1024 * 1024

PROJ_TOKENS = 512
MIX_TOKENS = 256
A_PREV_ROWS = A_PREV_CHUNKS * CHUNK
B_PREV_ROWS = B_PREV_CHUNKS * CHUNK
DIFF_Q = 1024
DIFF_K = 512
SAFE_SCORE = 40.0


def _params(semantics):
    return pltpu.CompilerParams(dimension_semantics=semantics,
                                vmem_limit_bytes=V7X_VMEM_LIMIT_BYTES)


def _rms_rows(x, gain_row):
    ms = jnp.mean(x * x, axis=-1, keepdims=True)
    return x * lax.rsqrt(ms + EPS) * gain_row


def _proj_t(w_t_ref, lo, width, h):
    return lax.dot_general(w_t_ref[lo:lo + width, :], h,
                           (((1,), (1,)), ((), ())),
                           preferred_element_type=F32)


def _head_norm_t(p, gain_tile, cos=None, sin=None, scale=1.0):
    width, tm = p.shape
    y = p.reshape(width // HEAD_DIM, HEAD_DIM, tm)
    ms = jnp.mean(y * y, axis=1, keepdims=True)
    y = y * lax.rsqrt(ms + EPS) * gain_tile[None]
    if cos is not None:
        y1 = y[:, :HALF_DIM]
        y2 = y[:, HALF_DIM:]
        c = cos[None]
        s = sin[None]
        y = jnp.concatenate([y1 * c - y2 * s, y2 * c + y1 * s], axis=1)
    if scale != 1.0:
        y = y * scale
    return y.reshape(width, tm)


def _silu(x):
    return x * (1.0 / (1.0 + jnp.exp(-x)))


def _even_proj_kernel(x_ref, g_ref, w_t_ref, cos_ref, sin_ref,
                      aqn_ref, akn_ref, bqn_ref, bkn_ref,
                      aq_ref, ak_ref, av_ref, bq_ref, bk_ref, bv_ref, gate_ref):
    h = _rms_rows(x_ref[...], g_ref[...]).astype(BF16)
    cos = cos_ref[...]
    sin = sin_ref[...]
    lo = 0
    aq = _proj_t(w_t_ref, lo, A_Q, h); lo += A_Q
    aq_ref[0] = _head_norm_t(aq, aqn_ref[...], cos, sin, Q_FOLD).astype(BF16)
    ak = _proj_t(w_t_ref, lo, A_KV, h); lo += A_KV
    ak = _head_norm_t(ak, akn_ref[...], cos, sin)
    ak_ref[...] = ak.T.astype(BF16)
    av = _proj_t(w_t_ref, lo, A_KV, h); lo += A_KV
    _store_v_with_ones(av_ref, av)
    ag = _proj_t(w_t_ref, lo, A_Q, h); lo += A_Q
    gate_ref[0, :A_Q, :] = _silu(ag)
    bq = _proj_t(w_t_ref, lo, B_W, h); lo += B_W
    bq_ref[0] = _head_norm_t(bq, bqn_ref[...], scale=Q_FOLD).astype(BF16)
    bk = _proj_t(w_t_ref, lo, B_W, h); lo += B_W
    bk = _head_norm_t(bk, bkn_ref[...])
    bk_ref[...] = bk.T.astype(BF16)
    bv = _proj_t(w_t_ref, lo, B_W, h); lo += B_W
    _store_v_with_ones(bv_ref, bv)
    bg = _proj_t(w_t_ref, lo, B_W, h); lo += B_W
    gate_ref[0, A_Q:, :] = _silu(bg)


def _store_v_with_ones(v_ref, v):
    tm = v.shape[1]
    ones = jnp.ones((PV_SUM_ROWS, tm), BF16)
    for h in range(v.shape[0] // HEAD_DIM):
        v_ref[0, h * BAND_V_ROWS:h * BAND_V_ROWS + HEAD_DIM, :] = (
            v[h * HEAD_DIM:(h + 1) * HEAD_DIM].astype(BF16))
        v_ref[0, h * BAND_V_ROWS + HEAD_DIM:(h + 1) * BAND_V_ROWS, :] = ones


def _even_proj(x2d, gain, w_t, cos_t, sin_t, aqn, akn, bqn, bkn, batch, seq):
    tm = PROJ_TOKENS
    nt = seq // tm
    tokens = batch * seq
    even_in = w_t.shape[0]

    def tok(b, i):
        return (b * nt + i, 0)

    def feat(b, i):
        return (b, 0, i)

    def const2(b, i):
        return (0, 0)

    def tab(b, i):
        return (0, i)

    in_specs = [
        pl.BlockSpec((tm, D_MODEL), tok),
        pl.BlockSpec((1, D_MODEL), const2),
        pl.BlockSpec((even_in, D_MODEL), const2),
        pl.BlockSpec((HALF_DIM, tm), tab),
        pl.BlockSpec((HALF_DIM, tm), tab),
        pl.BlockSpec((HEAD_DIM, tm), const2),
        pl.BlockSpec((HEAD_DIM, tm), const2),
        pl.BlockSpec((HEAD_DIM, tm), const2),
        pl.BlockSpec((HEAD_DIM, tm), const2),
    ]
    out_shape = [
        jax.ShapeDtypeStruct((batch, A_Q, seq), BF16),
        jax.ShapeDtypeStruct((tokens, A_KV), BF16),
        jax.ShapeDtypeStruct((batch, A_V_ROWS, seq), BF16),
        jax.ShapeDtypeStruct((batch, B_W, seq), BF16),
        jax.ShapeDtypeStruct((tokens, B_W), BF16),
        jax.ShapeDtypeStruct((batch, B_V_ROWS, seq), BF16),
        jax.ShapeDtypeStruct((batch, EVEN_MIX, seq), F32),
    ]
    out_specs = [
        pl.BlockSpec((1, A_Q, tm), feat),
        pl.BlockSpec((tm, A_KV), tok),
        pl.BlockSpec((1, A_V_ROWS, tm), feat),
        pl.BlockSpec((1, B_W, tm), feat),
        pl.BlockSpec((tm, B_W), tok),
        pl.BlockSpec((1, B_V_ROWS, tm), feat),
        pl.BlockSpec((1, EVEN_MIX, tm), feat),
    ]
    return pl.pallas_call(
        _even_proj_kernel,
        grid=(batch, nt),
        in_specs=in_specs,
        out_specs=out_specs,
        out_shape=out_shape,
        compiler_params=_params(("parallel", "parallel")),
        name="even_proj",
    )(x2d, gain, w_t, cos_t, sin_t, aqn, akn, bqn, bkn)


def _band_mask_np(prev_rows, n_q, q_block):
    key = np.arange(prev_rows + n_q)[:, None]
    qry = np.arange(n_q)[None, :] + prev_rows
    return ((key // CHUNK <= qry // CHUNK)
            & (key // CHUNK >= qry // CHUNK - prev_rows // CHUNK)
            & (key >= prev_rows - q_block * n_q))


def _band_mask_variants(prev_rows, n_q):
    n = -(-prev_rows // n_q) + 1
    return np.stack([_band_mask_np(prev_rows, n_q, v) for v in range(n)])


def _band_probs(running_max, k_win, q_pad, add_tile, sink):
    s = jnp.dot(k_win, q_pad, preferred_element_type=F32) + add_tile
    if running_max:
        m = jnp.max(s, axis=0, keepdims=True)
        if sink is not None:
            m = jnp.maximum(m, sink)
        s = s - m
        sink = None if sink is None else sink - m
    sink_term = None if sink is None else jnp.exp2(jnp.broadcast_to(sink, (1, s.shape[1])))
    return jnp.exp2(s).astype(BF16), sink_term


def _pad_q(q_h, half):
    z = jnp.zeros_like(q_h)
    return jnp.concatenate([q_h, z] if half == 0 else [z, q_h], axis=0)


def _even_mix_kernel(running_max, sinks_ref, aq_ref, akp_ref, akc_ref, avp_ref, avc_ref,
                     bq_ref, bkp2_ref, bkp1_ref, bkc_ref, bvp2_ref, bvp1_ref, bvc_ref,
                     gate_ref, mask_ref, bias_ref, y_ref):
    ak = jnp.concatenate([akp_ref[0], akc_ref[0]], axis=0)
    av = jnp.concatenate([avp_ref[0], avc_ref[0]], axis=1)
    bk = jnp.concatenate([bkp2_ref[0], bkp1_ref[0], bkc_ref[0]], axis=0)
    bv = jnp.concatenate([bvp2_ref[0], bvp1_ref[0], bvc_ref[0]], axis=1)

    group = A_Q_HEADS // A_KV_HEADS
    heads = []
    for h in range(A_Q_HEADS):
        kv = h // group
        heads.append((ak, _pad_q(aq_ref[0, h * HEAD_DIM:(h + 1) * HEAD_DIM, :], kv),
                      av[kv * BAND_V_ROWS:(kv + 1) * BAND_V_ROWS, :], mask_ref[0], sinks_ref[h],
                      h * HEAD_DIM))
    for h in range(B_HEADS):
        pair = h // 2
        heads.append((bk[:, pair * 128:(pair + 1) * 128],
                      _pad_q(bq_ref[0, h * HEAD_DIM:(h + 1) * HEAD_DIM, :], h % 2),
                      bv[h * BAND_V_ROWS:(h + 1) * BAND_V_ROWS, :], bias_ref[0, h], None,
                      A_Q + h * HEAD_DIM))

    pending = []

    def emit_pv():
        v_t, p, sink_term, row0 = pending.pop(0)
        o = jnp.dot(v_t, p, preferred_element_type=F32)
        l = o[HEAD_DIM:HEAD_DIM + 1]
        if sink_term is not None:
            l = l + sink_term
        rows = slice(row0, row0 + HEAD_DIM)
        y_ref[0, rows, :] = (o[:HEAD_DIM] * (1.0 / l) * gate_ref[0, rows, :]).astype(BF16)

    for k_win, q_pad, v_t, add_tile, sink, row0 in heads:
        p, sink_term = _band_probs(running_max, k_win, q_pad, add_tile, sink)
        pending.append((v_t, p, sink_term, row0))
        if len(pending) > 1:
            emit_pv()
    while pending:
        emit_pv()


def _even_mix(running_max, sinks, aq_t, ak3, av_t, bq_t, bk3, bv_t, gate_t, mask_a, bias_b):
    batch, seq = ak3.shape[0], ak3.shape[1]
    tq = MIX_TOKENS
    nq = seq // tq
    a_ratio = tq // A_PREV_ROWS

    def cur_t(b, i, *_):
        return (b, 0, i)

    def cur_n(b, i, *_):
        return (b, i, 0)

    def a_prev_n(b, i, *_):
        return (b, jnp.maximum(i * a_ratio - 1, 0), 0)

    def a_prev_t(b, i, *_):
        return (b, 0, jnp.maximum(i * a_ratio - 1, 0))

    def b_prev_n(back):
        return lambda b, i, *_: (b, jnp.maximum(i - back, 0), 0)

    def b_prev_t(back):
        return lambda b, i, *_: (b, 0, jnp.maximum(i - back, 0))

    in_specs = [
        pl.BlockSpec((1, A_Q, tq), cur_t),
        pl.BlockSpec((1, A_PREV_ROWS, A_KV), a_prev_n),
        pl.BlockSpec((1, tq, A_KV), cur_n),
        pl.BlockSpec((1, A_V_ROWS, A_PREV_ROWS), a_prev_t),
        pl.BlockSpec((1, A_V_ROWS, tq), cur_t),
        pl.BlockSpec((1, B_W, tq), cur_t),
        pl.BlockSpec((1, tq, B_W), b_prev_n(2)),
        pl.BlockSpec((1, tq, B_W), b_prev_n(1)),
        pl.BlockSpec((1, tq, B_W), cur_n),
        pl.BlockSpec((1, B_V_ROWS, tq), b_prev_t(2)),
        pl.BlockSpec((1, B_V_ROWS, tq), b_prev_t(1)),
        pl.BlockSpec((1, B_V_ROWS, tq), cur_t),
        pl.BlockSpec((1, EVEN_MIX, tq), cur_t),
        pl.BlockSpec((1,) + mask_a.shape[1:],
                     lambda b, i, *_: (jnp.minimum(i, mask_a.shape[0] - 1), 0, 0)),
        pl.BlockSpec((1,) + bias_b.shape[1:],
                     lambda b, i, *_: (jnp.minimum(i, bias_b.shape[0] - 1), 0, 0, 0)),
    ]
    grid_spec = pltpu.PrefetchScalarGridSpec(
        num_scalar_prefetch=1,
        grid=(batch, nq),
        in_specs=in_specs,
        out_specs=pl.BlockSpec((1, EVEN_MIX, tq), cur_t),
    )
    return pl.pallas_call(
        functools.partial(_even_mix_kernel, running_max),
        grid_spec=grid_spec,
        out_shape=jax.ShapeDtypeStruct((batch, EVEN_MIX, seq), BF16),
        compiler_params=_params(("parallel", "parallel")),
        name="even_mix_online" if running_max else "even_mix_direct",
    )(sinks, aq_t, ak3, ak3, av_t, av_t, bq_t, bk3, bk3, bk3, bv_t, bv_t, bv_t,
      gate_t, mask_a, bias_b)


def _out_proj_kernel(x_feature_major, x_ref, y_ref, w_t_ref, o_ref):
    out_t = jnp.dot(w_t_ref[...], y_ref[0], preferred_element_type=F32)
    if x_feature_major:
        o_ref[...] = (x_ref[0] + out_t).T
    else:
        o_ref[0] = x_ref[...].T + out_t


def _out_proj(x_feature_major, x, y_t, w_t):
    batch, width, seq = y_t.shape
    tm = PROJ_TOKENS
    nt = seq // tm
    tok_spec = pl.BlockSpec((tm, D_MODEL), lambda b, i: (b * nt + i, 0))
    feat_spec = pl.BlockSpec((1, D_MODEL, tm), lambda b, i: (b, 0, i))
    tok_shape = jax.ShapeDtypeStruct((batch * seq, D_MODEL), F32)
    feat_shape = jax.ShapeDtypeStruct((batch, D_MODEL, seq), F32)
    return pl.pallas_call(
        functools.partial(_out_proj_kernel, x_feature_major),
        grid=(batch, nt),
        in_specs=[
            feat_spec if x_feature_major else tok_spec,
            pl.BlockSpec((1, width, tm), lambda b, i: (b, 0, i)),
            pl.BlockSpec((D_MODEL, width), lambda b, i: (0, 0)),
        ],
        out_specs=tok_spec if x_feature_major else feat_spec,
        out_shape=tok_shape if x_feature_major else feat_shape,
        compiler_params=_params(("parallel", "parallel")),
        name="out_proj_to_tokens" if x_feature_major else "out_proj_to_features",
    )(x, y_t, w_t)


def _odd_proj_kernel(x_ref, g_ref, w_t_ref, cos_ref, sin_ref, qn_ref, kn_ref,
                     q_ref, k_ref, v_ref, gate_ref):
    x = x_ref[0]
    ms = jnp.mean(x * x, axis=0, keepdims=True)
    h_t = (x * lax.rsqrt(ms + EPS) * g_ref[...]).astype(BF16)
    cos = cos_ref[...]
    sin = sin_ref[...]

    def proj(lo, width):
        return jnp.dot(w_t_ref[lo:lo + width, :], h_t, preferred_element_type=F32)

    k = proj(C_QK, C_QK)
    q = proj(0, C_QK)
    g = proj(2 * C_QK + C_W, C_W)
    v = proj(2 * C_QK, C_W)
    k = _head_norm_t(k, kn_ref[...], cos, sin)
    k_ref[...] = k.T.astype(BF16)
    q_ref[0] = _head_norm_t(q, qn_ref[...], cos, sin, Q_FOLD).astype(BF16)
    gate_ref[0] = _silu(g)
    v_ref[0] = v.astype(BF16)


def _odd_proj(x_t, gain_col, w_t, cos_t, sin_t, qn, kn):
    batch, _, seq = x_t.shape
    tm = PROJ_TOKENS
    nt = seq // tm
    tokens = batch * seq
    odd_in = w_t.shape[0]

    def tok(b, i):
        return (b * nt + i, 0)

    def feat(b, i):
        return (b, 0, i)

    def const2(b, i):
        return (0, 0)

    def tab(b, i):
        return (0, i)

    return pl.pallas_call(
        _odd_proj_kernel,
        grid=(batch, nt),
        in_specs=[
            pl.BlockSpec((1, D_MODEL, tm), feat),
            pl.BlockSpec((D_MODEL, 1), const2),
            pl.BlockSpec((odd_in, D_MODEL), const2),
            pl.BlockSpec((HALF_DIM, tm), tab),
            pl.BlockSpec((HALF_DIM, tm), tab),
            pl.BlockSpec((HEAD_DIM, tm), const2),
            pl.BlockSpec((HEAD_DIM, tm), const2),
        ],
        out_specs=[
            pl.BlockSpec((1, C_QK, tm), feat),
            pl.BlockSpec((tm, C_QK), tok),
            pl.BlockSpec((1, C_W, tm), feat),
            pl.BlockSpec((1, C_W, tm), feat),
        ],
        out_shape=[
            jax.ShapeDtypeStruct((batch, C_QK, seq), BF16),
            jax.ShapeDtypeStruct((tokens, C_QK), BF16),
            jax.ShapeDtypeStruct((batch, C_W, seq), BF16),
            jax.ShapeDtypeStruct((batch, C_W, seq), F32),
        ],
        compiler_params=_params(("parallel", "parallel")),
        name="odd_proj",
    )(x_t, gain_col, w_t, cos_t, sin_t, qn, kn)


def _diff_attn_kernel(lambda_init, running_max, q_ref, k_ref, v_ref, gate_ref, subln_ref,
                      lq1_ref, lk1_ref, lq2_ref, lk2_ref, y_ref, acc1_ref, acc2_ref):
    tq, tk = DIFF_Q, DIFF_K
    i = pl.program_id(2)
    q = q_ref[0]
    row = lax.broadcasted_iota(jnp.int32, q.shape, 0)
    zero = jnp.zeros_like(q)
    q_pads = (jnp.where(row < HEAD_DIM, q, zero), jnp.where(row >= HEAD_DIM, q, zero))
    acc_refs = (acc1_ref, acc2_ref)

    def load_kv(block):
        start = pl.multiple_of(block * tk, tk)
        return k_ref[0, pl.ds(start, tk), :], v_ref[0, :, pl.ds(start, tk)]

    def diag_valid(d, n_cols):
        key_chunk = (lax.broadcasted_iota(jnp.int32, (tk, n_cols), 0) + d * tk) // CHUNK
        qry_chunk = (lax.broadcasted_iota(jnp.int32, (tk, n_cols), 1) + (tq - n_cols)) // CHUNK
        return key_chunk <= qry_chunk

    def online_step(k, v_t, carry, valid):
        new = []
        for c in range(2):
            m, l = carry[2 * c], carry[2 * c + 1]
            s = jnp.dot(k, q_pads[c], preferred_element_type=F32)
            if valid is not None:
                s = jnp.where(valid, s, NEG_INF)
            m_new = jnp.maximum(m, jnp.max(s, axis=0, keepdims=True))
            alpha = jnp.exp2(m - m_new)
            p = jnp.exp2(s - m_new)
            l_new = alpha * l + jnp.sum(p, axis=0, keepdims=True)
            pv = jnp.dot(v_t, p.astype(BF16), preferred_element_type=F32)
            acc_refs[c][...] = alpha * acc_refs[c][...] + pv
            new += [m_new, l_new]
        return tuple(new)

    def direct_probs(k, q_c, valid):
        p = jnp.exp2(jnp.dot(k, q_c, preferred_element_type=F32))
        if valid is not None:
            p = jnp.where(valid, p, 0.0)
        return p.astype(BF16), jnp.sum(p, axis=0, keepdims=True)

    def direct_blocks(blocks, l_sums):
        l_sums = list(l_sums)
        pv_sum = {}
        pending = []

        def emit_pv():
            v_t, c, col0, p = pending.pop(0)
            pv = jnp.dot(v_t, p, preferred_element_type=F32)
            pv_sum[(c, col0)] = pv_sum[(c, col0)] + pv if (c, col0) in pv_sum else pv

        for k, v_t, col0, valid in blocks:
            for c in range(2):
                p, total = direct_probs(k, q_pads[c][:, col0:], valid)
                if col0:
                    total = jnp.concatenate([jnp.zeros((1, col0), F32), total], axis=1)
                l_sums[c] = l_sums[c] + total
                pending.append((v_t, c, col0, p))
                if len(pending) > 1:
                    emit_pv()
        while pending:
            emit_pv()
        for (c, col0), pv in pv_sum.items():
            acc_refs[c][:, col0:] += pv
        return tuple(l_sums)

    diag = range(tq // tk)
    first_diag = i * len(diag)
    if running_max:
        acc1_ref[...] = jnp.zeros_like(acc1_ref)
        acc2_ref[...] = jnp.zeros_like(acc2_ref)
        init = (jnp.full((1, tq), NEG_INF, F32), jnp.zeros((1, tq), F32)) * 2
        carry = lax.fori_loop(0, first_diag,
                              lambda j, c: online_step(*load_kv(j), c, None), init)
        for d in diag:
            carry = online_step(*load_kv(first_diag + d), carry, diag_valid(d, tq))
        l1, l2 = carry[1], carry[3]
    else:
        acc1_ref[...] = jnp.zeros_like(acc1_ref)
        acc2_ref[...] = jnp.zeros_like(acc2_ref)

        def pair(t, l_sums):
            return direct_blocks([load_kv(t * len(diag) + u) + (0, None) for u in diag], l_sums)

        l_sums = lax.fori_loop(0, i, pair, (jnp.zeros((1, tq), F32),) * 2)
        l1, l2 = direct_blocks(
            [load_kv(first_diag + d) + (d * tk, diag_valid(d, tq - d * tk)) for d in diag], l_sums)

    lam = (jnp.exp(jnp.sum(lq1_ref[...] * lk1_ref[...], axis=-1, keepdims=True))
           - jnp.exp(jnp.sum(lq2_ref[...] * lk2_ref[...], axis=-1, keepdims=True))
           + lambda_init)
    o = acc1_ref[...] * (1.0 / l1) - lam * (acc2_ref[...] * (1.0 / l2))
    ms = jnp.mean(o * o, axis=0, keepdims=True)
    o = o * lax.rsqrt(ms + EPS) * subln_ref[...] * (1.0 - lambda_init)
    y_ref[0] = (o * gate_ref[0]).astype(BF16)


def _diff_attn(running_max, q_t, k3, v_t, gate_t, subln_tile, lq1, lk1, lq2, lk2, lambda_init):
    batch, seq = k3.shape[0], k3.shape[1]
    tq = DIFF_Q
    nq = seq // tq
    head_w = 2 * HEAD_DIM

    def blk(b, h, i):
        return (b, h, i)

    def const2(b, h, i):
        return (0, 0)

    return pl.pallas_call(
        functools.partial(_diff_attn_kernel, lambda_init, running_max),
        grid=(batch, C_HEADS, nq),
        in_specs=[
            pl.BlockSpec((1, head_w, tq), blk),
            pl.BlockSpec((1, seq, head_w), lambda b, h, i: (b, 0, h)),
            pl.BlockSpec((1, C_V_DIM, seq), lambda b, h, i: (b, h, 0)),
            pl.BlockSpec((1, C_V_DIM, tq), blk),
            pl.BlockSpec((C_V_DIM, tq), const2),
            pl.BlockSpec((1, HEAD_DIM), const2),
            pl.BlockSpec((1, HEAD_DIM), const2),
            pl.BlockSpec((1, HEAD_DIM), const2),
            pl.BlockSpec((1, HEAD_DIM), const2),
        ],
        out_specs=pl.BlockSpec((1, C_V_DIM, tq), blk),
        out_shape=jax.ShapeDtypeStruct((batch, C_W, seq), BF16),
        scratch_shapes=[pltpu.VMEM((C_V_DIM, tq), F32), pltpu.VMEM((C_V_DIM, tq), F32)],
        compiler_params=_params(("parallel", "parallel", "arbitrary")),
        name="diff_attn_online" if running_max else "diff_attn_direct",
    )(q_t, k3, v_t, gate_t, subln_tile, lq1, lk1, lq2, lk2)


def _qk_score_bound(q_gain, k_gain):
    bound = HEAD_DIM * QK_SCALE * jnp.max(jnp.abs(q_gain)) * jnp.max(jnp.abs(k_gain))
    return bound.astype(F32) * (1.0 + 2.0 ** -6)


def _rope_tables_t(seq):
    inv = 1.0 / (ROPE_THETA ** (jnp.arange(0, HEAD_DIM, 2, dtype=F32) / HEAD_DIM))
    ang = inv[:, None] * jnp.arange(seq, dtype=F32)[None, :]
    return jnp.cos(ang), jnp.sin(ang)


def _gain_tile(g, width):
    return jnp.broadcast_to(g.astype(F32)[:, None], (g.shape[0], width))


def _rel_bias_t(rel_table, n_q):
    n_k = B_PREV_ROWS + n_q
    width = n_k + n_q
    offset = np.arange(width) - (n_k - 1)
    rel = np.clip(B_PREV_ROWS + offset, -B_MAX_REL, B_MAX_REL) + B_MAX_REL
    by_offset = rel_table.astype(F32)[:, rel]
    heads = rel_table.shape[0]
    sub = 8

    def build(f_ref, o_ref):
        for h in range(heads):
            f = f_ref[h:h + 1, :]
            rows = jnp.concatenate(
                [pltpu.roll(f, (width + b - (sub - 1)) % width, axis=1) for b in range(sub)], axis=0)
            for blk in range(n_k // sub):
                lead = n_k - sub - sub * blk
                o_ref[h, sub * blk:sub * (blk + 1), :] = pltpu.roll(
                    rows, (width - lead) % width, axis=1)[:, :n_q]

    return pl.pallas_call(
        build,
        out_shape=jax.ShapeDtypeStruct((heads, n_k, n_q), F32),
        name="rel_bias_tile",
    )(by_offset)


def kernel(x, ev_norm, ev_w_in, ev_w_out, ev_a_q_norm, ev_a_k_norm, ev_a_sinks,
           ev_b_q_norm, ev_b_k_norm, ev_b_rel_bias, od_norm, od_w_in, od_w_out,
           od_q_norm, od_k_norm, od_lambda_q1, od_lambda_k1, od_lambda_q2,
           od_lambda_k2, od_subln):
    batch, seq, d = x.shape
    cos_t, sin_t = _rope_tables_t(seq)
    depth = ev_norm.shape[0] + od_norm.shape[0]
    assert ev_norm.shape[0] == od_norm.shape[0]
    x2d = x.reshape(batch * seq, d)
    for layer in range(depth):
        i = layer // 2
        if layer % 2 == 0:
            w_in_t = ev_w_in[i].T.astype(BF16)
            w_out_t = ev_w_out[i].T.astype(BF16)
            aq_t, ak, av_t, bq_t, bk, bv_t, gate_t = _even_proj(
                x2d, ev_norm[i][None, :], w_in_t, cos_t, sin_t,
                _gain_tile(ev_a_q_norm[i], PROJ_TOKENS), _gain_tile(ev_a_k_norm[i], PROJ_TOKENS),
                _gain_tile(ev_b_q_norm[i], PROJ_TOKENS), _gain_tile(ev_b_k_norm[i], PROJ_TOKENS),
                batch, seq)
            mask_a = jnp.where(_band_mask_variants(A_PREV_ROWS, MIX_TOKENS), 0.0, NEG_INF).astype(F32)
            bias_b = jnp.where(_band_mask_variants(B_PREV_ROWS, MIX_TOKENS)[:, None],
                               _rel_bias_t(ev_b_rel_bias[i] * LOG2E, MIX_TOKENS)[None], NEG_INF)
            sinks = ev_a_sinks[i].astype(F32)
            mix_args = (sinks * LOG2E, aq_t, ak.reshape(batch, seq, A_KV), av_t,
                        bq_t, bk.reshape(batch, seq, B_W), bv_t, gate_t, mask_a, bias_b)
            bound_a = jnp.maximum(_qk_score_bound(ev_a_q_norm[i], ev_a_k_norm[i]),
                                  jnp.max(jnp.abs(sinks)))
            bound_b = (_qk_score_bound(ev_b_q_norm[i], ev_b_k_norm[i])
                       + jnp.max(jnp.abs(ev_b_rel_bias[i])))
            y_t = lax.cond(jnp.maximum(bound_a, bound_b) <= SAFE_SCORE,
                           functools.partial(_even_mix, False),
                           functools.partial(_even_mix, True), *mix_args)
            x_t = _out_proj(False, x2d, y_t, w_out_t)
        else:
            lambda_init = 0.8 - 0.6 * math.exp(-0.3 * layer)
            w_in_t = od_w_in[i].T.astype(BF16)
            w_out_t = od_w_out[i].T.astype(BF16)
            q_t, k, v_t, gate_t = _odd_proj(
                x_t, od_norm[i][:, None], w_in_t, cos_t, sin_t,
                _gain_tile(od_q_norm[i], PROJ_TOKENS), _gain_tile(od_k_norm[i], PROJ_TOKENS))
            diff_args = (q_t, k.reshape(batch, seq, C_QK), v_t, gate_t,
                         _gain_tile(od_subln[i], DIFF_Q),
                         od_lambda_q1[i][None, :], od_lambda_k1[i][None, :],
                         od_lambda_q2[i][None, :], od_lambda_k2[i][None, :])
            y_t = lax.cond(
                _qk_score_bound(od_q_norm[i], od_k_norm[i]) <= SAFE_SCORE,
                lambda *a: _diff_attn(False, *a, lambda_init),
                lambda *a: _diff_attn(True, *a, lambda_init),
                *diff_args)
            x2d = _out_proj(True, x_t, y_t, w_out_t)
    return x2d.reshape(batch, seq, d)
```

```python
import functools
import math

import jax
import jax.numpy as jnp
import numpy as np
from jax import lax
from jax.experimental import pallas as pl
from jax.experimental.pallas import tpu as pltpu

F32 = jnp.float32
BF16 = jnp.bfloat16

D_MODEL = 1024
CHUNK = 64
HEAD_DIM = 64
HALF_DIM = HEAD_DIM // 2
ROPE_THETA = 10000.0
EPS = 1e-6
NEG_INF = -1e30
QK_SCALE = 1.0 / math.sqrt(HEAD_DIM)
LOG2E = math.log2(math.e)
Q_FOLD = QK_SCALE * LOG2E

A_Q_HEADS = 8
A_KV_HEADS = 2
A_PREV_CHUNKS = 2
B_HEADS = 8
B_PREV_CHUNKS = 8
B_MAX_REL = 128
C_HEADS = 8
C_V_DIM = 2 * HEAD_DIM

A_Q = A_Q_HEADS * HEAD_DIM
A_KV = A_KV_HEADS * HEAD_DIM
B_W = B_HEADS * HEAD_DIM
EVEN_MIX = A_Q + B_W
C_QK = C_HEADS * 2 * HEAD_DIM
C_W = C_HEADS * C_V_DIM
PV_SUM_ROWS = 16
BAND_V_ROWS = HEAD_DIM + PV_SUM_ROWS
A_V_ROWS = A_KV_HEADS * BAND_V_ROWS
B_V_ROWS = B_HEADS * BAND_V_ROWS

V7X_VMEM_LIMIT_BYTES = 56 * 1024 * 1024

PROJ_TOKENS = 512
MIX_TOKENS = 256
MIX_PV_LAG = 3
A_PREV_ROWS = A_PREV_CHUNKS * CHUNK
B_PREV_ROWS = B_PREV_CHUNKS * CHUNK
DIFF_Q = 1024
DIFF_K = 512
SAFE_SCORE = 40.0


def _params(semantics):
    return pltpu.CompilerParams(dimension_semantics=semantics,
                                vmem_limit_bytes=V7X_VMEM_LIMIT_BYTES)


def _rms_rows(x, gain_row):
    ms = jnp.mean(x * x, axis=-1, keepdims=True)
    return x * lax.rsqrt(ms + EPS) * gain_row


def _proj_t(w_t_ref, lo, width, h):
    return lax.dot_general(w_t_ref[lo:lo + width, :], h,
                           (((1,), (1,)), ((), ())),
                           preferred_element_type=F32)


def _head_norm_t(p, gain_tile, cos=None, sin=None, scale=1.0):
    width, tm = p.shape
    y = p.reshape(width // HEAD_DIM, HEAD_DIM, tm)
    ms = jnp.mean(y * y, axis=1, keepdims=True)
    y = y * lax.rsqrt(ms + EPS) * gain_tile[None]
    if cos is not None:
        y1 = y[:, :HALF_DIM]
        y2 = y[:, HALF_DIM:]
        c = cos[None]
        s = sin[None]
        y = jnp.concatenate([y1 * c - y2 * s, y2 * c + y1 * s], axis=1)
    if scale != 1.0:
        y = y * scale
    return y.reshape(width, tm)


def _silu(x):
    return x * (1.0 / (1.0 + jnp.exp(-x)))


def _even_proj_kernel(x_ref, g_ref, w_t_ref, cos_ref, sin_ref,
                      aqn_ref, akn_ref, bqn_ref, bkn_ref,
                      aq_ref, ak_ref, av_ref, bq_ref, bk_ref, bv_ref, gate_ref):
    h = _rms_rows(x_ref[...], g_ref[...]).astype(BF16)
    cos = cos_ref[...]
    sin = sin_ref[...]
    a_w = 2 * A_Q + 2 * A_KV
    a_all = _proj_t(w_t_ref, 0, a_w, h)
    b_qk = _proj_t(w_t_ref, a_w, 2 * B_W, h)
    b_vg = _proj_t(w_t_ref, a_w + 2 * B_W, 2 * B_W, h)
    ak = _head_norm_t(a_all[A_Q:A_Q + A_KV], akn_ref[...], cos, sin)
    ak_ref[...] = ak.T.astype(BF16)
    aq_ref[0] = _head_norm_t(a_all[:A_Q], aqn_ref[...], cos, sin, Q_FOLD).astype(BF16)
    _store_v_with_ones(av_ref, a_all[A_Q + A_KV:A_Q + 2 * A_KV])
    gate_ref[0, :A_Q, :] = _silu(a_all[A_Q + 2 * A_KV:])
    bk = _head_norm_t(b_qk[B_W:], bkn_ref[...])
    bk_ref[...] = bk.T.astype(BF16)
    bq_ref[0] = _head_norm_t(b_qk[:B_W], bqn_ref[...], scale=Q_FOLD).astype(BF16)
    gate_ref[0, A_Q:, :] = _silu(b_vg[B_W:])
    _store_v_with_ones(bv_ref, b_vg[:B_W])


def _store_v_with_ones(v_ref, v):
    tm = v.shape[1]
    ones = jnp.ones((PV_SUM_ROWS, tm), BF16)
    for h in range(v.shape[0] // HEAD_DIM):
        v_ref[0, h * BAND_V_ROWS:h * BAND_V_ROWS + HEAD_DIM, :] = (
            v[h * HEAD_DIM:(h + 1) * HEAD_DIM].astype(BF16))
        v_ref[0, h * BAND_V_ROWS + HEAD_DIM:(h + 1) * BAND_V_ROWS, :] = ones


def _even_proj(x2d, gain, w_t, cos_t, sin_t, aqn, akn, bqn, bkn, batch, seq):
    tm = PROJ_TOKENS
    nt = seq // tm
    tokens = batch * seq
    even_in = w_t.shape[0]

    def tok(b, i):
        return (b * nt + i, 0)

    def feat(b, i):
        return (b, 0, i)

    def const2(b, i):
        return (0, 0)

    def tab(b, i):
        return (0, i)

    in_specs = [
        pl.BlockSpec((tm, D_MODEL), tok),
        pl.BlockSpec((1, D_MODEL), const2),
        pl.BlockSpec((even_in, D_MODEL), const2),
        pl.BlockSpec((HALF_DIM, tm), tab),
        pl.BlockSpec((HALF_DIM, tm), tab),
        pl.BlockSpec((HEAD_DIM, tm), const2),
        pl.BlockSpec((HEAD_DIM, tm), const2),
        pl.BlockSpec((HEAD_DIM, tm), const2),
        pl.BlockSpec((HEAD_DIM, tm), const2),
    ]
    out_shape = [
        jax.ShapeDtypeStruct((batch, A_Q, seq), BF16),
        jax.ShapeDtypeStruct((tokens, A_KV), BF16),
        jax.ShapeDtypeStruct((batch, A_V_ROWS, seq), BF16),
        jax.ShapeDtypeStruct((batch, B_W, seq), BF16),
        jax.ShapeDtypeStruct((tokens, B_W), BF16),
        jax.ShapeDtypeStruct((batch, B_V_ROWS, seq), BF16),
        jax.ShapeDtypeStruct((batch, EVEN_MIX, seq), F32),
    ]
    out_specs = [
        pl.BlockSpec((1, A_Q, tm), feat),
        pl.BlockSpec((tm, A_KV), tok),
        pl.BlockSpec((1, A_V_ROWS, tm), feat),
        pl.BlockSpec((1, B_W, tm), feat),
        pl.BlockSpec((tm, B_W), tok),
        pl.BlockSpec((1, B_V_ROWS, tm), feat),
        pl.BlockSpec((1, EVEN_MIX, tm), feat),
    ]
    return pl.pallas_call(
        _even_proj_kernel,
        grid=(batch, nt),
        in_specs=in_specs,
        out_specs=out_specs,
        out_shape=out_shape,
        compiler_params=_params(("parallel", "parallel")),
        name="even_proj",
    )(x2d, gain, w_t, cos_t, sin_t, aqn, akn, bqn, bkn)


def _band_mask_np(prev_rows, n_q, q_block):
    key = np.arange(prev_rows + n_q)[:, None]
    qry = np.arange(n_q)[None, :] + prev_rows
    return ((key // CHUNK <= qry // CHUNK)
            & (key // CHUNK >= qry // CHUNK - prev_rows // CHUNK)
            & (key >= prev_rows - q_block * n_q))


def _band_mask_variants(prev_rows, n_q):
    n = -(-prev_rows // n_q) + 1
    return np.stack([_band_mask_np(prev_rows, n_q, v) for v in range(n)])


def _band_probs(running_max, k_win, q_pad, add_tile, sink):
    s = jnp.dot(k_win, q_pad, preferred_element_type=F32) + add_tile
    if running_max:
        m = jnp.max(s, axis=0, keepdims=True)
        if sink is not None:
            m = jnp.maximum(m, sink)
        s = s - m
        sink = None if sink is None else sink - m
    sink_term = None if sink is None else jnp.exp2(jnp.broadcast_to(sink, (1, s.shape[1])))
    return jnp.exp2(s).astype(BF16), sink_term


def _pad_q(q_h, half):
    z = jnp.zeros_like(q_h)
    return jnp.concatenate([q_h, z] if half == 0 else [z, q_h], axis=0)


def _even_mix_kernel(running_max, sinks_ref, aq_ref, akp_ref, akc_ref, avp_ref, avc_ref,
                     bq_ref, bkp2_ref, bkp1_ref, bkc_ref, bvp2_ref, bvp1_ref, bvc_ref,
                     gate_ref, mask_ref, bias_ref, y_ref):
    ak = jnp.concatenate([akp_ref[0], akc_ref[0]], axis=0)
    av = jnp.concatenate([avp_ref[0], avc_ref[0]], axis=1)
    bk = jnp.concatenate([bkp2_ref[0], bkp1_ref[0], bkc_ref[0]], axis=0)
    bv = jnp.concatenate([bvp2_ref[0], bvp1_ref[0], bvc_ref[0]], axis=1)

    group = A_Q_HEADS // A_KV_HEADS
    heads = []
    for h in range(A_Q_HEADS):
        kv = h // group
        heads.append((ak, _pad_q(aq_ref[0, h * HEAD_DIM:(h + 1) * HEAD_DIM, :], kv),
                      av[kv * BAND_V_ROWS:(kv + 1) * BAND_V_ROWS, :], mask_ref[0], sinks_ref[h],
                      h * HEAD_DIM))
    for h in range(B_HEADS):
        pair = h // 2
        heads.append((bk[:, pair * 128:(pair + 1) * 128],
                      _pad_q(bq_ref[0, h * HEAD_DIM:(h + 1) * HEAD_DIM, :], h % 2),
                      bv[h * BAND_V_ROWS:(h + 1) * BAND_V_ROWS, :], bias_ref[0, h], None,
                      A_Q + h * HEAD_DIM))

    pending = []

    def emit_pv():
        v_t, p, sink_term, row0 = pending.pop(0)
        o = jnp.dot(v_t, p, preferred_element_type=F32)
        l = o[HEAD_DIM:HEAD_DIM + 1]
        if sink_term is not None:
            l = l + sink_term
        rows = slice(row0, row0 + HEAD_DIM)
        y_ref[0, rows, :] = (o[:HEAD_DIM] * (1.0 / l) * gate_ref[0, rows, :]).astype(BF16)

    for k_win, q_pad, v_t, add_tile, sink, row0 in heads:
        p, sink_term = _band_probs(running_max, k_win, q_pad, add_tile, sink)
        pending.append((v_t, p, sink_term, row0))
        if len(pending) > MIX_PV_LAG:
            emit_pv()
    while pending:
        emit_pv()


def _even_mix(running_max, sinks, aq_t, ak3, av_t, bq_t, bk3, bv_t, gate_t, mask_a, bias_b):
    batch, seq = ak3.shape[0], ak3.shape[1]
    tq = MIX_TOKENS
    nq = seq // tq
    a_ratio = tq // A_PREV_ROWS

    def cur_t(b, i, *_):
        return (b, 0, i)

    def cur_n(b, i, *_):
        return (b, i, 0)

    def a_prev_n(b, i, *_):
        return (b, jnp.maximum(i * a_ratio - 1, 0), 0)

    def a_prev_t(b, i, *_):
        return (b, 0, jnp.maximum(i * a_ratio - 1, 0))

    def b_prev_n(back):
        return lambda b, i, *_: (b, jnp.maximum(i - back, 0), 0)

    def b_prev_t(back):
        return lambda b, i, *_: (b, 0, jnp.maximum(i - back, 0))

    in_specs = [
        pl.BlockSpec((1, A_Q, tq), cur_t),
        pl.BlockSpec((1, A_PREV_ROWS, A_KV), a_prev_n),
        pl.BlockSpec((1, tq, A_KV), cur_n),
        pl.BlockSpec((1, A_V_ROWS, A_PREV_ROWS), a_prev_t),
        pl.BlockSpec((1, A_V_ROWS, tq), cur_t),
        pl.BlockSpec((1, B_W, tq), cur_t),
        pl.BlockSpec((1, tq, B_W), b_prev_n(2)),
        pl.BlockSpec((1, tq, B_W), b_prev_n(1)),
        pl.BlockSpec((1, tq, B_W), cur_n),
        pl.BlockSpec((1, B_V_ROWS, tq), b_prev_t(2)),
        pl.BlockSpec((1, B_V_ROWS, tq), b_prev_t(1)),
        pl.BlockSpec((1, B_V_ROWS, tq), cur_t),
        pl.BlockSpec((1, EVEN_MIX, tq), cur_t),
        pl.BlockSpec((1,) + mask_a.shape[1:],
                     lambda b, i, *_: (jnp.minimum(i, mask_a.shape[0] - 1), 0, 0)),
        pl.BlockSpec((1,) + bias_b.shape[1:],
                     lambda b, i, *_: (jnp.minimum(i, bias_b.shape[0] - 1), 0, 0, 0)),
    ]
    grid_spec = pltpu.PrefetchScalarGridSpec(
        num_scalar_prefetch=1,
        grid=(batch, nq),
        in_specs=in_specs,
        out_specs=pl.BlockSpec((1, EVEN_MIX, tq), cur_t),
    )
    return pl.pallas_call(
        functools.partial(_even_mix_kernel, running_max),
        grid_spec=grid_spec,
        out_shape=jax.ShapeDtypeStruct((batch, EVEN_MIX, seq), BF16),
        compiler_params=_params(("parallel", "parallel")),
        name="even_mix_online" if running_max else "even_mix_direct",
    )(sinks, aq_t, ak3, ak3, av_t, av_t, bq_t, bk3, bk3, bk3, bv_t, bv_t, bv_t,
      gate_t, mask_a, bias_b)


def _out_proj_kernel(x_feature_major, x_ref, y_ref, w_t_ref, o_ref):
    out_t = jnp.dot(w_t_ref[...], y_ref[0], preferred_element_type=F32)
    if x_feature_major:
        o_ref[...] = (x_ref[0] + out_t).T
    else:
        o_ref[0] = x_ref[...].T + out_t


def _out_proj(x_feature_major, x, y_t, w_t):
    batch, width, seq = y_t.shape
    tm = PROJ_TOKENS
    nt = seq // tm
    tok_spec = pl.BlockSpec((tm, D_MODEL), lambda b, i: (b * nt + i, 0))
    feat_spec = pl.BlockSpec((1, D_MODEL, tm), lambda b, i: (b, 0, i))
    tok_shape = jax.ShapeDtypeStruct((batch * seq, D_MODEL), F32)
    feat_shape = jax.ShapeDtypeStruct((batch, D_MODEL, seq), F32)
    return pl.pallas_call(
        functools.partial(_out_proj_kernel, x_feature_major),
        grid=(batch, nt),
        in_specs=[
            feat_spec if x_feature_major else tok_spec,
            pl.BlockSpec((1, width, tm), lambda b, i: (b, 0, i)),
            pl.BlockSpec((D_MODEL, width), lambda b, i: (0, 0)),
        ],
        out_specs=tok_spec if x_feature_major else feat_spec,
        out_shape=tok_shape if x_feature_major else feat_shape,
        compiler_params=_params(("parallel", "parallel")),
        name="out_proj_to_tokens" if x_feature_major else "out_proj_to_features",
    )(x, y_t, w_t)


def _odd_proj_kernel(x_ref, g_ref, w_t_ref, cos_ref, sin_ref, qn_ref, kn_ref,
                     q_ref, k_ref, v_ref, gate_ref):
    x = x_ref[0]
    ms = jnp.mean(x * x, axis=0, keepdims=True)
    h_t = (x * lax.rsqrt(ms + EPS) * g_ref[...]).astype(BF16)
    cos = cos_ref[...]
    sin = sin_ref[...]

    def proj(lo, width):
        return jnp.dot(w_t_ref[lo:lo + width, :], h_t, preferred_element_type=F32)

    k = proj(C_QK, C_QK)
    q = proj(0, C_QK)
    g = proj(2 * C_QK + C_W, C_W)
    v = proj(2 * C_QK, C_W)
    k = _head_norm_t(k, kn_ref[...], cos, sin)
    k_ref[...] = k.T.astype(BF16)
    q_ref[0] = _head_norm_t(q, qn_ref[...], cos, sin, Q_FOLD).astype(BF16)
    gate_ref[0] = _silu(g)
    v_ref[0] = v.astype(BF16)


def _odd_proj(x_t, gain_col, w_t, cos_t, sin_t, qn, kn):
    batch, _, seq = x_t.shape
    tm = PROJ_TOKENS
    nt = seq // tm
    tokens = batch * seq
    odd_in = w_t.shape[0]

    def tok(b, i):
        return (b * nt + i, 0)

    def feat(b, i):
        return (b, 0, i)

    def const2(b, i):
        return (0, 0)

    def tab(b, i):
        return (0, i)

    return pl.pallas_call(
        _odd_proj_kernel,
        grid=(batch, nt),
        in_specs=[
            pl.BlockSpec((1, D_MODEL, tm), feat),
            pl.BlockSpec((D_MODEL, 1), const2),
            pl.BlockSpec((odd_in, D_MODEL), const2),
            pl.BlockSpec((HALF_DIM, tm), tab),
            pl.BlockSpec((HALF_DIM, tm), tab),
            pl.BlockSpec((HEAD_DIM, tm), const2),
            pl.BlockSpec((HEAD_DIM, tm), const2),
        ],
        out_specs=[
            pl.BlockSpec((1, C_QK, tm), feat),
            pl.BlockSpec((tm, C_QK), tok),
            pl.BlockSpec((1, C_W, tm), feat),
            pl.BlockSpec((1, C_W, tm), feat),
        ],
        out_shape=[
            jax.ShapeDtypeStruct((batch, C_QK, seq), BF16),
            jax.ShapeDtypeStruct((tokens, C_QK), BF16),
            jax.ShapeDtypeStruct((batch, C_W, seq), BF16),
            jax.ShapeDtypeStruct((batch, C_W, seq), F32),
        ],
        compiler_params=_params(("parallel", "parallel")),
        name="odd_proj",
    )(x_t, gain_col, w_t, cos_t, sin_t, qn, kn)


def _diff_attn_kernel(lambda_init, running_max, q_ref, k_ref, v_ref, gate_ref, subln_ref,
                      lq1_ref, lk1_ref, lq2_ref, lk2_ref, y_ref, acc1_ref, acc2_ref):
    tq, tk = DIFF_Q, DIFF_K
    i = pl.program_id(2)
    q = q_ref[0]
    row = lax.broadcasted_iota(jnp.int32, q.shape, 0)
    zero = jnp.zeros_like(q)
    q_pads = (jnp.where(row < HEAD_DIM, q, zero), jnp.where(row >= HEAD_DIM, q, zero))
    acc_refs = (acc1_ref, acc2_ref)

    def load_kv(block):
        start = pl.multiple_of(block * tk, tk)
        return k_ref[0, pl.ds(start, tk), :], v_ref[0, :, pl.ds(start, tk)]

    def diag_valid(d, n_cols):
        key_chunk = (lax.broadcasted_iota(jnp.int32, (tk, n_cols), 0) + d * tk) // CHUNK
        qry_chunk = (lax.broadcasted_iota(jnp.int32, (tk, n_cols), 1) + (tq - n_cols)) // CHUNK
        return key_chunk <= qry_chunk

    def online_step(k, v_t, carry, valid):
        new = []
        for c in range(2):
            m, l = carry[2 * c], carry[2 * c + 1]
            s = jnp.dot(k, q_pads[c], preferred_element_type=F32)
            if valid is not None:
                s = jnp.where(valid, s, NEG_INF)
            m_new = jnp.maximum(m, jnp.max(s, axis=0, keepdims=True))
            alpha = jnp.exp2(m - m_new)
            p = jnp.exp2(s - m_new)
            l_new = alpha * l + jnp.sum(p, axis=0, keepdims=True)
            pv = jnp.dot(v_t, p.astype(BF16), preferred_element_type=F32)
            acc_refs[c][...] = alpha * acc_refs[c][...] + pv
            new += [m_new, l_new]
        return tuple(new)

    def direct_probs(k, q_c, valid):
        p = jnp.exp2(jnp.dot(k, q_c, preferred_element_type=F32))
        if valid is not None:
            p = jnp.where(valid, p, 0.0)
        return p.astype(BF16), jnp.sum(p, axis=0, keepdims=True)

    def direct_blocks(blocks, l_sums):
        l_sums = list(l_sums)
        pv_sum = {}
        pending = []

        def emit_pv():
            v_t, c, col0, p = pending.pop(0)
            pv = jnp.dot(v_t, p, preferred_element_type=F32)
            pv_sum[(c, col0)] = pv_sum[(c, col0)] + pv if (c, col0) in pv_sum else pv

        for k, v_t, col0, valid in blocks:
            for c in range(2):
                p, total = direct_probs(k, q_pads[c][:, col0:], valid)
                if col0:
                    total = jnp.concatenate([jnp.zeros((1, col0), F32), total], axis=1)
                l_sums[c] = l_sums[c] + total
                pending.append((v_t, c, col0, p))
                if len(pending) > 1:
                    emit_pv()
        while pending:
            emit_pv()
        for (c, col0), pv in pv_sum.items():
            acc_refs[c][:, col0:] += pv
        return tuple(l_sums)

    diag = range(tq // tk)
    first_diag = i * len(diag)
    if running_max:
        acc1_ref[...] = jnp.zeros_like(acc1_ref)
        acc2_ref[...] = jnp.zeros_like(acc2_ref)
        init = (jnp.full((1, tq), NEG_INF, F32), jnp.zeros((1, tq), F32)) * 2
        carry = lax.fori_loop(0, first_diag,
                              lambda j, c: online_step(*load_kv(j), c, None), init)
        for d in diag:
            carry = online_step(*load_kv(first_diag + d), carry, diag_valid(d, tq))
        l1, l2 = carry[1], carry[3]
    else:
        acc1_ref[...] = jnp.zeros_like(acc1_ref)
        acc2_ref[...] = jnp.zeros_like(acc2_ref)

        def pair(t, l_sums):
            return direct_blocks([load_kv(t * len(diag) + u) + (0, None) for u in diag], l_sums)

        l_sums = lax.fori_loop(0, i, pair, (jnp.zeros((1, tq), F32),) * 2)
        l1, l2 = direct_blocks(
            [load_kv(first_diag + d) + (d * tk, diag_valid(d, tq - d * tk)) for d in diag], l_sums)

    lam = (jnp.exp(jnp.sum(lq1_ref[...] * lk1_ref[...], axis=-1, keepdims=True))
           - jnp.exp(jnp.sum(lq2_ref[...] * lk2_ref[...], axis=-1, keepdims=True))
           + lambda_init)
    o = acc1_ref[...] * (1.0 / l1) - lam * (acc2_ref[...] * (1.0 / l2))
    ms = jnp.mean(o * o, axis=0, keepdims=True)
    o = o * lax.rsqrt(ms + EPS) * subln_ref[...] * (1.0 - lambda_init)
    y_ref[0] = (o * gate_ref[0]).astype(BF16)


def _diff_attn(running_max, q_t, k3, v_t, gate_t, subln_tile, lq1, lk1, lq2, lk2, lambda_init):
    batch, seq = k3.shape[0], k3.shape[1]
    tq = DIFF_Q
    nq = seq // tq
    head_w = 2 * HEAD_DIM

    def blk(b, h, i):
        return (b, h, i)

    def const2(b, h, i):
        return (0, 0)

    return pl.pallas_call(
        functools.partial(_diff_attn_kernel, lambda_init, running_max),
        grid=(batch, C_HEADS, nq),
        in_specs=[
            pl.BlockSpec((1, head_w, tq), blk),
            pl.BlockSpec((1, seq, head_w), lambda b, h, i: (b, 0, h)),
            pl.BlockSpec((1, C_V_DIM, seq), lambda b, h, i: (b, h, 0)),
            pl.BlockSpec((1, C_V_DIM, tq), blk),
            pl.BlockSpec((C_V_DIM, tq), const2),
            pl.BlockSpec((1, HEAD_DIM), const2),
            pl.BlockSpec((1, HEAD_DIM), const2),
            pl.BlockSpec((1, HEAD_DIM), const2),
            pl.BlockSpec((1, HEAD_DIM), const2),
        ],
        out_specs=pl.BlockSpec((1, C_V_DIM, tq), blk),
        out_shape=jax.ShapeDtypeStruct((batch, C_W, seq), BF16),
        scratch_shapes=[pltpu.VMEM((C_V_DIM, tq), F32), pltpu.VMEM((C_V_DIM, tq), F32)],
        compiler_params=_params(("parallel", "parallel", "arbitrary")),
        name="diff_attn_online" if running_max else "diff_attn_direct",
    )(q_t, k3, v_t, gate_t, subln_tile, lq1, lk1, lq2, lk2)


def _qk_score_bound(q_gain, k_gain):
    bound = HEAD_DIM * QK_SCALE * jnp.max(jnp.abs(q_gain)) * jnp.max(jnp.abs(k_gain))
    return bound.astype(F32) * (1.0 + 2.0 ** -6)


def _rope_tables_t(seq):
    inv = 1.0 / (ROPE_THETA ** (jnp.arange(0, HEAD_DIM, 2, dtype=F32) / HEAD_DIM))
    ang = inv[:, None] * jnp.arange(seq, dtype=F32)[None, :]
    return jnp.cos(ang), jnp.sin(ang)


def _gain_tile(g, width):
    return jnp.broadcast_to(g.astype(F32)[:, None], (g.shape[0], width))


def _rel_bias_t(rel_table, n_q):
    n_k = B_PREV_ROWS + n_q
    width = n_k + n_q
    offset = np.arange(width) - (n_k - 1)
    rel = np.clip(B_PREV_ROWS + offset, -B_MAX_REL, B_MAX_REL) + B_MAX_REL
    by_offset = rel_table.astype(F32)[:, rel]
    heads = rel_table.shape[0]
    sub = 8

    def build(f_ref, o_ref):
        for h in range(heads):
            f = f_ref[h:h + 1, :]
            rows = jnp.concatenate(
                [pltpu.roll(f, (width + b - (sub - 1)) % width, axis=1) for b in range(sub)], axis=0)
            for blk in range(n_k // sub):
                lead = n_k - sub - sub * blk
                o_ref[h, sub * blk:sub * (blk + 1), :] = pltpu.roll(
                    rows, (width - lead) % width, axis=1)[:, :n_q]

    return pl.pallas_call(
        build,
        out_shape=jax.ShapeDtypeStruct((heads, n_k, n_q), F32),
        name="rel_bias_tile",
    )(by_offset)


def kernel(x, ev_norm, ev_w_in, ev_w_out, ev_a_q_norm, ev_a_k_norm, ev_a_sinks,
           ev_b_q_norm, ev_b_k_norm, ev_b_rel_bias, od_norm, od_w_in, od_w_out,
           od_q_norm, od_k_norm, od_lambda_q1, od_lambda_k1, od_lambda_q2,
           od_lambda_k2, od_subln):
    batch, seq, d = x.shape
    cos_t, sin_t = _rope_tables_t(seq)
    depth = ev_norm.shape[0] + od_norm.shape[0]
    assert ev_norm.shape[0] == od_norm.shape[0]
    x2d = x.reshape(batch * seq, d)
    for layer in range(depth):
        i = layer // 2
        if layer % 2 == 0:
            w_in_t = ev_w_in[i].T.astype(BF16)
            w_out_t = ev_w_out[i].T.astype(BF16)
            aq_t, ak, av_t, bq_t, bk, bv_t, gate_t = _even_proj(
                x2d, ev_norm[i][None, :], w_in_t, cos_t, sin_t,
                _gain_tile(ev_a_q_norm[i], PROJ_TOKENS), _gain_tile(ev_a_k_norm[i], PROJ_TOKENS),
                _gain_tile(ev_b_q_norm[i], PROJ_TOKENS), _gain_tile(ev_b_k_norm[i], PROJ_TOKENS),
                batch, seq)
            mask_a = jnp.where(_band_mask_variants(A_PREV_ROWS, MIX_TOKENS), 0.0, NEG_INF).astype(F32)
            bias_b = jnp.where(_band_mask_variants(B_PREV_ROWS, MIX_TOKENS)[:, None],
                               _rel_bias_t(ev_b_rel_bias[i] * LOG2E, MIX_TOKENS)[None], NEG_INF)
            sinks = ev_a_sinks[i].astype(F32)
            mix_args = (sinks * LOG2E, aq_t, ak.reshape(batch, seq, A_KV), av_t,
                        bq_t, bk.reshape(batch, seq, B_W), bv_t, gate_t, mask_a, bias_b)
            bound_a = jnp.maximum(_qk_score_bound(ev_a_q_norm[i], ev_a_k_norm[i]),
                                  jnp.max(jnp.abs(sinks)))
            bound_b = (_qk_score_bound(ev_b_q_norm[i], ev_b_k_norm[i])
                       + jnp.max(jnp.abs(ev_b_rel_bias[i])))
            y_t = lax.cond(jnp.maximum(bound_a, bound_b) <= SAFE_SCORE,
                           functools.partial(_even_mix, False),
                           functools.partial(_even_mix, True), *mix_args)
            x_t = _out_proj(False, x2d, y_t, w_out_t)
        else:
            lambda_init = 0.8 - 0.6 * math.exp(-0.3 * layer)
            w_in_t = od_w_in[i].T.astype(BF16)
            w_out_t = od_w_out[i].T.astype(BF16)
            q_t, k, v_t, gate_t = _odd_proj(
                x_t, od_norm[i][:, None], w_in_t, cos_t, sin_t,
                _gain_tile(od_q_norm[i], PROJ_TOKENS), _gain_tile(od_k_norm[i], PROJ_TOKENS))
            diff_args = (q_t, k.reshape(batch, seq, C_QK), v_t, gate_t,
                         _gain_tile(od_subln[i], DIFF_Q),
                         od_lambda_q1[i][None, :], od_lambda_k1[i][None, :],
                         od_lambda_q2[i][None, :], od_lambda_k2[i][None, :])
            y_t = lax.cond(
                _qk_score_bound(od_q_norm[i], od_k_norm[i]) <= SAFE_SCORE,
                lambda *a: _diff_attn(False, *a, lambda_init),
                lambda *a: _diff_attn(True, *a, lambda_init),
                *diff_args)
            x2d = _out_proj(True, x_t, y_t, w_out_t)
    return x2d.reshape(batch, seq, d)
```

```python
import functools
import math

import jax
import jax.numpy as jnp
import numpy as np
from jax import lax
from jax.experimental import pallas as pl
from jax.experimental.pallas import tpu as pltpu

F32 = jnp.float32
BF16 = jnp.bfloat16

D_MODEL = 1024
CHUNK = 64
HEAD_DIM = 64
HALF_DIM = HEAD_DIM // 2
ROPE_THETA = 10000.0
EPS = 1e-6
NEG_INF = -1e30
QK_SCALE = 1.0 / math.sqrt(HEAD_DIM)
LOG2E = math.log2(math.e)
Q_FOLD = QK_SCALE * LOG2E

A_Q_HEADS = 8
A_KV_HEADS = 2
A_PREV_CHUNKS = 2
B_HEADS = 8
B_PREV_CHUNKS = 8
B_MAX_REL = 128
C_HEADS = 8
C_V_DIM = 2 * HEAD_DIM

A_Q = A_Q_HEADS * HEAD_DIM
A_KV = A_KV_HEADS * HEAD_DIM
B_W = B_HEADS * HEAD_DIM
EVEN_MIX = A_Q + B_W
C_QK = C_HEADS * 2 * HEAD_DIM
C_W = C_HEADS * C_V_DIM
PV_SUM_ROWS = 16
BAND_V_ROWS = HEAD_DIM + PV_SUM_ROWS
A_V_ROWS = A_KV_HEADS * BAND_V_ROWS
B_V_ROWS = B_HEADS * BAND_V_ROWS

V7X_VMEM_LIMIT_BYTES = 56 * 1024 * 1024

PROJ_TOKENS = 512
MIX_TOKENS = 256
MIX_PV_LAG = 3
A_PREV_ROWS = A_PREV_CHUNKS * CHUNK
B_PREV_ROWS = B_PREV_CHUNKS * CHUNK
DIFF_Q = 2048
DIFF_K = 512
DIFF_DIAG_K = 256
SAFE_SCORE = 40.0


def _params(semantics):
    return pltpu.CompilerParams(dimension_semantics=semantics,
                                vmem_limit_bytes=V7X_VMEM_LIMIT_BYTES)


def _rms_rows(x, gain_row):
    ms = jnp.mean(x * x, axis=-1, keepdims=True)
    return x * lax.rsqrt(ms + EPS) * gain_row


def _proj_t(w_t_ref, lo, width, h):
    return lax.dot_general(w_t_ref[lo:lo + width, :], h,
                           (((1,), (1,)), ((), ())),
                           preferred_element_type=F32)


def _head_norm_t(p, gain_tile, cos=None, sin=None, scale=1.0):
    width, tm = p.shape
    y = p.reshape(width // HEAD_DIM, HEAD_DIM, tm)
    ms = jnp.mean(y * y, axis=1, keepdims=True)
    y = y * lax.rsqrt(ms + EPS) * gain_tile[None]
    if cos is not None:
        y1 = y[:, :HALF_DIM]
        y2 = y[:, HALF_DIM:]
        c = cos[None]
        s = sin[None]
        y = jnp.concatenate([y1 * c - y2 * s, y2 * c + y1 * s], axis=1)
    if scale != 1.0:
        y = y * scale
    return y.reshape(width, tm)


def _silu(x):
    return x * (1.0 / (1.0 + jnp.exp(-x)))


def _even_proj_kernel(x_ref, g_ref, w_t_ref, cos_ref, sin_ref,
                      aqn_ref, akn_ref, bqn_ref, bkn_ref,
                      aq_ref, ak_ref, av_ref, bq_ref, bk_ref, bv_ref, gate_ref):
    h = _rms_rows(x_ref[...], g_ref[...]).astype(BF16)
    cos = cos_ref[...]
    sin = sin_ref[...]
    a_w = 2 * A_Q + 2 * A_KV
    a_all = _proj_t(w_t_ref, 0, a_w, h)
    b_qk = _proj_t(w_t_ref, a_w, 2 * B_W, h)
    b_vg = _proj_t(w_t_ref, a_w + 2 * B_W, 2 * B_W, h)
    ak = _head_norm_t(a_all[A_Q:A_Q + A_KV], akn_ref[...], cos, sin)
    ak_ref[...] = ak.T.astype(BF16)
    aq_ref[0] = _head_norm_t(a_all[:A_Q], aqn_ref[...], cos, sin, Q_FOLD).astype(BF16)
    _store_v_with_ones(av_ref, a_all[A_Q + A_KV:A_Q + 2 * A_KV])
    gate_ref[0, :A_Q, :] = _silu(a_all[A_Q + 2 * A_KV:])
    bk = _head_norm_t(b_qk[B_W:], bkn_ref[...])
    bk_ref[...] = bk.T.astype(BF16)
    bq_ref[0] = _head_norm_t(b_qk[:B_W], bqn_ref[...], scale=Q_FOLD).astype(BF16)
    gate_ref[0, A_Q:, :] = _silu(b_vg[B_W:])
    _store_v_with_ones(bv_ref, b_vg[:B_W])


def _store_v_with_ones(v_ref, v):
    tm = v.shape[1]
    ones = jnp.ones((PV_SUM_ROWS, tm), BF16)
    for h in range(v.shape[0] // HEAD_DIM):
        v_ref[0, h * BAND_V_ROWS:h * BAND_V_ROWS + HEAD_DIM, :] = (
            v[h * HEAD_DIM:(h + 1) * HEAD_DIM].astype(BF16))
        v_ref[0, h * BAND_V_ROWS + HEAD_DIM:(h + 1) * BAND_V_ROWS, :] = ones


def _even_proj(x2d, gain, w_t, cos_t, sin_t, aqn, akn, bqn, bkn, batch, seq):
    tm = PROJ_TOKENS
    nt = seq // tm
    tokens = batch * seq
    even_in = w_t.shape[0]

    def tok(b, i):
        return (b * nt + i, 0)

    def feat(b, i):
        return (b, 0, i)

    def const2(b, i):
        return (0, 0)

    def tab(b, i):
        return (0, i)

    in_specs = [
        pl.BlockSpec((tm, D_MODEL), tok),
        pl.BlockSpec((1, D_MODEL), const2),
        pl.BlockSpec((even_in, D_MODEL), const2),
        pl.BlockSpec((HALF_DIM, tm), tab),
        pl.BlockSpec((HALF_DIM, tm), tab),
        pl.BlockSpec((HEAD_DIM, tm), const2),
        pl.BlockSpec((HEAD_DIM, tm), const2),
        pl.BlockSpec((HEAD_DIM, tm), const2),
        pl.BlockSpec((HEAD_DIM, tm), const2),
    ]
    out_shape = [
        jax.ShapeDtypeStruct((batch, A_Q, seq), BF16),
        jax.ShapeDtypeStruct((tokens, A_KV), BF16),
        jax.ShapeDtypeStruct((batch, A_V_ROWS, seq), BF16),
        jax.ShapeDtypeStruct((batch, B_W, seq), BF16),
        jax.ShapeDtypeStruct((tokens, B_W), BF16),
        jax.ShapeDtypeStruct((batch, B_V_ROWS, seq), BF16),
        jax.ShapeDtypeStruct((batch, EVEN_MIX, seq), F32),
    ]
    out_specs = [
        pl.BlockSpec((1, A_Q, tm), feat),
        pl.BlockSpec((tm, A_KV), tok),
        pl.BlockSpec((1, A_V_ROWS, tm), feat),
        pl.BlockSpec((1, B_W, tm), feat),
        pl.BlockSpec((tm, B_W), tok),
        pl.BlockSpec((1, B_V_ROWS, tm), feat),
        pl.BlockSpec((1, EVEN_MIX, tm), feat),
    ]
    return pl.pallas_call(
        _even_proj_kernel,
        grid=(batch, nt),
        in_specs=in_specs,
        out_specs=out_specs,
        out_shape=out_shape,
        compiler_params=_params(("parallel", "parallel")),
        name="even_proj",
    )(x2d, gain, w_t, cos_t, sin_t, aqn, akn, bqn, bkn)


def _band_mask_np(prev_rows, n_q, q_block):
    key = np.arange(prev_rows + n_q)[:, None]
    qry = np.arange(n_q)[None, :] + prev_rows
    return ((key // CHUNK <= qry // CHUNK)
            & (key // CHUNK >= qry // CHUNK - prev_rows // CHUNK)
            & (key >= prev_rows - q_block * n_q))


def _band_mask_variants(prev_rows, n_q):
    n = -(-prev_rows // n_q) + 1
    return np.stack([_band_mask_np(prev_rows, n_q, v) for v in range(n)])


def _band_probs(running_max, k_win, q_pad, add_tile, sink):
    s = jnp.dot(k_win, q_pad, preferred_element_type=F32) + add_tile
    if running_max:
        m = jnp.max(s, axis=0, keepdims=True)
        if sink is not None:
            m = jnp.maximum(m, sink)
        s = s - m
        sink = None if sink is None else sink - m
    sink_term = None if sink is None else jnp.exp2(jnp.broadcast_to(sink, (1, s.shape[1])))
    return jnp.exp2(s).astype(BF16), sink_term


def _pad_q(q_h, half):
    z = jnp.zeros_like(q_h)
    return jnp.concatenate([q_h, z] if half == 0 else [z, q_h], axis=0)


def _even_mix_kernel(running_max, sinks_ref, aq_ref, akp_ref, akc_ref, avp_ref, avc_ref,
                     bq_ref, bkp2_ref, bkp1_ref, bkc_ref, bvp2_ref, bvp1_ref, bvc_ref,
                     gate_ref, mask_ref, bias_ref, y_ref):
    ak = jnp.concatenate([akp_ref[0], akc_ref[0]], axis=0)
    av = jnp.concatenate([avp_ref[0], avc_ref[0]], axis=1)
    bk = jnp.concatenate([bkp2_ref[0], bkp1_ref[0], bkc_ref[0]], axis=0)
    bv = jnp.concatenate([bvp2_ref[0], bvp1_ref[0], bvc_ref[0]], axis=1)

    group = A_Q_HEADS // A_KV_HEADS
    heads = []
    for h in range(A_Q_HEADS):
        kv = h // group
        heads.append((ak, _pad_q(aq_ref[0, h * HEAD_DIM:(h + 1) * HEAD_DIM, :], kv),
                      av[kv * BAND_V_ROWS:(kv + 1) * BAND_V_ROWS, :], mask_ref[0], sinks_ref[h],
                      h * HEAD_DIM))
    for h in range(B_HEADS):
        pair = h // 2
        heads.append((bk[:, pair * 128:(pair + 1) * 128],
                      _pad_q(bq_ref[0, h * HEAD_DIM:(h + 1) * HEAD_DIM, :], h % 2),
                      bv[h * BAND_V_ROWS:(h + 1) * BAND_V_ROWS, :], bias_ref[0, h], None,
                      A_Q + h * HEAD_DIM))

    pending = []

    def emit_pv():
        v_t, p, sink_term, row0 = pending.pop(0)
        o = jnp.dot(v_t, p, preferred_element_type=F32)
        l = o[HEAD_DIM:HEAD_DIM + 1]
        if sink_term is not None:
            l = l + sink_term
        rows = slice(row0, row0 + HEAD_DIM)
        y_ref[0, rows, :] = (o[:HEAD_DIM] * (1.0 / l) * gate_ref[0, rows, :]).astype(BF16)

    for k_win, q_pad, v_t, add_tile, sink, row0 in heads:
        p, sink_term = _band_probs(running_max, k_win, q_pad, add_tile, sink)
        pending.append((v_t, p, sink_term, row0))
        if len(pending) > MIX_PV_LAG:
            emit_pv()
    while pending:
        emit_pv()


def _even_mix(running_max, sinks, aq_t, ak3, av_t, bq_t, bk3, bv_t, gate_t, mask_a, bias_b):
    batch, seq = ak3.shape[0], ak3.shape[1]
    tq = MIX_TOKENS
    nq = seq // tq
    a_ratio = tq // A_PREV_ROWS

    def cur_t(b, i, *_):
        return (b, 0, i)

    def cur_n(b, i, *_):
        return (b, i, 0)

    def a_prev_n(b, i, *_):
        return (b, jnp.maximum(i * a_ratio - 1, 0), 0)

    def a_prev_t(b, i, *_):
        return (b, 0, jnp.maximum(i * a_ratio - 1, 0))

    def b_prev_n(back):
        return lambda b, i, *_: (b, jnp.maximum(i - back, 0), 0)

    def b_prev_t(back):
        return lambda b, i, *_: (b, 0, jnp.maximum(i - back, 0))

    in_specs = [
        pl.BlockSpec((1, A_Q, tq), cur_t),
        pl.BlockSpec((1, A_PREV_ROWS, A_KV), a_prev_n),
        pl.BlockSpec((1, tq, A_KV), cur_n),
        pl.BlockSpec((1, A_V_ROWS, A_PREV_ROWS), a_prev_t),
        pl.BlockSpec((1, A_V_ROWS, tq), cur_t),
        pl.BlockSpec((1, B_W, tq), cur_t),
        pl.BlockSpec((1, tq, B_W), b_prev_n(2)),
        pl.BlockSpec((1, tq, B_W), b_prev_n(1)),
        pl.BlockSpec((1, tq, B_W), cur_n),
        pl.BlockSpec((1, B_V_ROWS, tq), b_prev_t(2)),
        pl.BlockSpec((1, B_V_ROWS, tq), b_prev_t(1)),
        pl.BlockSpec((1, B_V_ROWS, tq), cur_t),
        pl.BlockSpec((1, EVEN_MIX, tq), cur_t),
        pl.BlockSpec((1,) + mask_a.shape[1:],
                     lambda b, i, *_: (jnp.minimum(i, mask_a.shape[0] - 1), 0, 0)),
        pl.BlockSpec((1,) + bias_b.shape[1:],
                     lambda b, i, *_: (jnp.minimum(i, bias_b.shape[0] - 1), 0, 0, 0)),
    ]
    grid_spec = pltpu.PrefetchScalarGridSpec(
        num_scalar_prefetch=1,
        grid=(batch, nq),
        in_specs=in_specs,
        out_specs=pl.BlockSpec((1, EVEN_MIX, tq), cur_t),
    )
    return pl.pallas_call(
        functools.partial(_even_mix_kernel, running_max),
        grid_spec=grid_spec,
        out_shape=jax.ShapeDtypeStruct((batch, EVEN_MIX, seq), BF16),
        compiler_params=_params(("parallel", "parallel")),
        name="even_mix_online" if running_max else "even_mix_direct",
    )(sinks, aq_t, ak3, ak3, av_t, av_t, bq_t, bk3, bk3, bk3, bv_t, bv_t, bv_t,
      gate_t, mask_a, bias_b)


def _out_proj_kernel(x_feature_major, x_ref, y_ref, w_t_ref, o_ref):
    out_t = jnp.dot(w_t_ref[...], y_ref[0], preferred_element_type=F32)
    if x_feature_major:
        o_ref[...] = (x_ref[0] + out_t).T
    else:
        o_ref[0] = x_ref[...].T + out_t


def _out_proj(x_feature_major, x, y_t, w_t):
    batch, width, seq = y_t.shape
    tm = PROJ_TOKENS
    nt = seq // tm
    tok_spec = pl.BlockSpec((tm, D_MODEL), lambda b, i: (b * nt + i, 0))
    feat_spec = pl.BlockSpec((1, D_MODEL, tm), lambda b, i: (b, 0, i))
    tok_shape = jax.ShapeDtypeStruct((batch * seq, D_MODEL), F32)
    feat_shape = jax.ShapeDtypeStruct((batch, D_MODEL, seq), F32)
    return pl.pallas_call(
        functools.partial(_out_proj_kernel, x_feature_major),
        grid=(batch, nt),
        in_specs=[
            feat_spec if x_feature_major else tok_spec,
            pl.BlockSpec((1, width, tm), lambda b, i: (b, 0, i)),
            pl.BlockSpec((D_MODEL, width), lambda b, i: (0, 0)),
        ],
        out_specs=tok_spec if x_feature_major else feat_spec,
        out_shape=tok_shape if x_feature_major else feat_shape,
        compiler_params=_params(("parallel", "parallel")),
        name="out_proj_to_tokens" if x_feature_major else "out_proj_to_features",
    )(x, y_t, w_t)


def _odd_proj_kernel(x_ref, g_ref, w_t_ref, cos_ref, sin_ref, qn_ref, kn_ref,
                     q_ref, k_ref, v_ref, gate_ref):
    x = x_ref[0]
    ms = jnp.mean(x * x, axis=0, keepdims=True)
    h_t = (x * lax.rsqrt(ms + EPS) * g_ref[...]).astype(BF16)
    cos = cos_ref[...]
    sin = sin_ref[...]

    def proj(lo, width):
        return jnp.dot(w_t_ref[lo:lo + width, :], h_t, preferred_element_type=F32)

    k = proj(C_QK, C_QK)
    q = proj(0, C_QK)
    g = proj(2 * C_QK + C_W, C_W)
    v = proj(2 * C_QK, C_W)
    k = _head_norm_t(k, kn_ref[...], cos, sin)
    k_ref[...] = k.T.astype(BF16)
    q_ref[0] = _head_norm_t(q, qn_ref[...], cos, sin, Q_FOLD).astype(BF16)
    gate_ref[0] = _silu(g)
    v_ref[0] = v.astype(BF16)


def _odd_proj(x_t, gain_col, w_t, cos_t, sin_t, qn, kn):
    batch, _, seq = x_t.shape
    tm = PROJ_TOKENS
    nt = seq // tm
    tokens = batch * seq
    odd_in = w_t.shape[0]

    def tok(b, i):
        return (b * nt + i, 0)

    def feat(b, i):
        return (b, 0, i)

    def const2(b, i):
        return (0, 0)

    def tab(b, i):
        return (0, i)

    return pl.pallas_call(
        _odd_proj_kernel,
        grid=(batch, nt),
        in_specs=[
            pl.BlockSpec((1, D_MODEL, tm), feat),
            pl.BlockSpec((D_MODEL, 1), const2),
            pl.BlockSpec((odd_in, D_MODEL), const2),
            pl.BlockSpec((HALF_DIM, tm), tab),
            pl.BlockSpec((HALF_DIM, tm), tab),
            pl.BlockSpec((HEAD_DIM, tm), const2),
            pl.BlockSpec((HEAD_DIM, tm), const2),
        ],
        out_specs=[
            pl.BlockSpec((1, C_QK, tm), feat),
            pl.BlockSpec((tm, C_QK), tok),
            pl.BlockSpec((1, C_W, tm), feat),
            pl.BlockSpec((1, C_W, tm), feat),
        ],
        out_shape=[
            jax.ShapeDtypeStruct((batch, C_QK, seq), BF16),
            jax.ShapeDtypeStruct((tokens, C_QK), BF16),
            jax.ShapeDtypeStruct((batch, C_W, seq), BF16),
            jax.ShapeDtypeStruct((batch, C_W, seq), F32),
        ],
        compiler_params=_params(("parallel", "parallel")),
        name="odd_proj",
    )(x_t, gain_col, w_t, cos_t, sin_t, qn, kn)


def _diff_attn_kernel(lambda_init, running_max, q_ref, k_ref, v_ref, gate_ref, subln_ref,
                      lq1_ref, lk1_ref, lq2_ref, lk2_ref, y_ref, acc1_ref, acc2_ref):
    tq, tk = DIFF_Q, DIFF_K
    i = pl.program_id(2)
    q = q_ref[0]
    row = lax.broadcasted_iota(jnp.int32, q.shape, 0)
    zero = jnp.zeros_like(q)
    q_pads = (jnp.where(row < HEAD_DIM, q, zero), jnp.where(row >= HEAD_DIM, q, zero))
    acc_refs = (acc1_ref, acc2_ref)

    def load_kv(start, size):
        start = pl.multiple_of(start, size)
        return k_ref[0, pl.ds(start, size), :], v_ref[0, :, pl.ds(start, size)]

    def diag_valid(key0, size, n_cols):
        key_chunk = (lax.broadcasted_iota(jnp.int32, (size, n_cols), 0) + key0) // CHUNK
        qry_chunk = (lax.broadcasted_iota(jnp.int32, (size, n_cols), 1) + (tq - n_cols)) // CHUNK
        return key_chunk <= qry_chunk

    def online_step(k, v_t, carry, valid):
        new = []
        for c in range(2):
            m, l = carry[2 * c], carry[2 * c + 1]
            s = jnp.dot(k, q_pads[c], preferred_element_type=F32)
            if valid is not None:
                s = jnp.where(valid, s, NEG_INF)
            m_new = jnp.maximum(m, jnp.max(s, axis=0, keepdims=True))
            alpha = jnp.exp2(m - m_new)
            p = jnp.exp2(s - m_new)
            l_new = alpha * l + jnp.sum(p, axis=0, keepdims=True)
            pv = jnp.dot(v_t, p.astype(BF16), preferred_element_type=F32)
            acc_refs[c][...] = alpha * acc_refs[c][...] + pv
            new += [m_new, l_new]
        return tuple(new)

    def direct_probs(k, q_c, valid):
        p = jnp.exp2(jnp.dot(k, q_c, preferred_element_type=F32))
        if valid is not None:
            p = jnp.where(valid, p, 0.0)
        return p.astype(BF16), jnp.sum(p, axis=0, keepdims=True)

    def direct_blocks(blocks, l_sums):
        l_sums = list(l_sums)
        pv_sum = {}
        pending = []

        edges = sorted({col0 for _, _, col0, _ in blocks}) + [tq]

        def emit_pv():
            v_t, c, col0, p = pending.pop(0)
            pv = jnp.dot(v_t, p, preferred_element_type=F32)
            for lo, hi in zip(edges[:-1], edges[1:]):
                if lo >= col0:
                    part = pv[:, lo - col0:hi - col0]
                    pv_sum[(c, lo, hi)] = pv_sum[(c, lo, hi)] + part if (c, lo, hi) in pv_sum else part

        for k, v_t, col0, valid in blocks:
            for c in range(2):
                p, total = direct_probs(k, q_pads[c][:, col0:], valid)
                if col0:
                    total = jnp.concatenate([jnp.zeros((1, col0), F32), total], axis=1)
                l_sums[c] = l_sums[c] + total
                pending.append((v_t, c, col0, p))
                if len(pending) > 1:
                    emit_pv()
        while pending:
            emit_pv()
        for (c, lo, hi), pv in pv_sum.items():
            acc_refs[c][:, lo:hi] += pv
        return tuple(l_sums)

    q0 = i * tq
    acc1_ref[...] = jnp.zeros_like(acc1_ref)
    acc2_ref[...] = jnp.zeros_like(acc2_ref)
    if running_max:
        init = (jnp.full((1, tq), NEG_INF, F32), jnp.zeros((1, tq), F32)) * 2
        carry = lax.fori_loop(0, q0 // tk,
                              lambda j, c: online_step(*load_kv(j * tk, tk), c, None), init)
        for key0 in range(0, tq, tk):
            carry = online_step(*load_kv(q0 + key0, tk), carry, diag_valid(key0, tk, tq))
        l1, l2 = carry[1], carry[3]
    else:
        def group(t, l_sums):
            return direct_blocks(
                [load_kv(t * tq + u * tk, tk) + (0, None) for u in range(tq // tk)], l_sums)

        l_sums = lax.fori_loop(0, i, group, (jnp.zeros((1, tq), F32),) * 2)
        td = DIFF_DIAG_K
        l1, l2 = direct_blocks(
            [load_kv(q0 + key0, td) + (key0, diag_valid(key0, td, tq - key0))
             for key0 in range(0, tq, td)], l_sums)

    lam = (jnp.exp(jnp.sum(lq1_ref[...] * lk1_ref[...], axis=-1, keepdims=True))
           - jnp.exp(jnp.sum(lq2_ref[...] * lk2_ref[...], axis=-1, keepdims=True))
           + lambda_init)
    o = acc1_ref[...] * (1.0 / l1) - lam * (acc2_ref[...] * (1.0 / l2))
    ms = jnp.mean(o * o, axis=0, keepdims=True)
    o = o * lax.rsqrt(ms + EPS) * subln_ref[...] * (1.0 - lambda_init)
    y_ref[0] = (o * gate_ref[0]).astype(BF16)


def _diff_attn(running_max, q_t, k3, v_t, gate_t, subln_tile, lq1, lk1, lq2, lk2, lambda_init):
    batch, seq = k3.shape[0], k3.shape[1]
    tq = DIFF_Q
    nq = seq // tq
    head_w = 2 * HEAD_DIM

    def blk(b, h, i):
        return (b, h, i)

    def const2(b, h, i):
        return (0, 0)

    return pl.pallas_call(
        functools.partial(_diff_attn_kernel, lambda_init, running_max),
        grid=(batch, C_HEADS, nq),
        in_specs=[
            pl.BlockSpec((1, head_w, tq), blk),
            pl.BlockSpec((1, seq, head_w), lambda b, h, i: (b, 0, h)),
            pl.BlockSpec((1, C_V_DIM, seq), lambda b, h, i: (b, h, 0)),
            pl.BlockSpec((1, C_V_DIM, tq), blk),
            pl.BlockSpec((C_V_DIM, tq), const2),
            pl.BlockSpec((1, HEAD_DIM), const2),
            pl.BlockSpec((1, HEAD_DIM), const2),
            pl.BlockSpec((1, HEAD_DIM), const2),
            pl.BlockSpec((1, HEAD_DIM), const2),
        ],
        out_specs=pl.BlockSpec((1, C_V_DIM, tq), blk),
        out_shape=jax.ShapeDtypeStruct((batch, C_W, seq), BF16),
        scratch_shapes=[pltpu.VMEM((C_V_DIM, tq), F32), pltpu.VMEM((C_V_DIM, tq), F32)],
        compiler_params=_params(("parallel", "parallel", "arbitrary")),
        name="diff_attn_online" if running_max else "diff_attn_direct",
    )(q_t, k3, v_t, gate_t, subln_tile, lq1, lk1, lq2, lk2)


def _qk_score_bound(q_gain, k_gain):
    bound = HEAD_DIM * QK_SCALE * jnp.max(jnp.abs(q_gain)) * jnp.max(jnp.abs(k_gain))
    return bound.astype(F32) * (1.0 + 2.0 ** -6)


def _rope_tables_t(seq):
    inv = 1.0 / (ROPE_THETA ** (jnp.arange(0, HEAD_DIM, 2, dtype=F32) / HEAD_DIM))
    ang = inv[:, None] * jnp.arange(seq, dtype=F32)[None, :]
    return jnp.cos(ang), jnp.sin(ang)


def _gain_tile(g, width):
    return jnp.broadcast_to(g.astype(F32)[:, None], (g.shape[0], width))


def _rel_bias_t(rel_table, n_q):
    n_k = B_PREV_ROWS + n_q
    width = n_k + n_q
    offset = np.arange(width) - (n_k - 1)
    rel = np.clip(B_PREV_ROWS + offset, -B_MAX_REL, B_MAX_REL) + B_MAX_REL
    by_offset = rel_table.astype(F32)[:, rel]
    heads = rel_table.shape[0]
    sub = 8

    def build(f_ref, o_ref):
        for h in range(heads):
            f = f_ref[h:h + 1, :]
            rows = jnp.concatenate(
                [pltpu.roll(f, (width + b - (sub - 1)) % width, axis=1) for b in range(sub)], axis=0)
            for blk in range(n_k // sub):
                lead = n_k - sub - sub * blk
                o_ref[h, sub * blk:sub * (blk + 1), :] = pltpu.roll(
                    rows, (width - lead) % width, axis=1)[:, :n_q]

    return pl.pallas_call(
        build,
        out_shape=jax.ShapeDtypeStruct((heads, n_k, n_q), F32),
        name="rel_bias_tile",
    )(by_offset)


def kernel(x, ev_norm, ev_w_in, ev_w_out, ev_a_q_norm, ev_a_k_norm, ev_a_sinks,
           ev_b_q_norm, ev_b_k_norm, ev_b_rel_bias, od_norm, od_w_in, od_w_out,
           od_q_norm, od_k_norm, od_lambda_q1, od_lambda_k1, od_lambda_q2,
           od_lambda_k2, od_subln):
    batch, seq, d = x.shape
    cos_t, sin_t = _rope_tables_t(seq)
    depth = ev_norm.shape[0] + od_norm.shape[0]
    assert ev_norm.shape[0] == od_norm.shape[0]
    x2d = x.reshape(batch * seq, d)
    for layer in range(depth):
        i = layer // 2
        if layer % 2 == 0:
            w_in_t = ev_w_in[i].T.astype(BF16)
            w_out_t = ev_w_out[i].T.astype(BF16)
            aq_t, ak, av_t, bq_t, bk, bv_t, gate_t = _even_proj(
                x2d, ev_norm[i][None, :], w_in_t, cos_t, sin_t,
                _gain_tile(ev_a_q_norm[i], PROJ_TOKENS), _gain_tile(ev_a_k_norm[i], PROJ_TOKENS),
                _gain_tile(ev_b_q_norm[i], PROJ_TOKENS), _gain_tile(ev_b_k_norm[i], PROJ_TOKENS),
                batch, seq)
            mask_a = jnp.where(_band_mask_variants(A_PREV_ROWS, MIX_TOKENS), 0.0, NEG_INF).astype(F32)
            bias_b = jnp.where(_band_mask_variants(B_PREV_ROWS, MIX_TOKENS)[:, None],
                               _rel_bias_t(ev_b_rel_bias[i] * LOG2E, MIX_TOKENS)[None], NEG_INF)
            sinks = ev_a_sinks[i].astype(F32)
            mix_args = (sinks * LOG2E, aq_t, ak.reshape(batch, seq, A_KV), av_t,
                        bq_t, bk.reshape(batch, seq, B_W), bv_t, gate_t, mask_a, bias_b)
            bound_a = jnp.maximum(_qk_score_bound(ev_a_q_norm[i], ev_a_k_norm[i]),
                                  jnp.max(jnp.abs(sinks)))
            bound_b = (_qk_score_bound(ev_b_q_norm[i], ev_b_k_norm[i])
                       + jnp.max(jnp.abs(ev_b_rel_bias[i])))
            y_t = lax.cond(jnp.maximum(bound_a, bound_b) <= SAFE_SCORE,
                           functools.partial(_even_mix, False),
                           functools.partial(_even_mix, True), *mix_args)
            x_t = _out_proj(False, x2d, y_t, w_out_t)
        else:
            lambda_init = 0.8 - 0.6 * math.exp(-0.3 * layer)
            w_in_t = od_w_in[i].T.astype(BF16)
            w_out_t = od_w_out[i].T.astype(BF16)
            q_t, k, v_t, gate_t = _odd_proj(
                x_t, od_norm[i][:, None], w_in_t, cos_t, sin_t,
                _gain_tile(od_q_norm[i], PROJ_TOKENS), _gain_tile(od_k_norm[i], PROJ_TOKENS))
            diff_args = (q_t, k.reshape(batch, seq, C_QK), v_t, gate_t,
                         _gain_tile(od_subln[i], DIFF_Q),
                         od_lambda_q1[i][None, :], od_lambda_k1[i][None, :],
                         od_lambda_q2[i][None, :], od_lambda_k2[i][None, :])
            y_t = lax.cond(
                _qk_score_bound(od_q_norm[i], od_k_norm[i]) <= SAFE_SCORE,
                lambda *a: _diff_attn(False, *a, lambda_init),
                lambda *a: _diff_attn(True, *a, lambda_init),
                *diff_args)
            x2d = _out_proj(True, x_t, y_t, w_out_t)
    return x2d.reshape(batch, seq, d)
```

```python
import functools
import math

import jax
import jax.numpy as jnp
import numpy as np
from jax import lax
from jax.experimental import pallas as pl
from jax.experimental.pallas import tpu as pltpu

F32 = jnp.float32
BF16 = jnp.bfloat16

D_MODEL = 1024
CHUNK = 64
HEAD_DIM = 64
HALF_DIM = HEAD_DIM // 2
ROPE_THETA = 10000.0
EPS = 1e-6
NEG_INF = -1e30
QK_SCALE = 1.0 / math.sqrt(HEAD_DIM)
LOG2E = math.log2(math.e)
Q_FOLD = QK_SCALE * LOG2E

A_Q_HEADS = 8
A_KV_HEADS = 2
A_PREV_CHUNKS = 2
B_HEADS = 8
B_PREV_CHUNKS = 8
B_MAX_REL = 128
C_HEADS = 8
C_V_DIM = 2 * HEAD_DIM

A_Q = A_Q_HEADS * HEAD_DIM
A_KV = A_KV_HEADS * HEAD_DIM
B_W = B_HEADS * HEAD_DIM
EVEN_MIX = A_Q + B_W
C_QK = C_HEADS * 2 * HEAD_DIM
C_W = C_HEADS * C_V_DIM
PV_SUM_ROWS = 16
BAND_V_ROWS = HEAD_DIM + PV_SUM_ROWS
A_V_ROWS = A_KV_HEADS * BAND_V_ROWS
B_V_ROWS = B_HEADS * BAND_V_ROWS

V7X_VMEM_LIMIT_BYTES = 56 * 1024 * 1024

PROJ_TOKENS = 512
MIX_TOKENS = 256
MIX_PV_LAG = 3
A_PREV_ROWS = A_PREV_CHUNKS * CHUNK
B_PREV_ROWS = B_PREV_CHUNKS * CHUNK
DIFF_Q = 2048
DIFF_K = 512
DIFF_DIAG_K = 256
SAFE_SCORE = 40.0


def _params(semantics):
    return pltpu.CompilerParams(dimension_semantics=semantics,
                                vmem_limit_bytes=V7X_VMEM_LIMIT_BYTES)


def _rms_rows(x, gain_row):
    ms = jnp.mean(x * x, axis=-1, keepdims=True)
    return x * lax.rsqrt(ms + EPS) * gain_row


def _proj_t(w_t_ref, lo, width, h):
    return lax.dot_general(w_t_ref[lo:lo + width, :], h,
                           (((1,), (1,)), ((), ())),
                           preferred_element_type=F32)


def _head_norm_t(p, gain_tile, cos=None, sin=None, scale=1.0):
    width, tm = p.shape
    y = p.reshape(width // HEAD_DIM, HEAD_DIM, tm)
    ms = jnp.mean(y * y, axis=1, keepdims=True)
    y = y * lax.rsqrt(ms + EPS) * gain_tile[None]
    if cos is not None:
        y1 = y[:, :HALF_DIM]
        y2 = y[:, HALF_DIM:]
        c = cos[None]
        s = sin[None]
        y = jnp.concatenate([y1 * c - y2 * s, y2 * c + y1 * s], axis=1)
    if scale != 1.0:
        y = y * scale
    return y.reshape(width, tm)


def _silu(x):
    return x * (1.0 / (1.0 + jnp.exp(-x)))


def _even_proj_kernel(x_ref, g_ref, w_t_ref, cos_ref, sin_ref,
                      aqn_ref, akn_ref, bqn_ref, bkn_ref,
                      aq_ref, ak_ref, av_ref, bq_ref, bk_ref, bv_ref, gate_ref):
    h = _rms_rows(x_ref[...], g_ref[...]).astype(BF16)
    cos = cos_ref[...]
    sin = sin_ref[...]
    a_w = 2 * A_Q + 2 * A_KV
    a_all = _proj_t(w_t_ref, 0, a_w, h)
    b_qk = _proj_t(w_t_ref, a_w, 2 * B_W, h)
    b_vg = _proj_t(w_t_ref, a_w + 2 * B_W, 2 * B_W, h)
    ak = _head_norm_t(a_all[A_Q:A_Q + A_KV], akn_ref[...], cos, sin)
    ak_ref[...] = ak.T.astype(BF16)
    aq_ref[0] = _head_norm_t(a_all[:A_Q], aqn_ref[...], cos, sin, Q_FOLD).astype(BF16)
    _store_v_with_ones(av_ref, a_all[A_Q + A_KV:A_Q + 2 * A_KV])
    gate_ref[0, :A_Q, :] = _silu(a_all[A_Q + 2 * A_KV:])
    bk = _head_norm_t(b_qk[B_W:], bkn_ref[...])
    bk_ref[...] = bk.T.astype(BF16)
    bq_ref[0] = _head_norm_t(b_qk[:B_W], bqn_ref[...], scale=Q_FOLD).astype(BF16)
    gate_ref[0, A_Q:, :] = _silu(b_vg[B_W:])
    _store_v_with_ones(bv_ref, b_vg[:B_W])


def _store_v_with_ones(v_ref, v):
    tm = v.shape[1]
    ones = jnp.ones((PV_SUM_ROWS, tm), BF16)
    for h in range(v.shape[0] // HEAD_DIM):
        v_ref[0, h * BAND_V_ROWS:h * BAND_V_ROWS + HEAD_DIM, :] = (
            v[h * HEAD_DIM:(h + 1) * HEAD_DIM].astype(BF16))
        v_ref[0, h * BAND_V_ROWS + HEAD_DIM:(h + 1) * BAND_V_ROWS, :] = ones


def _even_proj(x2d, gain, w_t, cos_t, sin_t, aqn, akn, bqn, bkn, batch, seq):
    tm = PROJ_TOKENS
    nt = seq // tm
    tokens = batch * seq
    even_in = w_t.shape[0]

    def tok(b, i):
        return (b * nt + i, 0)

    def feat(b, i):
        return (b, 0, i)

    def const2(b, i):
        return (0, 0)

    def tab(b, i):
        return (0, i)

    in_specs = [
        pl.BlockSpec((tm, D_MODEL), tok),
        pl.BlockSpec((1, D_MODEL), const2),
        pl.BlockSpec((even_in, D_MODEL), const2),
        pl.BlockSpec((HALF_DIM, tm), tab),
        pl.BlockSpec((HALF_DIM, tm), tab),
        pl.BlockSpec((HEAD_DIM, tm), const2),
        pl.BlockSpec((HEAD_DIM, tm), const2),
        pl.BlockSpec((HEAD_DIM, tm), const2),
        pl.BlockSpec((HEAD_DIM, tm), const2),
    ]
    out_shape = [
        jax.ShapeDtypeStruct((batch, A_Q, seq), BF16),
        jax.ShapeDtypeStruct((tokens, A_KV), BF16),
        jax.ShapeDtypeStruct((batch, A_V_ROWS, seq), BF16),
        jax.ShapeDtypeStruct((batch, B_W, seq), BF16),
        jax.ShapeDtypeStruct((tokens, B_W), BF16),
        jax.ShapeDtypeStruct((batch, B_V_ROWS, seq), BF16),
        jax.ShapeDtypeStruct((batch, EVEN_MIX, seq), F32),
    ]
    out_specs = [
        pl.BlockSpec((1, A_Q, tm), feat),
        pl.BlockSpec((tm, A_KV), tok),
        pl.BlockSpec((1, A_V_ROWS, tm), feat),
        pl.BlockSpec((1, B_W, tm), feat),
        pl.BlockSpec((tm, B_W), tok),
        pl.BlockSpec((1, B_V_ROWS, tm), feat),
        pl.BlockSpec((1, EVEN_MIX, tm), feat),
    ]
    return pl.pallas_call(
        _even_proj_kernel,
        grid=(batch, nt),
        in_specs=in_specs,
        out_specs=out_specs,
        out_shape=out_shape,
        compiler_params=_params(("parallel", "parallel")),
        name="even_proj",
    )(x2d, gain, w_t, cos_t, sin_t, aqn, akn, bqn, bkn)


def _band_mask_np(prev_rows, n_q, q_block):
    key = np.arange(prev_rows + n_q)[:, None]
    qry = np.arange(n_q)[None, :] + prev_rows
    return ((key // CHUNK <= qry // CHUNK)
            & (key // CHUNK >= qry // CHUNK - prev_rows // CHUNK)
            & (key >= prev_rows - q_block * n_q))


def _band_mask_variants(prev_rows, n_q):
    n = -(-prev_rows // n_q) + 1
    return np.stack([_band_mask_np(prev_rows, n_q, v) for v in range(n)])


def _band_probs(running_max, k_win, q_pad, add_tile, sink):
    s = jnp.dot(k_win, q_pad, preferred_element_type=F32) + add_tile
    if running_max:
        m = jnp.max(s, axis=0, keepdims=True)
        if sink is not None:
            m = jnp.maximum(m, sink)
        s = s - m
        sink = None if sink is None else sink - m
    sink_term = None if sink is None else jnp.exp2(jnp.broadcast_to(sink, (1, s.shape[1])))
    return jnp.exp2(s).astype(BF16), sink_term


def _pad_q(q_h, half):
    z = jnp.zeros_like(q_h)
    return jnp.concatenate([q_h, z] if half == 0 else [z, q_h], axis=0)


def _even_mix_kernel(running_max, sinks_ref, aq_ref, akp_ref, akc_ref, avp_ref, avc_ref,
                     bq_ref, bkp2_ref, bkp1_ref, bkc_ref, bvp2_ref, bvp1_ref, bvc_ref,
                     gate_ref, mask_ref, bias_ref, y_ref):
    ak = jnp.concatenate([akp_ref[0], akc_ref[0]], axis=0)
    av = jnp.concatenate([avp_ref[0], avc_ref[0]], axis=1)
    bk = jnp.concatenate([bkp2_ref[0], bkp1_ref[0], bkc_ref[0]], axis=0)
    bv = jnp.concatenate([bvp2_ref[0], bvp1_ref[0], bvc_ref[0]], axis=1)

    group = A_Q_HEADS // A_KV_HEADS
    heads = []
    for h in range(A_Q_HEADS):
        kv = h // group
        heads.append((ak, _pad_q(aq_ref[0, h * HEAD_DIM:(h + 1) * HEAD_DIM, :], kv),
                      av[kv * BAND_V_ROWS:(kv + 1) * BAND_V_ROWS, :], mask_ref[0], sinks_ref[h],
                      h * HEAD_DIM))
    for h in range(B_HEADS):
        pair = h // 2
        heads.append((bk[:, pair * 128:(pair + 1) * 128],
                      _pad_q(bq_ref[0, h * HEAD_DIM:(h + 1) * HEAD_DIM, :], h % 2),
                      bv[h * BAND_V_ROWS:(h + 1) * BAND_V_ROWS, :], bias_ref[0, h], None,
                      A_Q + h * HEAD_DIM))

    pending = []

    def emit_pv():
        v_t, p, sink_term, row0 = pending.pop(0)
        o = jnp.dot(v_t, p, preferred_element_type=F32)
        l = o[HEAD_DIM:HEAD_DIM + 1]
        if sink_term is not None:
            l = l + sink_term
        rows = slice(row0, row0 + HEAD_DIM)
        y_ref[0, rows, :] = (o[:HEAD_DIM] * (1.0 / l) * gate_ref[0, rows, :]).astype(BF16)

    for k_win, q_pad, v_t, add_tile, sink, row0 in heads:
        p, sink_term = _band_probs(running_max, k_win, q_pad, add_tile, sink)
        pending.append((v_t, p, sink_term, row0))
        if len(pending) > MIX_PV_LAG:
            emit_pv()
    while pending:
        emit_pv()


def _even_mix(running_max, sinks, aq_t, ak3, av_t, bq_t, bk3, bv_t, gate_t, mask_a, bias_b):
    batch, seq = ak3.shape[0], ak3.shape[1]
    tq = MIX_TOKENS
    nq = seq // tq
    a_ratio = tq // A_PREV_ROWS

    def cur_t(b, i, *_):
        return (b, 0, i)

    def cur_n(b, i, *_):
        return (b, i, 0)

    def a_prev_n(b, i, *_):
        return (b, jnp.maximum(i * a_ratio - 1, 0), 0)

    def a_prev_t(b, i, *_):
        return (b, 0, jnp.maximum(i * a_ratio - 1, 0))

    def b_prev_n(back):
        return lambda b, i, *_: (b, jnp.maximum(i - back, 0), 0)

    def b_prev_t(back):
        return lambda b, i, *_: (b, 0, jnp.maximum(i - back, 0))

    in_specs = [
        pl.BlockSpec((1, A_Q, tq), cur_t),
        pl.BlockSpec((1, A_PREV_ROWS, A_KV), a_prev_n),
        pl.BlockSpec((1, tq, A_KV), cur_n),
        pl.BlockSpec((1, A_V_ROWS, A_PREV_ROWS), a_prev_t),
        pl.BlockSpec((1, A_V_ROWS, tq), cur_t),
        pl.BlockSpec((1, B_W, tq), cur_t),
        pl.BlockSpec((1, tq, B_W), b_prev_n(2)),
        pl.BlockSpec((1, tq, B_W), b_prev_n(1)),
        pl.BlockSpec((1, tq, B_W), cur_n),
        pl.BlockSpec((1, B_V_ROWS, tq), b_prev_t(2)),
        pl.BlockSpec((1, B_V_ROWS, tq), b_prev_t(1)),
        pl.BlockSpec((1, B_V_ROWS, tq), cur_t),
        pl.BlockSpec((1, EVEN_MIX, tq), cur_t),
        pl.BlockSpec((1,) + mask_a.shape[1:],
                     lambda b, i, *_: (jnp.minimum(i, mask_a.shape[0] - 1), 0, 0)),
        pl.BlockSpec((1,) + bias_b.shape[1:],
                     lambda b, i, *_: (jnp.minimum(i, bias_b.shape[0] - 1), 0, 0, 0)),
    ]
    grid_spec = pltpu.PrefetchScalarGridSpec(
        num_scalar_prefetch=1,
        grid=(batch, nq),
        in_specs=in_specs,
        out_specs=pl.BlockSpec((1, EVEN_MIX, tq), cur_t),
    )
    return pl.pallas_call(
        functools.partial(_even_mix_kernel, running_max),
        grid_spec=grid_spec,
        out_shape=jax.ShapeDtypeStruct((batch, EVEN_MIX, seq), BF16),
        compiler_params=_params(("parallel", "parallel")),
        name="even_mix_online" if running_max else "even_mix_direct",
    )(sinks, aq_t, ak3, ak3, av_t, av_t, bq_t, bk3, bk3, bk3, bv_t, bv_t, bv_t,
      gate_t, mask_a, bias_b)


def _out_proj_kernel(x_ref, y_ref, w_t_ref, o_ref):
    out_t = jnp.dot(w_t_ref[...], y_ref[0], preferred_element_type=F32)
    o_ref[...] = (x_ref[0] + out_t).T


def _out_proj(x_t, y_t, w_t):
    batch, width, seq = y_t.shape
    tm = PROJ_TOKENS
    nt = seq // tm
    return pl.pallas_call(
        _out_proj_kernel,
        grid=(batch, nt),
        in_specs=[
            pl.BlockSpec((1, D_MODEL, tm), lambda b, i: (b, 0, i)),
            pl.BlockSpec((1, width, tm), lambda b, i: (b, 0, i)),
            pl.BlockSpec((D_MODEL, width), lambda b, i: (0, 0)),
        ],
        out_specs=pl.BlockSpec((tm, D_MODEL), lambda b, i: (b * nt + i, 0)),
        out_shape=jax.ShapeDtypeStruct((batch * seq, D_MODEL), F32),
        compiler_params=_params(("parallel", "parallel")),
        name="out_proj",
    )(x_t, y_t, w_t)


def _odd_proj_kernel(x_ref, y_ref, w_out_t_ref, g_ref, w_t_ref, cos_ref, sin_ref, qn_ref, kn_ref,
                     x1_ref, q_ref, k_ref, v_ref, gate_ref):
    out_t = jnp.dot(w_out_t_ref[...], y_ref[0], preferred_element_type=F32)
    x = x_ref[...].T + out_t
    x1_ref[0] = x
    ms = jnp.mean(x * x, axis=0, keepdims=True)
    h_t = (x * lax.rsqrt(ms + EPS) * g_ref[...]).astype(BF16)
    cos = cos_ref[...]
    sin = sin_ref[...]

    def proj(lo, width):
        return jnp.dot(w_t_ref[lo:lo + width, :], h_t, preferred_element_type=F32)

    k = proj(C_QK, C_QK)
    q = proj(0, C_QK)
    g = proj(2 * C_QK + C_W, C_W)
    v = proj(2 * C_QK, C_W)
    k = _head_norm_t(k, kn_ref[...], cos, sin)
    k_ref[...] = k.T.astype(BF16)
    q_ref[0] = _head_norm_t(q, qn_ref[...], cos, sin, Q_FOLD).astype(BF16)
    gate_ref[0] = _silu(g)
    v_ref[0] = v.astype(BF16)


def _odd_proj(x2d, y_t, w_out_t, gain_col, w_t, cos_t, sin_t, qn, kn):
    batch, mix_w, seq = y_t.shape
    tm = PROJ_TOKENS
    nt = seq // tm
    tokens = batch * seq
    odd_in = w_t.shape[0]
    resident = pl.Buffered(1)

    def tok(b, i):
        return (b * nt + i, 0)

    def feat(b, i):
        return (b, 0, i)

    def const2(b, i):
        return (0, 0)

    def tab(b, i):
        return (0, i)

    return pl.pallas_call(
        _odd_proj_kernel,
        grid=(batch, nt),
        in_specs=[
            pl.BlockSpec((tm, D_MODEL), tok),
            pl.BlockSpec((1, mix_w, tm), feat),
            pl.BlockSpec((D_MODEL, mix_w), const2, pipeline_mode=resident),
            pl.BlockSpec((D_MODEL, 1), const2),
            pl.BlockSpec((odd_in, D_MODEL), const2, pipeline_mode=resident),
            pl.BlockSpec((HALF_DIM, tm), tab),
            pl.BlockSpec((HALF_DIM, tm), tab),
            pl.BlockSpec((HEAD_DIM, tm), const2),
            pl.BlockSpec((HEAD_DIM, tm), const2),
        ],
        out_specs=[
            pl.BlockSpec((1, D_MODEL, tm), feat),
            pl.BlockSpec((1, C_QK, tm), feat),
            pl.BlockSpec((tm, C_QK), tok),
            pl.BlockSpec((1, C_W, tm), feat),
            pl.BlockSpec((1, C_W, tm), feat),
        ],
        out_shape=[
            jax.ShapeDtypeStruct((batch, D_MODEL, seq), F32),
            jax.ShapeDtypeStruct((batch, C_QK, seq), BF16),
            jax.ShapeDtypeStruct((tokens, C_QK), BF16),
            jax.ShapeDtypeStruct((batch, C_W, seq), BF16),
            jax.ShapeDtypeStruct((batch, C_W, seq), F32),
        ],
        compiler_params=_params(("parallel", "parallel")),
        name="odd_proj",
    )(x2d, y_t, w_out_t, gain_col, w_t, cos_t, sin_t, qn, kn)


def _diff_attn_kernel(lambda_init, running_max, q_ref, k_ref, v_ref, gate_ref, subln_ref,
                      lq1_ref, lk1_ref, lq2_ref, lk2_ref, y_ref, acc1_ref, acc2_ref):
    tq, tk = DIFF_Q, DIFF_K
    i = pl.program_id(2)
    q = q_ref[0]
    row = lax.broadcasted_iota(jnp.int32, q.shape, 0)
    zero = jnp.zeros_like(q)
    q_pads = (jnp.where(row < HEAD_DIM, q, zero), jnp.where(row >= HEAD_DIM, q, zero))
    acc_refs = (acc1_ref, acc2_ref)

    def load_kv(start, size):
        start = pl.multiple_of(start, size)
        return k_ref[0, pl.ds(start, size), :], v_ref[0, :, pl.ds(start, size)]

    def diag_valid(key0, size, n_cols):
        key_chunk = (lax.broadcasted_iota(jnp.int32, (size, n_cols), 0) + key0) // CHUNK
        qry_chunk = (lax.broadcasted_iota(jnp.int32, (size, n_cols), 1) + (tq - n_cols)) // CHUNK
        return key_chunk <= qry_chunk

    def online_step(k, v_t, carry, valid):
        new = []
        for c in range(2):
            m, l = carry[2 * c], carry[2 * c + 1]
            s = jnp.dot(k, q_pads[c], preferred_element_type=F32)
            if valid is not None:
                s = jnp.where(valid, s, NEG_INF)
            m_new = jnp.maximum(m, jnp.max(s, axis=0, keepdims=True))
            alpha = jnp.exp2(m - m_new)
            p = jnp.exp2(s - m_new)
            l_new = alpha * l + jnp.sum(p, axis=0, keepdims=True)
            pv = jnp.dot(v_t, p.astype(BF16), preferred_element_type=F32)
            acc_refs[c][...] = alpha * acc_refs[c][...] + pv
            new += [m_new, l_new]
        return tuple(new)

    def direct_probs(k, q_c, valid):
        p = jnp.exp2(jnp.dot(k, q_c, preferred_element_type=F32))
        if valid is not None:
            p = jnp.where(valid, p, 0.0)
        return p.astype(BF16), jnp.sum(p, axis=0, keepdims=True)

    def direct_blocks(blocks, l_sums):
        l_sums = list(l_sums)
        pv_sum = {}
        pending = []

        edges = sorted({col0 for _, _, col0, _ in blocks}) + [tq]

        def emit_pv():
            v_t, c, col0, p = pending.pop(0)
            pv = jnp.dot(v_t, p, preferred_element_type=F32)
            for lo, hi in zip(edges[:-1], edges[1:]):
                if lo >= col0:
                    part = pv[:, lo - col0:hi - col0]
                    pv_sum[(c, lo, hi)] = pv_sum[(c, lo, hi)] + part if (c, lo, hi) in pv_sum else part

        for k, v_t, col0, valid in blocks:
            for c in range(2):
                p, total = direct_probs(k, q_pads[c][:, col0:], valid)
                if col0:
                    total = jnp.concatenate([jnp.zeros((1, col0), F32), total], axis=1)
                l_sums[c] = l_sums[c] + total
                pending.append((v_t, c, col0, p))
                if len(pending) > 1:
                    emit_pv()
        while pending:
            emit_pv()
        for (c, lo, hi), pv in pv_sum.items():
            acc_refs[c][:, lo:hi] += pv
        return tuple(l_sums)

    q0 = i * tq
    acc1_ref[...] = jnp.zeros_like(acc1_ref)
    acc2_ref[...] = jnp.zeros_like(acc2_ref)
    if running_max:
        init = (jnp.full((1, tq), NEG_INF, F32), jnp.zeros((1, tq), F32)) * 2
        carry = lax.fori_loop(0, q0 // tk,
                              lambda j, c: online_step(*load_kv(j * tk, tk), c, None), init)
        for key0 in range(0, tq, tk):
            carry = online_step(*load_kv(q0 + key0, tk), carry, diag_valid(key0, tk, tq))
        l1, l2 = carry[1], carry[3]
    else:
        def group(t, l_sums):
            return direct_blocks(
                [load_kv(t * tq + u * tk, tk) + (0, None) for u in range(tq // tk)], l_sums)

        l_sums = lax.fori_loop(0, i, group, (jnp.zeros((1, tq), F32),) * 2)
        td = DIFF_DIAG_K
        l1, l2 = direct_blocks(
            [load_kv(q0 + key0, td) + (key0, diag_valid(key0, td, tq - key0))
             for key0 in range(0, tq, td)], l_sums)

    lam = (jnp.exp(jnp.sum(lq1_ref[...] * lk1_ref[...], axis=-1, keepdims=True))
           - jnp.exp(jnp.sum(lq2_ref[...] * lk2_ref[...], axis=-1, keepdims=True))
           + lambda_init)
    o = acc1_ref[...] * (1.0 / l1) - lam * (acc2_ref[...] * (1.0 / l2))
    ms = jnp.mean(o * o, axis=0, keepdims=True)
    o = o * lax.rsqrt(ms + EPS) * subln_ref[...] * (1.0 - lambda_init)
    y_ref[0] = (o * gate_ref[0]).astype(BF16)


def _diff_attn(running_max, q_t, k3, v_t, gate_t, subln_tile, lq1, lk1, lq2, lk2, lambda_init):
    batch, seq = k3.shape[0], k3.shape[1]
    tq = DIFF_Q
    nq = seq // tq
    head_w = 2 * HEAD_DIM

    def blk(b, h, i):
        return (b, h, i)

    def const2(b, h, i):
        return (0, 0)

    return pl.pallas_call(
        functools.partial(_diff_attn_kernel, lambda_init, running_max),
        grid=(batch, C_HEADS, nq),
        in_specs=[
            pl.BlockSpec((1, head_w, tq), blk),
            pl.BlockSpec((1, seq, head_w), lambda b, h, i: (b, 0, h)),
            pl.BlockSpec((1, C_V_DIM, seq), lambda b, h, i: (b, h, 0)),
            pl.BlockSpec((1, C_V_DIM, tq), blk),
            pl.BlockSpec((C_V_DIM, tq), const2),
            pl.BlockSpec((1, HEAD_DIM), const2),
            pl.BlockSpec((1, HEAD_DIM), const2),
            pl.BlockSpec((1, HEAD_DIM), const2),
            pl.BlockSpec((1, HEAD_DIM), const2),
        ],
        out_specs=pl.BlockSpec((1, C_V_DIM, tq), blk),
        out_shape=jax.ShapeDtypeStruct((batch, C_W, seq), BF16),
        scratch_shapes=[pltpu.VMEM((C_V_DIM, tq), F32), pltpu.VMEM((C_V_DIM, tq), F32)],
        compiler_params=_params(("parallel", "parallel", "arbitrary")),
        name="diff_attn_online" if running_max else "diff_attn_direct",
    )(q_t, k3, v_t, gate_t, subln_tile, lq1, lk1, lq2, lk2)


def _qk_score_bound(q_gain, k_gain):
    bound = HEAD_DIM * QK_SCALE * jnp.max(jnp.abs(q_gain)) * jnp.max(jnp.abs(k_gain))
    return bound.astype(F32) * (1.0 + 2.0 ** -6)


def _rope_tables_t(seq):
    inv = 1.0 / (ROPE_THETA ** (jnp.arange(0, HEAD_DIM, 2, dtype=F32) / HEAD_DIM))
    ang = inv[:, None] * jnp.arange(seq, dtype=F32)[None, :]
    return jnp.cos(ang), jnp.sin(ang)


def _gain_tile(g, width):
    return jnp.broadcast_to(g.astype(F32)[:, None], (g.shape[0], width))


def _rel_bias_t(rel_table, n_q):
    n_k = B_PREV_ROWS + n_q
    width = n_k + n_q
    offset = np.arange(width) - (n_k - 1)
    rel = np.clip(B_PREV_ROWS + offset, -B_MAX_REL, B_MAX_REL) + B_MAX_REL
    by_offset = rel_table.astype(F32)[:, rel]
    heads = rel_table.shape[0]
    sub = 8

    def build(f_ref, o_ref):
        for h in range(heads):
            f = f_ref[h:h + 1, :]
            rows = jnp.concatenate(
                [pltpu.roll(f, (width + b - (sub - 1)) % width, axis=1) for b in range(sub)], axis=0)
            for blk in range(n_k // sub):
                lead = n_k - sub - sub * blk
                o_ref[h, sub * blk:sub * (blk + 1), :] = pltpu.roll(
                    rows, (width - lead) % width, axis=1)[:, :n_q]

    return pl.pallas_call(
        build,
        out_shape=jax.ShapeDtypeStruct((heads, n_k, n_q), F32),
        name="rel_bias_tile",
    )(by_offset)


def kernel(x, ev_norm, ev_w_in, ev_w_out, ev_a_q_norm, ev_a_k_norm, ev_a_sinks,
           ev_b_q_norm, ev_b_k_norm, ev_b_rel_bias, od_norm, od_w_in, od_w_out,
           od_q_norm, od_k_norm, od_lambda_q1, od_lambda_k1, od_lambda_q2,
           od_lambda_k2, od_subln):
    batch, seq, d = x.shape
    cos_t, sin_t = _rope_tables_t(seq)
    depth = ev_norm.shape[0] + od_norm.shape[0]
    assert ev_norm.shape[0] == od_norm.shape[0]
    x2d = x.reshape(batch * seq, d)
    for layer in range(depth):
        i = layer // 2
        if layer % 2 == 0:
            w_in_t = ev_w_in[i].T.astype(BF16)
            w_out_t = ev_w_out[i].T.astype(BF16)
            aq_t, ak, av_t, bq_t, bk, bv_t, gate_t = _even_proj(
                x2d, ev_norm[i][None, :], w_in_t, cos_t, sin_t,
                _gain_tile(ev_a_q_norm[i], PROJ_TOKENS), _gain_tile(ev_a_k_norm[i], PROJ_TOKENS),
                _gain_tile(ev_b_q_norm[i], PROJ_TOKENS), _gain_tile(ev_b_k_norm[i], PROJ_TOKENS),
                batch, seq)
            mask_a = jnp.where(_band_mask_variants(A_PREV_ROWS, MIX_TOKENS), 0.0, NEG_INF).astype(F32)
            bias_b = jnp.where(_band_mask_variants(B_PREV_ROWS, MIX_TOKENS)[:, None],
                               _rel_bias_t(ev_b_rel_bias[i] * LOG2E, MIX_TOKENS)[None], NEG_INF)
            sinks = ev_a_sinks[i].astype(F32)
            mix_args = (sinks * LOG2E, aq_t, ak.reshape(batch, seq, A_KV), av_t,
                        bq_t, bk.reshape(batch, seq, B_W), bv_t, gate_t, mask_a, bias_b)
            bound_a = jnp.maximum(_qk_score_bound(ev_a_q_norm[i], ev_a_k_norm[i]),
                                  jnp.max(jnp.abs(sinks)))
            bound_b = (_qk_score_bound(ev_b_q_norm[i], ev_b_k_norm[i])
                       + jnp.max(jnp.abs(ev_b_rel_bias[i])))
            y_t = lax.cond(jnp.maximum(bound_a, bound_b) <= SAFE_SCORE,
                           functools.partial(_even_mix, False),
                           functools.partial(_even_mix, True), *mix_args)
            even_w_out_t = w_out_t
        else:
            lambda_init = 0.8 - 0.6 * math.exp(-0.3 * layer)
            w_in_t = od_w_in[i].T.astype(BF16)
            w_out_t = od_w_out[i].T.astype(BF16)
            x_t, q_t, k, v_t, gate_t = _odd_proj(
                x2d, y_t, even_w_out_t, od_norm[i][:, None], w_in_t, cos_t, sin_t,
                _gain_tile(od_q_norm[i], PROJ_TOKENS), _gain_tile(od_k_norm[i], PROJ_TOKENS))
            diff_args = (q_t, k.reshape(batch, seq, C_QK), v_t, gate_t,
                         _gain_tile(od_subln[i], DIFF_Q),
                         od_lambda_q1[i][None, :], od_lambda_k1[i][None, :],
                         od_lambda_q2[i][None, :], od_lambda_k2[i][None, :])
            y_t = lax.cond(
                _qk_score_bound(od_q_norm[i], od_k_norm[i]) <= SAFE_SCORE,
                lambda *a: _diff_attn(False, *a, lambda_init),
                lambda *a: _diff_attn(True, *a, lambda_init),
                *diff_args)
            x2d = _out_proj(x_t, y_t, w_out_t)
    return x2d.reshape(batch, seq, d)
```

```python
import functools
import math

import jax
import jax.numpy as jnp
import numpy as np
from jax import lax
from jax.experimental import pallas as pl
from jax.experimental.pallas import tpu as pltpu

F32 = jnp.float32
BF16 = jnp.bfloat16

D_MODEL = 1024
CHUNK = 64
HEAD_DIM = 64
HALF_DIM = HEAD_DIM // 2
ROPE_THETA = 10000.0
EPS = 1e-6
NEG_INF = -1e30
QK_SCALE = 1.0 / math.sqrt(HEAD_DIM)
LOG2E = math.log2(math.e)
Q_FOLD = QK_SCALE * LOG2E

A_Q_HEADS = 8
A_KV_HEADS = 2
A_PREV_CHUNKS = 2
B_HEADS = 8
B_PREV_CHUNKS = 8
B_MAX_REL = 128
C_HEADS = 8
C_V_DIM = 2 * HEAD_DIM

A_Q = A_Q_HEADS * HEAD_DIM
A_KV = A_KV_HEADS * HEAD_DIM
B_W = B_HEADS * HEAD_DIM
EVEN_MIX = A_Q + B_W
C_QK = C_HEADS * 2 * HEAD_DIM
C_W = C_HEADS * C_V_DIM
PV_SUM_ROWS = 16
BAND_V_ROWS = HEAD_DIM + PV_SUM_ROWS
A_V_ROWS = A_KV_HEADS * BAND_V_ROWS
B_V_ROWS = B_HEADS * BAND_V_ROWS

V7X_VMEM_LIMIT_BYTES = 56 * 1024 * 1024

EVEN_PROJ_TOKENS = 1024
ODD_PROJ_TOKENS = 512
OUT_PROJ_TOKENS = 1024
MIX_TOKENS = 256
MIX_PV_LAG = 3
A_PREV_ROWS = A_PREV_CHUNKS * CHUNK
B_PREV_ROWS = B_PREV_CHUNKS * CHUNK
DIFF_Q = 2048
DIFF_K = 512
DIFF_DIAG_K = 256
SAFE_SCORE = 40.0


def _params(semantics):
    return pltpu.CompilerParams(dimension_semantics=semantics,
                                vmem_limit_bytes=V7X_VMEM_LIMIT_BYTES)


def _rms_rows(x, gain_row):
    ms = jnp.mean(x * x, axis=-1, keepdims=True)
    return x * lax.rsqrt(ms + EPS) * gain_row


def _proj_t(w_t_ref, lo, width, h):
    return lax.dot_general(w_t_ref[lo:lo + width, :], h,
                           (((1,), (1,)), ((), ())),
                           preferred_element_type=F32)


def _head_norm_t(p, gain_tile, cos=None, sin=None, scale=1.0):
    width, tm = p.shape
    y = p.reshape(width // HEAD_DIM, HEAD_DIM, tm)
    ms = jnp.mean(y * y, axis=1, keepdims=True)
    y = y * lax.rsqrt(ms + EPS) * gain_tile[None]
    if cos is not None:
        y1 = y[:, :HALF_DIM]
        y2 = y[:, HALF_DIM:]
        c = cos[None]
        s = sin[None]
        y = jnp.concatenate([y1 * c - y2 * s, y2 * c + y1 * s], axis=1)
    if scale != 1.0:
        y = y * scale
    return y.reshape(width, tm)


def _silu(x):
    return x * (1.0 / (1.0 + jnp.exp(-x)))


def _even_proj_kernel(x_ref, g_ref, w_t_ref, cos_ref, sin_ref,
                      aqn_ref, akn_ref, bqn_ref, bkn_ref,
                      aq_ref, ak_ref, av_ref, bq_ref, bk_ref, bv_ref, gate_ref):
    h = _rms_rows(x_ref[...], g_ref[...]).astype(BF16)
    cos = cos_ref[...]
    sin = sin_ref[...]
    a_w = 2 * A_Q + 2 * A_KV
    a_all = _proj_t(w_t_ref, 0, a_w, h)
    b_qk = _proj_t(w_t_ref, a_w, 2 * B_W, h)
    b_vg = _proj_t(w_t_ref, a_w + 2 * B_W, 2 * B_W, h)
    ak = _head_norm_t(a_all[A_Q:A_Q + A_KV], akn_ref[...], cos, sin)
    ak_ref[...] = ak.T.astype(BF16)
    aq_ref[0] = _head_norm_t(a_all[:A_Q], aqn_ref[...], cos, sin, Q_FOLD).astype(BF16)
    _store_v_with_ones(av_ref, a_all[A_Q + A_KV:A_Q + 2 * A_KV])
    gate_ref[0, :A_Q, :] = _silu(a_all[A_Q + 2 * A_KV:])
    bk = _head_norm_t(b_qk[B_W:], bkn_ref[...])
    bk_ref[...] = bk.T.astype(BF16)
    bq_ref[0] = _head_norm_t(b_qk[:B_W], bqn_ref[...], scale=Q_FOLD).astype(BF16)
    gate_ref[0, A_Q:, :] = _silu(b_vg[B_W:])
    _store_v_with_ones(bv_ref, b_vg[:B_W])


def _store_v_with_ones(v_ref, v):
    tm = v.shape[1]
    ones = jnp.ones((PV_SUM_ROWS, tm), BF16)
    for h in range(v.shape[0] // HEAD_DIM):
        v_ref[0, h * BAND_V_ROWS:h * BAND_V_ROWS + HEAD_DIM, :] = (
            v[h * HEAD_DIM:(h + 1) * HEAD_DIM].astype(BF16))
        v_ref[0, h * BAND_V_ROWS + HEAD_DIM:(h + 1) * BAND_V_ROWS, :] = ones


def _even_proj(x2d, gain, w_t, cos_t, sin_t, aqn, akn, bqn, bkn, batch, seq):
    tm = EVEN_PROJ_TOKENS
    nt = seq // tm
    tokens = batch * seq
    even_in = w_t.shape[0]

    def tok(b, i):
        return (b * nt + i, 0)

    def feat(b, i):
        return (b, 0, i)

    def const2(b, i):
        return (0, 0)

    def tab(b, i):
        return (0, i)

    in_specs = [
        pl.BlockSpec((tm, D_MODEL), tok),
        pl.BlockSpec((1, D_MODEL), const2),
        pl.BlockSpec((even_in, D_MODEL), const2, pipeline_mode=pl.Buffered(1)),
        pl.BlockSpec((HALF_DIM, tm), tab),
        pl.BlockSpec((HALF_DIM, tm), tab),
        pl.BlockSpec((HEAD_DIM, tm), const2),
        pl.BlockSpec((HEAD_DIM, tm), const2),
        pl.BlockSpec((HEAD_DIM, tm), const2),
        pl.BlockSpec((HEAD_DIM, tm), const2),
    ]
    out_shape = [
        jax.ShapeDtypeStruct((batch, A_Q, seq), BF16),
        jax.ShapeDtypeStruct((tokens, A_KV), BF16),
        jax.ShapeDtypeStruct((batch, A_V_ROWS, seq), BF16),
        jax.ShapeDtypeStruct((batch, B_W, seq), BF16),
        jax.ShapeDtypeStruct((tokens, B_W), BF16),
        jax.ShapeDtypeStruct((batch, B_V_ROWS, seq), BF16),
        jax.ShapeDtypeStruct((batch, EVEN_MIX, seq), F32),
    ]
    out_specs = [
        pl.BlockSpec((1, A_Q, tm), feat),
        pl.BlockSpec((tm, A_KV), tok),
        pl.BlockSpec((1, A_V_ROWS, tm), feat),
        pl.BlockSpec((1, B_W, tm), feat),
        pl.BlockSpec((tm, B_W), tok),
        pl.BlockSpec((1, B_V_ROWS, tm), feat),
        pl.BlockSpec((1, EVEN_MIX, tm), feat),
    ]
    return pl.pallas_call(
        _even_proj_kernel,
        grid=(batch, nt),
        in_specs=in_specs,
        out_specs=out_specs,
        out_shape=out_shape,
        compiler_params=_params(("parallel", "parallel")),
        name="even_proj",
    )(x2d, gain, w_t, cos_t, sin_t, aqn, akn, bqn, bkn)


def _band_mask_np(prev_rows, n_q, q_block):
    key = np.arange(prev_rows + n_q)[:, None]
    qry = np.arange(n_q)[None, :] + prev_rows
    return ((key // CHUNK <= qry // CHUNK)
            & (key // CHUNK >= qry // CHUNK - prev_rows // CHUNK)
            & (key >= prev_rows - q_block * n_q))


def _band_mask_variants(prev_rows, n_q):
    n = -(-prev_rows // n_q) + 1
    return np.stack([_band_mask_np(prev_rows, n_q, v) for v in range(n)])


def _band_probs(running_max, k_win, q_pad, add_tile, sink):
    s = jnp.dot(k_win, q_pad, preferred_element_type=F32) + add_tile
    if running_max:
        m = jnp.max(s, axis=0, keepdims=True)
        if sink is not None:
            m = jnp.maximum(m, sink)
        s = s - m
        sink = None if sink is None else sink - m
    sink_term = None if sink is None else jnp.exp2(jnp.broadcast_to(sink, (1, s.shape[1])))
    return jnp.exp2(s).astype(BF16), sink_term


def _pad_q(q_h, half):
    z = jnp.zeros_like(q_h)
    return jnp.concatenate([q_h, z] if half == 0 else [z, q_h], axis=0)


def _even_mix_kernel(running_max, sinks_ref, aq_ref, akp_ref, akc_ref, avp_ref, avc_ref,
                     bq_ref, bkp2_ref, bkp1_ref, bkc_ref, bvp2_ref, bvp1_ref, bvc_ref,
                     gate_ref, mask_ref, bias_ref, y_ref):
    ak = jnp.concatenate([akp_ref[0], akc_ref[0]], axis=0)
    av = jnp.concatenate([avp_ref[0], avc_ref[0]], axis=1)
    bk = jnp.concatenate([bkp2_ref[0], bkp1_ref[0], bkc_ref[0]], axis=0)
    bv = jnp.concatenate([bvp2_ref[0], bvp1_ref[0], bvc_ref[0]], axis=1)

    group = A_Q_HEADS // A_KV_HEADS
    heads = []
    for h in range(A_Q_HEADS):
        kv = h // group
        heads.append((ak, _pad_q(aq_ref[0, h * HEAD_DIM:(h + 1) * HEAD_DIM, :], kv),
                      av[kv * BAND_V_ROWS:(kv + 1) * BAND_V_ROWS, :], mask_ref[0], sinks_ref[h],
                      h * HEAD_DIM))
    for h in range(B_HEADS):
        pair = h // 2
        heads.append((bk[:, pair * 128:(pair + 1) * 128],
                      _pad_q(bq_ref[0, h * HEAD_DIM:(h + 1) * HEAD_DIM, :], h % 2),
                      bv[h * BAND_V_ROWS:(h + 1) * BAND_V_ROWS, :], bias_ref[0, h], None,
                      A_Q + h * HEAD_DIM))

    pending = []

    def emit_pv():
        v_t, p, sink_term, row0 = pending.pop(0)
        o = jnp.dot(v_t, p, preferred_element_type=F32)
        l = o[HEAD_DIM:HEAD_DIM + 1]
        if sink_term is not None:
            l = l + sink_term
        rows = slice(row0, row0 + HEAD_DIM)
        y_ref[0, rows, :] = (o[:HEAD_DIM] * (1.0 / l) * gate_ref[0, rows, :]).astype(BF16)

    for k_win, q_pad, v_t, add_tile, sink, row0 in heads:
        p, sink_term = _band_probs(running_max, k_win, q_pad, add_tile, sink)
        pending.append((v_t, p, sink_term, row0))
        if len(pending) > MIX_PV_LAG:
            emit_pv()
    while pending:
        emit_pv()


def _even_mix(running_max, sinks, aq_t, ak3, av_t, bq_t, bk3, bv_t, gate_t, mask_a, bias_b):
    batch, seq = ak3.shape[0], ak3.shape[1]
    tq = MIX_TOKENS
    nq = seq // tq
    a_ratio = tq // A_PREV_ROWS

    def cur_t(b, i, *_):
        return (b, 0, i)

    def cur_n(b, i, *_):
        return (b, i, 0)

    def a_prev_n(b, i, *_):
        return (b, jnp.maximum(i * a_ratio - 1, 0), 0)

    def a_prev_t(b, i, *_):
        return (b, 0, jnp.maximum(i * a_ratio - 1, 0))

    def b_prev_n(back):
        return lambda b, i, *_: (b, jnp.maximum(i - back, 0), 0)

    def b_prev_t(back):
        return lambda b, i, *_: (b, 0, jnp.maximum(i - back, 0))

    in_specs = [
        pl.BlockSpec((1, A_Q, tq), cur_t),
        pl.BlockSpec((1, A_PREV_ROWS, A_KV), a_prev_n),
        pl.BlockSpec((1, tq, A_KV), cur_n),
        pl.BlockSpec((1, A_V_ROWS, A_PREV_ROWS), a_prev_t),
        pl.BlockSpec((1, A_V_ROWS, tq), cur_t),
        pl.BlockSpec((1, B_W, tq), cur_t),
        pl.BlockSpec((1, tq, B_W), b_prev_n(2)),
        pl.BlockSpec((1, tq, B_W), b_prev_n(1)),
        pl.BlockSpec((1, tq, B_W), cur_n),
        pl.BlockSpec((1, B_V_ROWS, tq), b_prev_t(2)),
        pl.BlockSpec((1, B_V_ROWS, tq), b_prev_t(1)),
        pl.BlockSpec((1, B_V_ROWS, tq), cur_t),
        pl.BlockSpec((1, EVEN_MIX, tq), cur_t),
        pl.BlockSpec((1,) + mask_a.shape[1:],
                     lambda b, i, *_: (jnp.minimum(i, mask_a.shape[0] - 1), 0, 0)),
        pl.BlockSpec((1,) + bias_b.shape[1:],
                     lambda b, i, *_: (jnp.minimum(i, bias_b.shape[0] - 1), 0, 0, 0)),
    ]
    grid_spec = pltpu.PrefetchScalarGridSpec(
        num_scalar_prefetch=1,
        grid=(batch, nq),
        in_specs=in_specs,
        out_specs=pl.BlockSpec((1, EVEN_MIX, tq), cur_t),
    )
    return pl.pallas_call(
        functools.partial(_even_mix_kernel, running_max),
        grid_spec=grid_spec,
        out_shape=jax.ShapeDtypeStruct((batch, EVEN_MIX, seq), BF16),
        compiler_params=_params(("parallel", "parallel")),
        name="even_mix_online" if running_max else "even_mix_direct",
    )(sinks, aq_t, ak3, ak3, av_t, av_t, bq_t, bk3, bk3, bk3, bv_t, bv_t, bv_t,
      gate_t, mask_a, bias_b)


def _out_proj_kernel(x_ref, y_ref, w_t_ref, o_ref):
    out_t = jnp.dot(w_t_ref[...], y_ref[0], preferred_element_type=F32)
    o_ref[...] = (x_ref[0] + out_t).T


def _out_proj(x_t, y_t, w_t):
    batch, width, seq = y_t.shape
    tm = OUT_PROJ_TOKENS
    nt = seq // tm
    return pl.pallas_call(
        _out_proj_kernel,
        grid=(batch, nt),
        in_specs=[
            pl.BlockSpec((1, D_MODEL, tm), lambda b, i: (b, 0, i)),
            pl.BlockSpec((1, width, tm), lambda b, i: (b, 0, i)),
            pl.BlockSpec((D_MODEL, width), lambda b, i: (0, 0)),
        ],
        out_specs=pl.BlockSpec((tm, D_MODEL), lambda b, i: (b * nt + i, 0)),
        out_shape=jax.ShapeDtypeStruct((batch * seq, D_MODEL), F32),
        compiler_params=_params(("parallel", "parallel")),
        name="out_proj",
    )(x_t, y_t, w_t)


def _odd_proj_kernel(x_ref, y_ref, w_out_t_ref, g_ref, w_t_ref, cos_ref, sin_ref, qn_ref, kn_ref,
                     x1_ref, q_ref, k_ref, v_ref, gate_ref):
    out_t = jnp.dot(w_out_t_ref[...], y_ref[0], preferred_element_type=F32)
    x = x_ref[...].T + out_t
    x1_ref[0] = x
    ms = jnp.mean(x * x, axis=0, keepdims=True)
    h_t = (x * lax.rsqrt(ms + EPS) * g_ref[...]).astype(BF16)
    cos = cos_ref[...]
    sin = sin_ref[...]

    def proj(lo, width):
        return jnp.dot(w_t_ref[lo:lo + width, :], h_t, preferred_element_type=F32)

    k = proj(C_QK, C_QK)
    q = proj(0, C_QK)
    g = proj(2 * C_QK + C_W, C_W)
    v = proj(2 * C_QK, C_W)
    k = _head_norm_t(k, kn_ref[...], cos, sin)
    k_ref[...] = k.T.astype(BF16)
    q_ref[0] = _head_norm_t(q, qn_ref[...], cos, sin, Q_FOLD).astype(BF16)
    gate_ref[0] = _silu(g)
    v_ref[0] = v.astype(BF16)


def _odd_proj(x2d, y_t, w_out_t, gain_col, w_t, cos_t, sin_t, qn, kn):
    batch, mix_w, seq = y_t.shape
    tm = ODD_PROJ_TOKENS
    nt = seq // tm
    tokens = batch * seq
    odd_in = w_t.shape[0]
    resident = pl.Buffered(1)

    def tok(b, i):
        return (b * nt + i, 0)

    def feat(b, i):
        return (b, 0, i)

    def const2(b, i):
        return (0, 0)

    def tab(b, i):
        return (0, i)

    return pl.pallas_call(
        _odd_proj_kernel,
        grid=(batch, nt),
        in_specs=[
            pl.BlockSpec((tm, D_MODEL), tok),
            pl.BlockSpec((1, mix_w, tm), feat),
            pl.BlockSpec((D_MODEL, mix_w), const2, pipeline_mode=resident),
            pl.BlockSpec((D_MODEL, 1), const2),
            pl.BlockSpec((odd_in, D_MODEL), const2, pipeline_mode=resident),
            pl.BlockSpec((HALF_DIM, tm), tab),
            pl.BlockSpec((HALF_DIM, tm), tab),
            pl.BlockSpec((HEAD_DIM, tm), const2),
            pl.BlockSpec((HEAD_DIM, tm), const2),
        ],
        out_specs=[
            pl.BlockSpec((1, D_MODEL, tm), feat),
            pl.BlockSpec((1, C_QK, tm), feat),
            pl.BlockSpec((tm, C_QK), tok),
            pl.BlockSpec((1, C_W, tm), feat),
            pl.BlockSpec((1, C_W, tm), feat),
        ],
        out_shape=[
            jax.ShapeDtypeStruct((batch, D_MODEL, seq), F32),
            jax.ShapeDtypeStruct((batch, C_QK, seq), BF16),
            jax.ShapeDtypeStruct((tokens, C_QK), BF16),
            jax.ShapeDtypeStruct((batch, C_W, seq), BF16),
            jax.ShapeDtypeStruct((batch, C_W, seq), F32),
        ],
        compiler_params=_params(("parallel", "parallel")),
        name="odd_proj",
    )(x2d, y_t, w_out_t, gain_col, w_t, cos_t, sin_t, qn, kn)


def _diff_attn_kernel(lambda_init, running_max, q_ref, k_ref, v_ref, gate_ref, subln_ref,
                      lq1_ref, lk1_ref, lq2_ref, lk2_ref, y_ref, acc1_ref, acc2_ref):
    tq, tk = DIFF_Q, DIFF_K
    i = pl.program_id(2)
    q = q_ref[0]
    row = lax.broadcasted_iota(jnp.int32, q.shape, 0)
    zero = jnp.zeros_like(q)
    q_pads = (jnp.where(row < HEAD_DIM, q, zero), jnp.where(row >= HEAD_DIM, q, zero))
    acc_refs = (acc1_ref, acc2_ref)

    def load_kv(start, size):
        start = pl.multiple_of(start, size)
        return k_ref[0, pl.ds(start, size), :], v_ref[0, :, pl.ds(start, size)]

    def diag_valid(key0, size, n_cols):
        key_chunk = (lax.broadcasted_iota(jnp.int32, (size, n_cols), 0) + key0) // CHUNK
        qry_chunk = (lax.broadcasted_iota(jnp.int32, (size, n_cols), 1) + (tq - n_cols)) // CHUNK
        return key_chunk <= qry_chunk

    def online_step(k, v_t, carry, valid):
        new = []
        for c in range(2):
            m, l = carry[2 * c], carry[2 * c + 1]
            s = jnp.dot(k, q_pads[c], preferred_element_type=F32)
            if valid is not None:
                s = jnp.where(valid, s, NEG_INF)
            m_new = jnp.maximum(m, jnp.max(s, axis=0, keepdims=True))
            alpha = jnp.exp2(m - m_new)
            p = jnp.exp2(s - m_new)
            l_new = alpha * l + jnp.sum(p, axis=0, keepdims=True)
            pv = jnp.dot(v_t, p.astype(BF16), preferred_element_type=F32)
            acc_refs[c][...] = alpha * acc_refs[c][...] + pv
            new += [m_new, l_new]
        return tuple(new)

    def direct_probs(k, q_c, valid):
        p = jnp.exp2(jnp.dot(k, q_c, preferred_element_type=F32))
        if valid is not None:
            p = jnp.where(valid, p, 0.0)
        return p.astype(BF16), jnp.sum(p, axis=0, keepdims=True)

    def direct_blocks(blocks, l_sums):
        l_sums = list(l_sums)
        pv_sum = {}
        pending = []

        edges = sorted({col0 for _, _, col0, _ in blocks}) + [tq]

        def emit_pv():
            v_t, c, col0, p = pending.pop(0)
            pv = jnp.dot(v_t, p, preferred_element_type=F32)
            for lo, hi in zip(edges[:-1], edges[1:]):
                if lo >= col0:
                    part = pv[:, lo - col0:hi - col0]
                    pv_sum[(c, lo, hi)] = pv_sum[(c, lo, hi)] + part if (c, lo, hi) in pv_sum else part

        for k, v_t, col0, valid in blocks:
            for c in range(2):
                p, total = direct_probs(k, q_pads[c][:, col0:], valid)
                if col0:
                    total = jnp.concatenate([jnp.zeros((1, col0), F32), total], axis=1)
                l_sums[c] = l_sums[c] + total
                pending.append((v_t, c, col0, p))
                if len(pending) > 1:
                    emit_pv()
        while pending:
            emit_pv()
        for (c, lo, hi), pv in pv_sum.items():
            acc_refs[c][:, lo:hi] += pv
        return tuple(l_sums)

    q0 = i * tq
    acc1_ref[...] = jnp.zeros_like(acc1_ref)
    acc2_ref[...] = jnp.zeros_like(acc2_ref)
    if running_max:
        init = (jnp.full((1, tq), NEG_INF, F32), jnp.zeros((1, tq), F32)) * 2
        carry = lax.fori_loop(0, q0 // tk,
                              lambda j, c: online_step(*load_kv(j * tk, tk), c, None), init)
        for key0 in range(0, tq, tk):
            carry = online_step(*load_kv(q0 + key0, tk), carry, diag_valid(key0, tk, tq))
        l1, l2 = carry[1], carry[3]
    else:
        def group(t, l_sums):
            return direct_blocks(
                [load_kv(t * tq + u * tk, tk) + (0, None) for u in range(tq // tk)], l_sums)

        l_sums = lax.fori_loop(0, i, group, (jnp.zeros((1, tq), F32),) * 2)
        td = DIFF_DIAG_K
        l1, l2 = direct_blocks(
            [load_kv(q0 + key0, td) + (key0, diag_valid(key0, td, tq - key0))
             for key0 in range(0, tq, td)], l_sums)

    lam = (jnp.exp(jnp.sum(lq1_ref[...] * lk1_ref[...], axis=-1, keepdims=True))
           - jnp.exp(jnp.sum(lq2_ref[...] * lk2_ref[...], axis=-1, keepdims=True))
           + lambda_init)
    o = acc1_ref[...] * (1.0 / l1) - lam * (acc2_ref[...] * (1.0 / l2))
    ms = jnp.mean(o * o, axis=0, keepdims=True)
    o = o * lax.rsqrt(ms + EPS) * subln_ref[...] * (1.0 - lambda_init)
    y_ref[0] = (o * gate_ref[0]).astype(BF16)


def _diff_attn(running_max, q_t, k3, v_t, gate_t, subln_tile, lq1, lk1, lq2, lk2, lambda_init):
    batch, seq = k3.shape[0], k3.shape[1]
    tq = DIFF_Q
    nq = seq // tq
    head_w = 2 * HEAD_DIM

    def blk(b, h, i):
        return (b, h, i)

    def const2(b, h, i):
        return (0, 0)

    return pl.pallas_call(
        functools.partial(_diff_attn_kernel, lambda_init, running_max),
        grid=(batch, C_HEADS, nq),
        in_specs=[
            pl.BlockSpec((1, head_w, tq), blk),
            pl.BlockSpec((1, seq, head_w), lambda b, h, i: (b, 0, h)),
            pl.BlockSpec((1, C_V_DIM, seq), lambda b, h, i: (b, h, 0)),
            pl.BlockSpec((1, C_V_DIM, tq), blk),
            pl.BlockSpec((C_V_DIM, tq), const2),
            pl.BlockSpec((1, HEAD_DIM), const2),
            pl.BlockSpec((1, HEAD_DIM), const2),
            pl.BlockSpec((1, HEAD_DIM), const2),
            pl.BlockSpec((1, HEAD_DIM), const2),
        ],
        out_specs=pl.BlockSpec((1, C_V_DIM, tq), blk),
        out_shape=jax.ShapeDtypeStruct((batch, C_W, seq), BF16),
        scratch_shapes=[pltpu.VMEM((C_V_DIM, tq), F32), pltpu.VMEM((C_V_DIM, tq), F32)],
        compiler_params=_params(("parallel", "parallel", "arbitrary")),
        name="diff_attn_online" if running_max else "diff_attn_direct",
    )(q_t, k3, v_t, gate_t, subln_tile, lq1, lk1, lq2, lk2)


def _qk_score_bound(q_gain, k_gain):
    bound = HEAD_DIM * QK_SCALE * jnp.max(jnp.abs(q_gain)) * jnp.max(jnp.abs(k_gain))
    return bound.astype(F32) * (1.0 + 2.0 ** -6)


def _rope_tables_t(seq):
    inv = 1.0 / (ROPE_THETA ** (jnp.arange(0, HEAD_DIM, 2, dtype=F32) / HEAD_DIM))
    ang = inv[:, None] * jnp.arange(seq, dtype=F32)[None, :]
    return jnp.cos(ang), jnp.sin(ang)


def _gain_tile(g, width):
    return jnp.broadcast_to(g.astype(F32)[:, None], (g.shape[0], width))


def _rel_bias_t(rel_table, n_q):
    n_k = B_PREV_ROWS + n_q
    width = n_k + n_q
    offset = np.arange(width) - (n_k - 1)
    rel = np.clip(B_PREV_ROWS + offset, -B_MAX_REL, B_MAX_REL) + B_MAX_REL
    by_offset = rel_table.astype(F32)[:, rel]
    heads = rel_table.shape[0]
    sub = 8

    def build(f_ref, o_ref):
        for h in range(heads):
            f = f_ref[h:h + 1, :]
            rows = jnp.concatenate(
                [pltpu.roll(f, (width + b - (sub - 1)) % width, axis=1) for b in range(sub)], axis=0)
            for blk in range(n_k // sub):
                lead = n_k - sub - sub * blk
                o_ref[h, sub * blk:sub * (blk + 1), :] = pltpu.roll(
                    rows, (width - lead) % width, axis=1)[:, :n_q]

    return pl.pallas_call(
        build,
        out_shape=jax.ShapeDtypeStruct((heads, n_k, n_q), F32),
        name="rel_bias_tile",
    )(by_offset)


def kernel(x, ev_norm, ev_w_in, ev_w_out, ev_a_q_norm, ev_a_k_norm, ev_a_sinks,
           ev_b_q_norm, ev_b_k_norm, ev_b_rel_bias, od_norm, od_w_in, od_w_out,
           od_q_norm, od_k_norm, od_lambda_q1, od_lambda_k1, od_lambda_q2,
           od_lambda_k2, od_subln):
    batch, seq, d = x.shape
    cos_t, sin_t = _rope_tables_t(seq)
    depth = ev_norm.shape[0] + od_norm.shape[0]
    assert ev_norm.shape[0] == od_norm.shape[0]
    x2d = x.reshape(batch * seq, d)
    for layer in range(depth):
        i = layer // 2
        if layer % 2 == 0:
            w_in_t = ev_w_in[i].T.astype(BF16)
            w_out_t = ev_w_out[i].T.astype(BF16)
            aq_t, ak, av_t, bq_t, bk, bv_t, gate_t = _even_proj(
                x2d, ev_norm[i][None, :], w_in_t, cos_t, sin_t,
                *[_gain_tile(g[i], EVEN_PROJ_TOKENS)
                  for g in (ev_a_q_norm, ev_a_k_norm, ev_b_q_norm, ev_b_k_norm)],
                batch, seq)
            mask_a = jnp.where(_band_mask_variants(A_PREV_ROWS, MIX_TOKENS), 0.0, NEG_INF).astype(F32)
            bias_b = jnp.where(_band_mask_variants(B_PREV_ROWS, MIX_TOKENS)[:, None],
                               _rel_bias_t(ev_b_rel_bias[i] * LOG2E, MIX_TOKENS)[None], NEG_INF)
            sinks = ev_a_sinks[i].astype(F32)
            mix_args = (sinks * LOG2E, aq_t, ak.reshape(batch, seq, A_KV), av_t,
                        bq_t, bk.reshape(batch, seq, B_W), bv_t, gate_t, mask_a, bias_b)
            bound_a = jnp.maximum(_qk_score_bound(ev_a_q_norm[i], ev_a_k_norm[i]),
                                  jnp.max(jnp.abs(sinks)))
            bound_b = (_qk_score_bound(ev_b_q_norm[i], ev_b_k_norm[i])
                       + jnp.max(jnp.abs(ev_b_rel_bias[i])))
            y_t = lax.cond(jnp.maximum(bound_a, bound_b) <= SAFE_SCORE,
                           functools.partial(_even_mix, False),
                           functools.partial(_even_mix, True), *mix_args)
            even_w_out_t = w_out_t
        else:
            lambda_init = 0.8 - 0.6 * math.exp(-0.3 * layer)
            w_in_t = od_w_in[i].T.astype(BF16)
            w_out_t = od_w_out[i].T.astype(BF16)
            x_t, q_t, k, v_t, gate_t = _odd_proj(
                x2d, y_t, even_w_out_t, od_norm[i][:, None], w_in_t, cos_t, sin_t,
                _gain_tile(od_q_norm[i], ODD_PROJ_TOKENS), _gain_tile(od_k_norm[i], ODD_PROJ_TOKENS))
            diff_args = (q_t, k.reshape(batch, seq, C_QK), v_t, gate_t,
                         _gain_tile(od_subln[i], DIFF_Q),
                         od_lambda_q1[i][None, :], od_lambda_k1[i][None, :],
                         od_lambda_q2[i][None, :], od_lambda_k2[i][None, :])
            y_t = lax.cond(
                _qk_score_bound(od_q_norm[i], od_k_norm[i]) <= SAFE_SCORE,
                lambda *a: _diff_attn(False, *a, lambda_init),
                lambda *a: _diff_attn(True, *a, lambda_init),
                *diff_args)
            x2d = _out_proj(x_t, y_t, w_out_t)
    return x2d.reshape(batch, seq, d)
```

```python
import functools
import math

import jax
import jax.numpy as jnp
import numpy as np
from jax import lax
from jax.experimental import pallas as pl
from jax.experimental.pallas import tpu as pltpu

F32 = jnp.float32
BF16 = jnp.bfloat16

D_MODEL = 1024
CHUNK = 64
HEAD_DIM = 64
HALF_DIM = HEAD_DIM // 2
ROPE_THETA = 10000.0
EPS = 1e-6
NEG_INF = -1e30
QK_SCALE = 1.0 / math.sqrt(HEAD_DIM)
LOG2E = math.log2(math.e)
Q_FOLD = QK_SCALE * LOG2E

A_Q_HEADS = 8
A_KV_HEADS = 2
A_PREV_CHUNKS = 2
B_HEADS = 8
B_PREV_CHUNKS = 8
B_MAX_REL = 128
C_HEADS = 8
C_V_DIM = 2 * HEAD_DIM

A_Q = A_Q_HEADS * HEAD_DIM
A_KV = A_KV_HEADS * HEAD_DIM
B_W = B_HEADS * HEAD_DIM
EVEN_MIX = A_Q + B_W
C_QK = C_HEADS * 2 * HEAD_DIM
C_W = C_HEADS * C_V_DIM
PV_SUM_ROWS = 16
BAND_V_ROWS = HEAD_DIM + PV_SUM_ROWS
A_V_ROWS = A_KV_HEADS * BAND_V_ROWS
B_V_ROWS = B_HEADS * BAND_V_ROWS

V7X_VMEM_LIMIT_BYTES = 56 * 1024 * 1024

EVEN_PROJ_TOKENS = 1024
ODD_PROJ_TOKENS = 512
OUT_PROJ_TOKENS = 2048
MIX_TOKENS = 256
MIX_PV_LAG = 3
A_PREV_ROWS = A_PREV_CHUNKS * CHUNK
B_PREV_ROWS = B_PREV_CHUNKS * CHUNK
DIFF_Q = 2048
DIFF_K = 512
DIFF_DIAG_K = 256
DIFF_PV_LAG = 2
SAFE_SCORE = 40.0


def _params(semantics):
    return pltpu.CompilerParams(dimension_semantics=semantics,
                                vmem_limit_bytes=V7X_VMEM_LIMIT_BYTES)


def _rms_rows(x, gain_row):
    ms = jnp.mean(x * x, axis=-1, keepdims=True)
    return x * lax.rsqrt(ms + EPS) * gain_row


def _proj_t(w_t_ref, lo, width, h):
    return lax.dot_general(w_t_ref[lo:lo + width, :], h,
                           (((1,), (1,)), ((), ())),
                           preferred_element_type=F32)


def _head_norm_t(p, gain_tile, cos=None, sin=None, scale=1.0):
    width, tm = p.shape
    y = p.reshape(width // HEAD_DIM, HEAD_DIM, tm)
    ms = jnp.mean(y * y, axis=1, keepdims=True)
    y = y * lax.rsqrt(ms + EPS) * gain_tile[None]
    if cos is not None:
        y1 = y[:, :HALF_DIM]
        y2 = y[:, HALF_DIM:]
        c = cos[None]
        s = sin[None]
        y = jnp.concatenate([y1 * c - y2 * s, y2 * c + y1 * s], axis=1)
    if scale != 1.0:
        y = y * scale
    return y.reshape(width, tm)


def _silu(x):
    return x * (1.0 / (1.0 + jnp.exp(-x)))


def _even_proj_kernel(x_ref, g_ref, w_t_ref, cos_ref, sin_ref,
                      aqn_ref, akn_ref, bqn_ref, bkn_ref,
                      aq_ref, ak_ref, av_ref, bq_ref, bk_ref, bv_ref, gate_ref):
    h = _rms_rows(x_ref[...], g_ref[...]).astype(BF16)
    cos = cos_ref[...]
    sin = sin_ref[...]
    a_w = 2 * A_Q + 2 * A_KV
    a_all = _proj_t(w_t_ref, 0, a_w, h)
    b_qk = _proj_t(w_t_ref, a_w, 2 * B_W, h)
    b_vg = _proj_t(w_t_ref, a_w + 2 * B_W, 2 * B_W, h)
    ak = _head_norm_t(a_all[A_Q:A_Q + A_KV], akn_ref[...], cos, sin)
    ak_ref[...] = ak.T.astype(BF16)
    aq_ref[0] = _head_norm_t(a_all[:A_Q], aqn_ref[...], cos, sin, Q_FOLD).astype(BF16)
    _store_v_with_ones(av_ref, a_all[A_Q + A_KV:A_Q + 2 * A_KV])
    gate_ref[0, :A_Q, :] = _silu(a_all[A_Q + 2 * A_KV:])
    bk = _head_norm_t(b_qk[B_W:], bkn_ref[...])
    bk_ref[...] = bk.T.astype(BF16)
    bq_ref[0] = _head_norm_t(b_qk[:B_W], bqn_ref[...], scale=Q_FOLD).astype(BF16)
    gate_ref[0, A_Q:, :] = _silu(b_vg[B_W:])
    _store_v_with_ones(bv_ref, b_vg[:B_W])


def _store_v_with_ones(v_ref, v):
    tm = v.shape[1]
    ones = jnp.ones((PV_SUM_ROWS, tm), BF16)
    for h in range(v.shape[0] // HEAD_DIM):
        v_ref[0, h * BAND_V_ROWS:h * BAND_V_ROWS + HEAD_DIM, :] = (
            v[h * HEAD_DIM:(h + 1) * HEAD_DIM].astype(BF16))
        v_ref[0, h * BAND_V_ROWS + HEAD_DIM:(h + 1) * BAND_V_ROWS, :] = ones


def _even_proj(x2d, gain, w_t, cos_t, sin_t, aqn, akn, bqn, bkn, batch, seq):
    tm = EVEN_PROJ_TOKENS
    nt = seq // tm
    tokens = batch * seq
    even_in = w_t.shape[0]

    def tok(b, i):
        return (b * nt + i, 0)

    def feat(b, i):
        return (b, 0, i)

    def const2(b, i):
        return (0, 0)

    def tab(b, i):
        return (0, i)

    in_specs = [
        pl.BlockSpec((tm, D_MODEL), tok),
        pl.BlockSpec((1, D_MODEL), const2),
        pl.BlockSpec((even_in, D_MODEL), const2, pipeline_mode=pl.Buffered(1)),
        pl.BlockSpec((HALF_DIM, tm), tab),
        pl.BlockSpec((HALF_DIM, tm), tab),
        pl.BlockSpec((HEAD_DIM, tm), const2),
        pl.BlockSpec((HEAD_DIM, tm), const2),
        pl.BlockSpec((HEAD_DIM, tm), const2),
        pl.BlockSpec((HEAD_DIM, tm), const2),
    ]
    out_shape = [
        jax.ShapeDtypeStruct((batch, A_Q, seq), BF16),
        jax.ShapeDtypeStruct((tokens, A_KV), BF16),
        jax.ShapeDtypeStruct((batch, A_V_ROWS, seq), BF16),
        jax.ShapeDtypeStruct((batch, B_W, seq), BF16),
        jax.ShapeDtypeStruct((tokens, B_W), BF16),
        jax.ShapeDtypeStruct((batch, B_V_ROWS, seq), BF16),
        jax.ShapeDtypeStruct((batch, EVEN_MIX, seq), F32),
    ]
    out_specs = [
        pl.BlockSpec((1, A_Q, tm), feat),
        pl.BlockSpec((tm, A_KV), tok),
        pl.BlockSpec((1, A_V_ROWS, tm), feat),
        pl.BlockSpec((1, B_W, tm), feat),
        pl.BlockSpec((tm, B_W), tok),
        pl.BlockSpec((1, B_V_ROWS, tm), feat),
        pl.BlockSpec((1, EVEN_MIX, tm), feat),
    ]
    return pl.pallas_call(
        _even_proj_kernel,
        grid=(batch, nt),
        in_specs=in_specs,
        out_specs=out_specs,
        out_shape=out_shape,
        compiler_params=_params(("parallel", "parallel")),
        name="even_proj",
    )(x2d, gain, w_t, cos_t, sin_t, aqn, akn, bqn, bkn)


def _band_mask_np(prev_rows, n_q, q_block):
    key = np.arange(prev_rows + n_q)[:, None]
    qry = np.arange(n_q)[None, :] + prev_rows
    return ((key // CHUNK <= qry // CHUNK)
            & (key // CHUNK >= qry // CHUNK - prev_rows // CHUNK)
            & (key >= prev_rows - q_block * n_q))


def _band_mask_variants(prev_rows, n_q):
    n = -(-prev_rows // n_q) + 1
    return np.stack([_band_mask_np(prev_rows, n_q, v) for v in range(n)])


def _band_probs(running_max, k_win, q_pad, add_tile, sink):
    s = jnp.dot(k_win, q_pad, preferred_element_type=F32) + add_tile
    if running_max:
        m = jnp.max(s, axis=0, keepdims=True)
        if sink is not None:
            m = jnp.maximum(m, sink)
        s = s - m
        sink = None if sink is None else sink - m
    sink_term = None if sink is None else jnp.exp2(jnp.broadcast_to(sink, (1, s.shape[1])))
    return jnp.exp2(s).astype(BF16), sink_term


def _pad_q(q_h, half):
    z = jnp.zeros_like(q_h)
    return jnp.concatenate([q_h, z] if half == 0 else [z, q_h], axis=0)


def _even_mix_kernel(running_max, sinks_ref, aq_ref, akp_ref, akc_ref, avp_ref, avc_ref,
                     bq_ref, bkp2_ref, bkp1_ref, bkc_ref, bvp2_ref, bvp1_ref, bvc_ref,
                     gate_ref, mask_ref, bias_ref, y_ref):
    ak = jnp.concatenate([akp_ref[0], akc_ref[0]], axis=0)
    av = jnp.concatenate([avp_ref[0], avc_ref[0]], axis=1)
    bk = jnp.concatenate([bkp2_ref[0], bkp1_ref[0], bkc_ref[0]], axis=0)
    bv = jnp.concatenate([bvp2_ref[0], bvp1_ref[0], bvc_ref[0]], axis=1)

    group = A_Q_HEADS // A_KV_HEADS
    heads = []
    for h in range(A_Q_HEADS):
        kv = h // group
        heads.append((ak, _pad_q(aq_ref[0, h * HEAD_DIM:(h + 1) * HEAD_DIM, :], kv),
                      av[kv * BAND_V_ROWS:(kv + 1) * BAND_V_ROWS, :], mask_ref[0], sinks_ref[h],
                      h * HEAD_DIM))
    for h in range(B_HEADS):
        pair = h // 2
        heads.append((bk[:, pair * 128:(pair + 1) * 128],
                      _pad_q(bq_ref[0, h * HEAD_DIM:(h + 1) * HEAD_DIM, :], h % 2),
                      bv[h * BAND_V_ROWS:(h + 1) * BAND_V_ROWS, :], bias_ref[0, h], None,
                      A_Q + h * HEAD_DIM))

    pending = []

    def emit_pv():
        v_t, p, sink_term, row0 = pending.pop(0)
        o = jnp.dot(v_t, p, preferred_element_type=F32)
        l = o[HEAD_DIM:HEAD_DIM + 1]
        if sink_term is not None:
            l = l + sink_term
        rows = slice(row0, row0 + HEAD_DIM)
        y_ref[0, rows, :] = (o[:HEAD_DIM] * (1.0 / l) * gate_ref[0, rows, :]).astype(BF16)

    for k_win, q_pad, v_t, add_tile, sink, row0 in heads:
        p, sink_term = _band_probs(running_max, k_win, q_pad, add_tile, sink)
        pending.append((v_t, p, sink_term, row0))
        if len(pending) > MIX_PV_LAG:
            emit_pv()
    while pending:
        emit_pv()


def _even_mix(running_max, sinks, aq_t, ak3, av_t, bq_t, bk3, bv_t, gate_t, mask_a, bias_b):
    batch, seq = ak3.shape[0], ak3.shape[1]
    tq = MIX_TOKENS
    nq = seq // tq
    a_ratio = tq // A_PREV_ROWS

    def cur_t(b, i, *_):
        return (b, 0, i)

    def cur_n(b, i, *_):
        return (b, i, 0)

    def a_prev_n(b, i, *_):
        return (b, jnp.maximum(i * a_ratio - 1, 0), 0)

    def a_prev_t(b, i, *_):
        return (b, 0, jnp.maximum(i * a_ratio - 1, 0))

    def b_prev_n(back):
        return lambda b, i, *_: (b, jnp.maximum(i - back, 0), 0)

    def b_prev_t(back):
        return lambda b, i, *_: (b, 0, jnp.maximum(i - back, 0))

    in_specs = [
        pl.BlockSpec((1, A_Q, tq), cur_t),
        pl.BlockSpec((1, A_PREV_ROWS, A_KV), a_prev_n),
        pl.BlockSpec((1, tq, A_KV), cur_n),
        pl.BlockSpec((1, A_V_ROWS, A_PREV_ROWS), a_prev_t),
        pl.BlockSpec((1, A_V_ROWS, tq), cur_t),
        pl.BlockSpec((1, B_W, tq), cur_t),
        pl.BlockSpec((1, tq, B_W), b_prev_n(2)),
        pl.BlockSpec((1, tq, B_W), b_prev_n(1)),
        pl.BlockSpec((1, tq, B_W), cur_n),
        pl.BlockSpec((1, B_V_ROWS, tq), b_prev_t(2)),
        pl.BlockSpec((1, B_V_ROWS, tq), b_prev_t(1)),
        pl.BlockSpec((1, B_V_ROWS, tq), cur_t),
        pl.BlockSpec((1, EVEN_MIX, tq), cur_t),
        pl.BlockSpec((1,) + mask_a.shape[1:],
                     lambda b, i, *_: (jnp.minimum(i, mask_a.shape[0] - 1), 0, 0)),
        pl.BlockSpec((1,) + bias_b.shape[1:],
                     lambda b, i, *_: (jnp.minimum(i, bias_b.shape[0] - 1), 0, 0, 0)),
    ]
    grid_spec = pltpu.PrefetchScalarGridSpec(
        num_scalar_prefetch=1,
        grid=(batch, nq),
        in_specs=in_specs,
        out_specs=pl.BlockSpec((1, EVEN_MIX, tq), cur_t),
    )
    return pl.pallas_call(
        functools.partial(_even_mix_kernel, running_max),
        grid_spec=grid_spec,
        out_shape=jax.ShapeDtypeStruct((batch, EVEN_MIX, seq), BF16),
        compiler_params=_params(("parallel", "parallel")),
        name="even_mix_online" if running_max else "even_mix_direct",
    )(sinks, aq_t, ak3, ak3, av_t, av_t, bq_t, bk3, bk3, bk3, bv_t, bv_t, bv_t,
      gate_t, mask_a, bias_b)


def _out_proj_kernel(x_ref, y_ref, w_t_ref, o_ref):
    out_t = jnp.dot(w_t_ref[...], y_ref[0], preferred_element_type=F32)
    o_ref[...] = (x_ref[0] + out_t).T


def _out_proj(x_t, y_t, w_t):
    batch, width, seq = y_t.shape
    tm = OUT_PROJ_TOKENS
    nt = seq // tm
    return pl.pallas_call(
        _out_proj_kernel,
        grid=(batch, nt),
        in_specs=[
            pl.BlockSpec((1, D_MODEL, tm), lambda b, i: (b, 0, i)),
            pl.BlockSpec((1, width, tm), lambda b, i: (b, 0, i)),
            pl.BlockSpec((D_MODEL, width), lambda b, i: (0, 0)),
        ],
        out_specs=pl.BlockSpec((tm, D_MODEL), lambda b, i: (b * nt + i, 0)),
        out_shape=jax.ShapeDtypeStruct((batch * seq, D_MODEL), F32),
        compiler_params=_params(("parallel", "parallel")),
        name="out_proj",
    )(x_t, y_t, w_t)


def _odd_proj_kernel(x_ref, y_ref, w_out_t_ref, g_ref, w_t_ref, cos_ref, sin_ref, qn_ref, kn_ref,
                     x1_ref, q_ref, k_ref, v_ref, gate_ref):
    out_t = jnp.dot(w_out_t_ref[...], y_ref[0], preferred_element_type=F32)
    x = x_ref[...].T + out_t
    x1_ref[0] = x
    ms = jnp.mean(x * x, axis=0, keepdims=True)
    h_t = (x * lax.rsqrt(ms + EPS) * g_ref[...]).astype(BF16)
    cos = cos_ref[...]
    sin = sin_ref[...]

    def proj(lo, width):
        return jnp.dot(w_t_ref[lo:lo + width, :], h_t, preferred_element_type=F32)

    k = proj(C_QK, C_QK)
    q = proj(0, C_QK)
    g = proj(2 * C_QK + C_W, C_W)
    v = proj(2 * C_QK, C_W)
    k = _head_norm_t(k, kn_ref[...], cos, sin)
    k_ref[...] = k.T.astype(BF16)
    q_ref[0] = _head_norm_t(q, qn_ref[...], cos, sin, Q_FOLD).astype(BF16)
    gate_ref[0] = _silu(g)
    v_ref[0] = v.astype(BF16)


def _odd_proj(x2d, y_t, w_out_t, gain_col, w_t, cos_t, sin_t, qn, kn):
    batch, mix_w, seq = y_t.shape
    tm = ODD_PROJ_TOKENS
    nt = seq // tm
    tokens = batch * seq
    odd_in = w_t.shape[0]
    resident = pl.Buffered(1)

    def tok(b, i):
        return (b * nt + i, 0)

    def feat(b, i):
        return (b, 0, i)

    def const2(b, i):
        return (0, 0)

    def tab(b, i):
        return (0, i)

    return pl.pallas_call(
        _odd_proj_kernel,
        grid=(batch, nt),
        in_specs=[
            pl.BlockSpec((tm, D_MODEL), tok),
            pl.BlockSpec((1, mix_w, tm), feat),
            pl.BlockSpec((D_MODEL, mix_w), const2, pipeline_mode=resident),
            pl.BlockSpec((D_MODEL, 1), const2),
            pl.BlockSpec((odd_in, D_MODEL), const2, pipeline_mode=resident),
            pl.BlockSpec((HALF_DIM, tm), tab),
            pl.BlockSpec((HALF_DIM, tm), tab),
            pl.BlockSpec((HEAD_DIM, tm), const2),
            pl.BlockSpec((HEAD_DIM, tm), const2),
        ],
        out_specs=[
            pl.BlockSpec((1, D_MODEL, tm), feat),
            pl.BlockSpec((1, C_QK, tm), feat),
            pl.BlockSpec((tm, C_QK), tok),
            pl.BlockSpec((1, C_W, tm), feat),
            pl.BlockSpec((1, C_W, tm), feat),
        ],
        out_shape=[
            jax.ShapeDtypeStruct((batch, D_MODEL, seq), F32),
            jax.ShapeDtypeStruct((batch, C_QK, seq), BF16),
            jax.ShapeDtypeStruct((tokens, C_QK), BF16),
            jax.ShapeDtypeStruct((batch, C_W, seq), BF16),
            jax.ShapeDtypeStruct((batch, C_W, seq), F32),
        ],
        compiler_params=_params(("parallel", "parallel")),
        name="odd_proj",
    )(x2d, y_t, w_out_t, gain_col, w_t, cos_t, sin_t, qn, kn)


def _diff_attn_kernel(lambda_init, running_max, q_ref, k_ref, v_ref, gate_ref, subln_ref,
                      lq1_ref, lk1_ref, lq2_ref, lk2_ref, y_ref, acc1_ref, acc2_ref):
    tq, tk = DIFF_Q, DIFF_K
    i = pl.program_id(2)
    q = q_ref[0]
    row = lax.broadcasted_iota(jnp.int32, q.shape, 0)
    zero = jnp.zeros_like(q)
    q_pads = (jnp.where(row < HEAD_DIM, q, zero), jnp.where(row >= HEAD_DIM, q, zero))
    acc_refs = (acc1_ref, acc2_ref)

    def load_kv(start, size):
        start = pl.multiple_of(start, size)
        return k_ref[0, pl.ds(start, size), :], v_ref[0, :, pl.ds(start, size)]

    def diag_valid(key0, size, n_cols):
        key_chunk = (lax.broadcasted_iota(jnp.int32, (size, n_cols), 0) + key0) // CHUNK
        qry_chunk = (lax.broadcasted_iota(jnp.int32, (size, n_cols), 1) + (tq - n_cols)) // CHUNK
        return key_chunk <= qry_chunk

    def online_step(k, v_t, carry, valid):
        new = []
        for c in range(2):
            m, l = carry[2 * c], carry[2 * c + 1]
            s = jnp.dot(k, q_pads[c], preferred_element_type=F32)
            if valid is not None:
                s = jnp.where(valid, s, NEG_INF)
            m_new = jnp.maximum(m, jnp.max(s, axis=0, keepdims=True))
            alpha = jnp.exp2(m - m_new)
            p = jnp.exp2(s - m_new)
            l_new = alpha * l + jnp.sum(p, axis=0, keepdims=True)
            pv = jnp.dot(v_t, p.astype(BF16), preferred_element_type=F32)
            acc_refs[c][...] = alpha * acc_refs[c][...] + pv
            new += [m_new, l_new]
        return tuple(new)

    def direct_probs(k, q_c, valid):
        p = jnp.exp2(jnp.dot(k, q_c, preferred_element_type=F32))
        if valid is not None:
            p = jnp.where(valid, p, 0.0)
        return p.astype(BF16), jnp.sum(p, axis=0, keepdims=True)

    def direct_blocks(blocks, l_sums, pv_lag):
        l_sums = list(l_sums)
        pv_sum = {}
        pending = []

        edges = sorted({col0 for _, _, col0, _ in blocks}) + [tq]

        def emit_pv():
            v_t, c, col0, p = pending.pop(0)
            pv = jnp.dot(v_t, p, preferred_element_type=F32)
            for lo, hi in zip(edges[:-1], edges[1:]):
                if lo >= col0:
                    part = pv[:, lo - col0:hi - col0]
                    pv_sum[(c, lo, hi)] = pv_sum[(c, lo, hi)] + part if (c, lo, hi) in pv_sum else part

        for k, v_t, col0, valid in blocks:
            for c in range(2):
                p, total = direct_probs(k, q_pads[c][:, col0:], valid)
                if col0:
                    total = jnp.concatenate([jnp.zeros((1, col0), F32), total], axis=1)
                l_sums[c] = l_sums[c] + total
                pending.append((v_t, c, col0, p))
                if len(pending) > pv_lag:
                    emit_pv()
        while pending:
            emit_pv()
        for (c, lo, hi), pv in pv_sum.items():
            acc_refs[c][:, lo:hi] += pv
        return tuple(l_sums)

    q0 = i * tq
    acc1_ref[...] = jnp.zeros_like(acc1_ref)
    acc2_ref[...] = jnp.zeros_like(acc2_ref)
    if running_max:
        init = (jnp.full((1, tq), NEG_INF, F32), jnp.zeros((1, tq), F32)) * 2
        carry = lax.fori_loop(0, q0 // tk,
                              lambda j, c: online_step(*load_kv(j * tk, tk), c, None), init)
        for key0 in range(0, tq, tk):
            carry = online_step(*load_kv(q0 + key0, tk), carry, diag_valid(key0, tk, tq))
        l1, l2 = carry[1], carry[3]
    else:
        def group(t, l_sums):
            return direct_blocks(
                [load_kv(t * tq + u * tk, tk) + (0, None) for u in range(tq // tk)], l_sums,
                pv_lag=DIFF_PV_LAG)

        l_sums = lax.fori_loop(0, i, group, (jnp.zeros((1, tq), F32),) * 2)
        td = DIFF_DIAG_K
        l1, l2 = direct_blocks(
            [load_kv(q0 + key0, td) + (key0, diag_valid(key0, td, tq - key0))
             for key0 in range(0, tq, td)], l_sums, pv_lag=1)

    lam = (jnp.exp(jnp.sum(lq1_ref[...] * lk1_ref[...], axis=-1, keepdims=True))
           - jnp.exp(jnp.sum(lq2_ref[...] * lk2_ref[...], axis=-1, keepdims=True))
           + lambda_init)
    o = acc1_ref[...] * (1.0 / l1) - lam * (acc2_ref[...] * (1.0 / l2))
    ms = jnp.mean(o * o, axis=0, keepdims=True)
    o = o * lax.rsqrt(ms + EPS) * subln_ref[...] * (1.0 - lambda_init)
    y_ref[0] = (o * gate_ref[0]).astype(BF16)


def _diff_attn(running_max, q_t, k3, v_t, gate_t, subln_tile, lq1, lk1, lq2, lk2, lambda_init):
    batch, seq = k3.shape[0], k3.shape[1]
    tq = DIFF_Q
    nq = seq // tq
    head_w = 2 * HEAD_DIM

    def blk(b, h, i):
        return (b, h, i)

    def const2(b, h, i):
        return (0, 0)

    return pl.pallas_call(
        functools.partial(_diff_attn_kernel, lambda_init, running_max),
        grid=(batch, C_HEADS, nq),
        in_specs=[
            pl.BlockSpec((1, head_w, tq), blk),
            pl.BlockSpec((1, seq, head_w), lambda b, h, i: (b, 0, h)),
            pl.BlockSpec((1, C_V_DIM, seq), lambda b, h, i: (b, h, 0)),
            pl.BlockSpec((1, C_V_DIM, tq), blk),
            pl.BlockSpec((C_V_DIM, tq), const2),
            pl.BlockSpec((1, HEAD_DIM), const2),
            pl.BlockSpec((1, HEAD_DIM), const2),
            pl.BlockSpec((1, HEAD_DIM), const2),
            pl.BlockSpec((1, HEAD_DIM), const2),
        ],
        out_specs=pl.BlockSpec((1, C_V_DIM, tq), blk),
        out_shape=jax.ShapeDtypeStruct((batch, C_W, seq), BF16),
        scratch_shapes=[pltpu.VMEM((C_V_DIM, tq), F32), pltpu.VMEM((C_V_DIM, tq), F32)],
        compiler_params=_params(("parallel", "parallel", "arbitrary")),
        name="diff_attn_online" if running_max else "diff_attn_direct",
    )(q_t, k3, v_t, gate_t, subln_tile, lq1, lk1, lq2, lk2)


def _qk_score_bound(q_gain, k_gain):
    bound = HEAD_DIM * QK_SCALE * jnp.max(jnp.abs(q_gain)) * jnp.max(jnp.abs(k_gain))
    return bound.astype(F32) * (1.0 + 2.0 ** -6)


def _rope_tables_t(seq):
    inv = 1.0 / (ROPE_THETA ** (jnp.arange(0, HEAD_DIM, 2, dtype=F32) / HEAD_DIM))
    ang = inv[:, None] * jnp.arange(seq, dtype=F32)[None, :]
    return jnp.cos(ang), jnp.sin(ang)


def _gain_tile(g, width):
    return jnp.broadcast_to(g.astype(F32)[:, None], (g.shape[0], width))


def _rel_bias_t(rel_table, n_q):
    n_k = B_PREV_ROWS + n_q
    width = n_k + n_q
    offset = np.arange(width) - (n_k - 1)
    rel = np.clip(B_PREV_ROWS + offset, -B_MAX_REL, B_MAX_REL) + B_MAX_REL
    by_offset = rel_table.astype(F32)[:, rel]
    heads = rel_table.shape[0]
    sub = 8

    def build(f_ref, o_ref):
        for h in range(heads):
            f = f_ref[h:h + 1, :]
            rows = jnp.concatenate(
                [pltpu.roll(f, (width + b - (sub - 1)) % width, axis=1) for b in range(sub)], axis=0)
            for blk in range(n_k // sub):
                lead = n_k - sub - sub * blk
                o_ref[h, sub * blk:sub * (blk + 1), :] = pltpu.roll(
                    rows, (width - lead) % width, axis=1)[:, :n_q]

    return pl.pallas_call(
        build,
        out_shape=jax.ShapeDtypeStruct((heads, n_k, n_q), F32),
        name="rel_bias_tile",
    )(by_offset)


def kernel(x, ev_norm, ev_w_in, ev_w_out, ev_a_q_norm, ev_a_k_norm, ev_a_sinks,
           ev_b_q_norm, ev_b_k_norm, ev_b_rel_bias, od_norm, od_w_in, od_w_out,
           od_q_norm, od_k_norm, od_lambda_q1, od_lambda_k1, od_lambda_q2,
           od_lambda_k2, od_subln):
    batch, seq, d = x.shape
    cos_t, sin_t = _rope_tables_t(seq)
    depth = ev_norm.shape[0] + od_norm.shape[0]
    assert ev_norm.shape[0] == od_norm.shape[0]
    x2d = x.reshape(batch * seq, d)
    for layer in range(depth):
        i = layer // 2
        if layer % 2 == 0:
            w_in_t = ev_w_in[i].T.astype(BF16)
            w_out_t = ev_w_out[i].T.astype(BF16)
            aq_t, ak, av_t, bq_t, bk, bv_t, gate_t = _even_proj(
                x2d, ev_norm[i][None, :], w_in_t, cos_t, sin_t,
                *[_gain_tile(g[i], EVEN_PROJ_TOKENS)
                  for g in (ev_a_q_norm, ev_a_k_norm, ev_b_q_norm, ev_b_k_norm)],
                batch, seq)
            mask_a = jnp.where(_band_mask_variants(A_PREV_ROWS, MIX_TOKENS), 0.0, NEG_INF).astype(F32)
            bias_b = jnp.where(_band_mask_variants(B_PREV_ROWS, MIX_TOKENS)[:, None],
                               _rel_bias_t(ev_b_rel_bias[i] * LOG2E, MIX_TOKENS)[None], NEG_INF)
            sinks = ev_a_sinks[i].astype(F32)
            mix_args = (sinks * LOG2E, aq_t, ak.reshape(batch, seq, A_KV), av_t,
                        bq_t, bk.reshape(batch, seq, B_W), bv_t, gate_t, mask_a, bias_b)
            bound_a = jnp.maximum(_qk_score_bound(ev_a_q_norm[i], ev_a_k_norm[i]),
                                  jnp.max(jnp.abs(sinks)))
            bound_b = (_qk_score_bound(ev_b_q_norm[i], ev_b_k_norm[i])
                       + jnp.max(jnp.abs(ev_b_rel_bias[i])))
            y_t = lax.cond(jnp.maximum(bound_a, bound_b) <= SAFE_SCORE,
                           functools.partial(_even_mix, False),
                           functools.partial(_even_mix, True), *mix_args)
            even_w_out_t = w_out_t
        else:
            lambda_init = 0.8 - 0.6 * math.exp(-0.3 * layer)
            w_in_t = od_w_in[i].T.astype(BF16)
            w_out_t = od_w_out[i].T.astype(BF16)
            x_t, q_t, k, v_t, gate_t = _odd_proj(
                x2d, y_t, even_w_out_t, od_norm[i][:, None], w_in_t, cos_t, sin_t,
                _gain_tile(od_q_norm[i], ODD_PROJ_TOKENS), _gain_tile(od_k_norm[i], ODD_PROJ_TOKENS))
            diff_args = (q_t, k.reshape(batch, seq, C_QK), v_t, gate_t,
                         _gain_tile(od_subln[i], DIFF_Q),
                         od_lambda_q1[i][None, :], od_lambda_k1[i][None, :],
                         od_lambda_q2[i][None, :], od_lambda_k2[i][None, :])
            y_t = lax.cond(
                _qk_score_bound(od_q_norm[i], od_k_norm[i]) <= SAFE_SCORE,
                lambda *a: _diff_attn(False, *a, lambda_init),
                lambda *a: _diff_attn(True, *a, lambda_init),
                *diff_args)
            x2d = _out_proj(x_t, y_t, w_out_t)
    return x2d.reshape(batch, seq, d)
```

```python
import functools
import math

import jax
import jax.numpy as jnp
import numpy as np
from jax import lax
from jax.experimental import pallas as pl
from jax.experimental.pallas import tpu as pltpu

F32 = jnp.float32
BF16 = jnp.bfloat16

D_MODEL = 1024
CHUNK = 64
HEAD_DIM = 64
HALF_DIM = HEAD_DIM // 2
ROPE_THETA = 10000.0
EPS = 1e-6
NEG_INF = -1e30
QK_SCALE = 1.0 / math.sqrt(HEAD_DIM)
LOG2E = math.log2(math.e)
Q_FOLD = QK_SCALE * LOG2E

A_Q_HEADS = 8
A_KV_HEADS = 2
A_PREV_CHUNKS = 2
B_HEADS = 8
B_PREV_CHUNKS = 8
B_MAX_REL = 128
C_HEADS = 8
C_V_DIM = 2 * HEAD_DIM

A_Q = A_Q_HEADS * HEAD_DIM
A_KV = A_KV_HEADS * HEAD_DIM
B_W = B_HEADS * HEAD_DIM
EVEN_MIX = A_Q + B_W
C_QK = C_HEADS * 2 * HEAD_DIM
C_W = C_HEADS * C_V_DIM
PV_SUM_ROWS = 16
BAND_V_ROWS = HEAD_DIM + PV_SUM_ROWS
A_V_ROWS = A_KV_HEADS * BAND_V_ROWS
B_V_ROWS = B_HEADS * BAND_V_ROWS

V7X_VMEM_LIMIT_BYTES = 56 * 1024 * 1024

EVEN_PROJ_TOKENS = 1024
ODD_PROJ_TOKENS = 512
OUT_PROJ_TOKENS = 2048
MIX_TOKENS = 256
MIX_PV_LAG = 4
A_PREV_ROWS = A_PREV_CHUNKS * CHUNK
B_PREV_ROWS = B_PREV_CHUNKS * CHUNK
DIFF_Q = 2048
DIFF_K = 512
DIFF_DIAG_K = 256
DIFF_PV_LAG = 3
SAFE_SCORE = 40.0


def _params(semantics):
    return pltpu.CompilerParams(dimension_semantics=semantics,
                                vmem_limit_bytes=V7X_VMEM_LIMIT_BYTES)


def _rms_rows(x, gain_row):
    ms = jnp.mean(x * x, axis=-1, keepdims=True)
    return x * lax.rsqrt(ms + EPS) * gain_row


def _proj_t(w_t_ref, lo, width, h):
    return lax.dot_general(w_t_ref[lo:lo + width, :], h,
                           (((1,), (1,)), ((), ())),
                           preferred_element_type=F32)


def _head_norm_t(p, gain_tile, cos=None, sin=None, scale=1.0):
    width, tm = p.shape
    y = p.reshape(width // HEAD_DIM, HEAD_DIM, tm)
    ms = jnp.mean(y * y, axis=1, keepdims=True)
    y = y * lax.rsqrt(ms + EPS) * gain_tile[None]
    if cos is not None:
        y1 = y[:, :HALF_DIM]
        y2 = y[:, HALF_DIM:]
        c = cos[None]
        s = sin[None]
        y = jnp.concatenate([y1 * c - y2 * s, y2 * c + y1 * s], axis=1)
    if scale != 1.0:
        y = y * scale
    return y.reshape(width, tm)


def _silu(x):
    return x * (1.0 / (1.0 + jnp.exp(-x)))


def _even_proj_kernel(x_ref, g_ref, w_t_ref, cos_ref, sin_ref,
                      aqn_ref, akn_ref, bqn_ref, bkn_ref,
                      aq_ref, ak_ref, av_ref, bq_ref, bk_ref, bv_ref, gate_ref):
    h = _rms_rows(x_ref[...], g_ref[...]).astype(BF16)
    cos = cos_ref[...]
    sin = sin_ref[...]
    a_w = 2 * A_Q + 2 * A_KV
    a_all = _proj_t(w_t_ref, 0, a_w, h)
    b_qk = _proj_t(w_t_ref, a_w, 2 * B_W, h)
    b_vg = _proj_t(w_t_ref, a_w + 2 * B_W, 2 * B_W, h)
    ak = _head_norm_t(a_all[A_Q:A_Q + A_KV], akn_ref[...], cos, sin)
    ak_ref[...] = ak.T.astype(BF16)
    aq_ref[0] = _head_norm_t(a_all[:A_Q], aqn_ref[...], cos, sin, Q_FOLD).astype(BF16)
    _store_v_with_ones(av_ref, a_all[A_Q + A_KV:A_Q + 2 * A_KV])
    gate_ref[0, :A_Q, :] = _silu(a_all[A_Q + 2 * A_KV:])
    bk = _head_norm_t(b_qk[B_W:], bkn_ref[...])
    bk_ref[...] = bk.T.astype(BF16)
    bq_ref[0] = _head_norm_t(b_qk[:B_W], bqn_ref[...], scale=Q_FOLD).astype(BF16)
    gate_ref[0, A_Q:, :] = _silu(b_vg[B_W:])
    _store_v_with_ones(bv_ref, b_vg[:B_W])


def _store_v_with_ones(v_ref, v):
    tm = v.shape[1]
    ones = jnp.ones((PV_SUM_ROWS, tm), BF16)
    for h in range(v.shape[0] // HEAD_DIM):
        v_ref[0, h * BAND_V_ROWS:h * BAND_V_ROWS + HEAD_DIM, :] = (
            v[h * HEAD_DIM:(h + 1) * HEAD_DIM].astype(BF16))
        v_ref[0, h * BAND_V_ROWS + HEAD_DIM:(h + 1) * BAND_V_ROWS, :] = ones


def _even_proj(x2d, gain, w_t, cos_t, sin_t, aqn, akn, bqn, bkn, batch, seq):
    tm = EVEN_PROJ_TOKENS
    nt = seq // tm
    tokens = batch * seq
    even_in = w_t.shape[0]

    def tok(b, i):
        return (b * nt + i, 0)

    def feat(b, i):
        return (b, 0, i)

    def const2(b, i):
        return (0, 0)

    def tab(b, i):
        return (0, i)

    in_specs = [
        pl.BlockSpec((tm, D_MODEL), tok),
        pl.BlockSpec((1, D_MODEL), const2),
        pl.BlockSpec((even_in, D_MODEL), const2, pipeline_mode=pl.Buffered(1)),
        pl.BlockSpec((HALF_DIM, tm), tab),
        pl.BlockSpec((HALF_DIM, tm), tab),
        pl.BlockSpec((HEAD_DIM, tm), const2),
        pl.BlockSpec((HEAD_DIM, tm), const2),
        pl.BlockSpec((HEAD_DIM, tm), const2),
        pl.BlockSpec((HEAD_DIM, tm), const2),
    ]
    out_shape = [
        jax.ShapeDtypeStruct((batch, A_Q, seq), BF16),
        jax.ShapeDtypeStruct((tokens, A_KV), BF16),
        jax.ShapeDtypeStruct((batch, A_V_ROWS, seq), BF16),
        jax.ShapeDtypeStruct((batch, B_W, seq), BF16),
        jax.ShapeDtypeStruct((tokens, B_W), BF16),
        jax.ShapeDtypeStruct((batch, B_V_ROWS, seq), BF16),
        jax.ShapeDtypeStruct((batch, EVEN_MIX, seq), F32),
    ]
    out_specs = [
        pl.BlockSpec((1, A_Q, tm), feat),
        pl.BlockSpec((tm, A_KV), tok),
        pl.BlockSpec((1, A_V_ROWS, tm), feat),
        pl.BlockSpec((1, B_W, tm), feat),
        pl.BlockSpec((tm, B_W), tok),
        pl.BlockSpec((1, B_V_ROWS, tm), feat),
        pl.BlockSpec((1, EVEN_MIX, tm), feat),
    ]
    return pl.pallas_call(
        _even_proj_kernel,
        grid=(batch, nt),
        in_specs=in_specs,
        out_specs=out_specs,
        out_shape=out_shape,
        compiler_params=_params(("parallel", "parallel")),
        name="even_proj",
    )(x2d, gain, w_t, cos_t, sin_t, aqn, akn, bqn, bkn)


def _band_mask_np(prev_rows, n_q, q_block):
    key = np.arange(prev_rows + n_q)[:, None]
    qry = np.arange(n_q)[None, :] + prev_rows
    return ((key // CHUNK <= qry // CHUNK)
            & (key // CHUNK >= qry // CHUNK - prev_rows // CHUNK)
            & (key >= prev_rows - q_block * n_q))


def _band_mask_variants(prev_rows, n_q):
    n = -(-prev_rows // n_q) + 1
    return np.stack([_band_mask_np(prev_rows, n_q, v) for v in range(n)])


def _band_probs(running_max, k_win, q_pad, add_tile, sink):
    s = jnp.dot(k_win, q_pad, preferred_element_type=F32) + add_tile
    if running_max:
        m = jnp.max(s, axis=0, keepdims=True)
        if sink is not None:
            m = jnp.maximum(m, sink)
        s = s - m
        sink = None if sink is None else sink - m
    sink_term = None if sink is None else jnp.exp2(jnp.broadcast_to(sink, (1, s.shape[1])))
    return jnp.exp2(s).astype(BF16), sink_term


def _pad_q(q_h, half):
    z = jnp.zeros_like(q_h)
    return jnp.concatenate([q_h, z] if half == 0 else [z, q_h], axis=0)


def _even_mix_kernel(running_max, sinks_ref, aq_ref, akp_ref, akc_ref, avp_ref, avc_ref,
                     bq_ref, bkp2_ref, bkp1_ref, bkc_ref, bvp2_ref, bvp1_ref, bvc_ref,
                     gate_ref, mask_ref, bias_ref, y_ref):
    ak = jnp.concatenate([akp_ref[0], akc_ref[0]], axis=0)
    av = jnp.concatenate([avp_ref[0], avc_ref[0]], axis=1)
    bk = jnp.concatenate([bkp2_ref[0], bkp1_ref[0], bkc_ref[0]], axis=0)
    bv = jnp.concatenate([bvp2_ref[0], bvp1_ref[0], bvc_ref[0]], axis=1)

    group = A_Q_HEADS // A_KV_HEADS
    heads = []
    for h in range(A_Q_HEADS):
        kv = h // group
        heads.append((ak, _pad_q(aq_ref[0, h * HEAD_DIM:(h + 1) * HEAD_DIM, :], kv),
                      av[kv * BAND_V_ROWS:(kv + 1) * BAND_V_ROWS, :], mask_ref[0], sinks_ref[h],
                      h * HEAD_DIM))
    for h in range(B_HEADS):
        pair = h // 2
        heads.append((bk[:, pair * 128:(pair + 1) * 128],
                      _pad_q(bq_ref[0, h * HEAD_DIM:(h + 1) * HEAD_DIM, :], h % 2),
                      bv[h * BAND_V_ROWS:(h + 1) * BAND_V_ROWS, :], bias_ref[0, h], None,
                      A_Q + h * HEAD_DIM))

    pending = []

    def emit_pv():
        v_t, p, sink_term, row0 = pending.pop(0)
        o = jnp.dot(v_t, p, preferred_element_type=F32)
        l = o[HEAD_DIM:HEAD_DIM + 1]
        if sink_term is not None:
            l = l + sink_term
        rows = slice(row0, row0 + HEAD_DIM)
        y_ref[0, rows, :] = (o[:HEAD_DIM] * (1.0 / l) * gate_ref[0, rows, :]).astype(BF16)

    for k_win, q_pad, v_t, add_tile, sink, row0 in heads:
        p, sink_term = _band_probs(running_max, k_win, q_pad, add_tile, sink)
        pending.append((v_t, p, sink_term, row0))
        if len(pending) > MIX_PV_LAG:
            emit_pv()
    while pending:
        emit_pv()


def _even_mix(running_max, sinks, aq_t, ak3, av_t, bq_t, bk3, bv_t, gate_t, mask_a, bias_b):
    batch, seq = ak3.shape[0], ak3.shape[1]
    tq = MIX_TOKENS
    nq = seq // tq
    assert tq % A_PREV_ROWS == 0 and B_PREV_ROWS == 2 * tq and seq % tq == 0
    a_ratio = tq // A_PREV_ROWS

    def cur_t(b, i, *_):
        return (b, 0, i)

    def cur_n(b, i, *_):
        return (b, i, 0)

    def a_prev_n(b, i, *_):
        return (b, jnp.maximum(i * a_ratio - 1, 0), 0)

    def a_prev_t(b, i, *_):
        return (b, 0, jnp.maximum(i * a_ratio - 1, 0))

    def b_prev_n(back):
        return lambda b, i, *_: (b, jnp.maximum(i - back, 0), 0)

    def b_prev_t(back):
        return lambda b, i, *_: (b, 0, jnp.maximum(i - back, 0))

    in_specs = [
        pl.BlockSpec((1, A_Q, tq), cur_t),
        pl.BlockSpec((1, A_PREV_ROWS, A_KV), a_prev_n),
        pl.BlockSpec((1, tq, A_KV), cur_n),
        pl.BlockSpec((1, A_V_ROWS, A_PREV_ROWS), a_prev_t),
        pl.BlockSpec((1, A_V_ROWS, tq), cur_t),
        pl.BlockSpec((1, B_W, tq), cur_t),
        pl.BlockSpec((1, tq, B_W), b_prev_n(2)),
        pl.BlockSpec((1, tq, B_W), b_prev_n(1)),
        pl.BlockSpec((1, tq, B_W), cur_n),
        pl.BlockSpec((1, B_V_ROWS, tq), b_prev_t(2)),
        pl.BlockSpec((1, B_V_ROWS, tq), b_prev_t(1)),
        pl.BlockSpec((1, B_V_ROWS, tq), cur_t),
        pl.BlockSpec((1, EVEN_MIX, tq), cur_t),
        pl.BlockSpec((1,) + mask_a.shape[1:],
                     lambda b, i, *_: (jnp.minimum(i, mask_a.shape[0] - 1), 0, 0)),
        pl.BlockSpec((1,) + bias_b.shape[1:],
                     lambda b, i, *_: (jnp.minimum(i, bias_b.shape[0] - 1), 0, 0, 0)),
    ]
    grid_spec = pltpu.PrefetchScalarGridSpec(
        num_scalar_prefetch=1,
        grid=(batch, nq),
        in_specs=in_specs,
        out_specs=pl.BlockSpec((1, EVEN_MIX, tq), cur_t),
    )
    return pl.pallas_call(
        functools.partial(_even_mix_kernel, running_max),
        grid_spec=grid_spec,
        out_shape=jax.ShapeDtypeStruct((batch, EVEN_MIX, seq), BF16),
        compiler_params=_params(("parallel", "parallel")),
        name="even_mix_online" if running_max else "even_mix_direct",
    )(sinks, aq_t, ak3, ak3, av_t, av_t, bq_t, bk3, bk3, bk3, bv_t, bv_t, bv_t,
      gate_t, mask_a, bias_b)


def _out_proj_kernel(x_ref, y_ref, w_t_ref, o_ref):
    out_t = jnp.dot(w_t_ref[...], y_ref[0], preferred_element_type=F32)
    o_ref[...] = (x_ref[0] + out_t).T


def _out_proj(x_t, y_t, w_t):
    batch, width, seq = y_t.shape
    tm = OUT_PROJ_TOKENS
    nt = seq // tm
    return pl.pallas_call(
        _out_proj_kernel,
        grid=(batch, nt),
        in_specs=[
            pl.BlockSpec((1, D_MODEL, tm), lambda b, i: (b, 0, i)),
            pl.BlockSpec((1, width, tm), lambda b, i: (b, 0, i)),
            pl.BlockSpec((D_MODEL, width), lambda b, i: (0, 0)),
        ],
        out_specs=pl.BlockSpec((tm, D_MODEL), lambda b, i: (b * nt + i, 0)),
        out_shape=jax.ShapeDtypeStruct((batch * seq, D_MODEL), F32),
        compiler_params=_params(("parallel", "parallel")),
        name="out_proj",
    )(x_t, y_t, w_t)


def _odd_proj_kernel(x_ref, y_ref, w_out_t_ref, g_ref, w_t_ref, cos_ref, sin_ref, qn_ref, kn_ref,
                     x1_ref, q_ref, k_ref, v_ref, gate_ref):
    out_t = jnp.dot(w_out_t_ref[...], y_ref[0], preferred_element_type=F32)
    x = x_ref[...].T + out_t
    x1_ref[0] = x
    ms = jnp.mean(x * x, axis=0, keepdims=True)
    h_t = (x * lax.rsqrt(ms + EPS) * g_ref[...]).astype(BF16)
    cos = cos_ref[...]
    sin = sin_ref[...]

    def proj(lo, width):
        return jnp.dot(w_t_ref[lo:lo + width, :], h_t, preferred_element_type=F32)

    k = proj(C_QK, C_QK)
    q = proj(0, C_QK)
    g = proj(2 * C_QK + C_W, C_W)
    v = proj(2 * C_QK, C_W)
    k = _head_norm_t(k, kn_ref[...], cos, sin)
    k_ref[...] = k.T.astype(BF16)
    q_ref[0] = _head_norm_t(q, qn_ref[...], cos, sin, Q_FOLD).astype(BF16)
    gate_ref[0] = _silu(g)
    v_ref[0] = v.astype(BF16)


def _odd_proj(x2d, y_t, w_out_t, gain_col, w_t, cos_t, sin_t, qn, kn):
    batch, mix_w, seq = y_t.shape
    tm = ODD_PROJ_TOKENS
    nt = seq // tm
    tokens = batch * seq
    odd_in = w_t.shape[0]
    resident = pl.Buffered(1)

    def tok(b, i):
        return (b * nt + i, 0)

    def feat(b, i):
        return (b, 0, i)

    def const2(b, i):
        return (0, 0)

    def tab(b, i):
        return (0, i)

    return pl.pallas_call(
        _odd_proj_kernel,
        grid=(batch, nt),
        in_specs=[
            pl.BlockSpec((tm, D_MODEL), tok),
            pl.BlockSpec((1, mix_w, tm), feat),
            pl.BlockSpec((D_MODEL, mix_w), const2, pipeline_mode=resident),
            pl.BlockSpec((D_MODEL, 1), const2),
            pl.BlockSpec((odd_in, D_MODEL), const2, pipeline_mode=resident),
            pl.BlockSpec((HALF_DIM, tm), tab),
            pl.BlockSpec((HALF_DIM, tm), tab),
            pl.BlockSpec((HEAD_DIM, tm), const2),
            pl.BlockSpec((HEAD_DIM, tm), const2),
        ],
        out_specs=[
            pl.BlockSpec((1, D_MODEL, tm), feat),
            pl.BlockSpec((1, C_QK, tm), feat),
            pl.BlockSpec((tm, C_QK), tok),
            pl.BlockSpec((1, C_W, tm), feat),
            pl.BlockSpec((1, C_W, tm), feat),
        ],
        out_shape=[
            jax.ShapeDtypeStruct((batch, D_MODEL, seq), F32),
            jax.ShapeDtypeStruct((batch, C_QK, seq), BF16),
            jax.ShapeDtypeStruct((tokens, C_QK), BF16),
            jax.ShapeDtypeStruct((batch, C_W, seq), BF16),
            jax.ShapeDtypeStruct((batch, C_W, seq), F32),
        ],
        compiler_params=_params(("parallel", "parallel")),
        name="odd_proj",
    )(x2d, y_t, w_out_t, gain_col, w_t, cos_t, sin_t, qn, kn)


def _diff_attn_kernel(lambda_init, running_max, q_ref, k_ref, v_ref, gate_ref, subln_ref,
                      lq1_ref, lk1_ref, lq2_ref, lk2_ref, y_ref, acc1_ref, acc2_ref):
    tq, tk = DIFF_Q, DIFF_K
    i = pl.program_id(2)
    q = q_ref[0]
    row = lax.broadcasted_iota(jnp.int32, q.shape, 0)
    zero = jnp.zeros_like(q)
    q_pads = (jnp.where(row < HEAD_DIM, q, zero), jnp.where(row >= HEAD_DIM, q, zero))
    acc_refs = (acc1_ref, acc2_ref)

    def load_kv(start, size):
        start = pl.multiple_of(start, size)
        return k_ref[0, pl.ds(start, size), :], v_ref[0, :, pl.ds(start, size)]

    def diag_valid(key0, size, n_cols):
        key_chunk = (lax.broadcasted_iota(jnp.int32, (size, n_cols), 0) + key0) // CHUNK
        qry_chunk = (lax.broadcasted_iota(jnp.int32, (size, n_cols), 1) + (tq - n_cols)) // CHUNK
        return key_chunk <= qry_chunk

    def online_step(k, v_t, carry, valid):
        new = []
        for c in range(2):
            m, l = carry[2 * c], carry[2 * c + 1]
            s = jnp.dot(k, q_pads[c], preferred_element_type=F32)
            if valid is not None:
                s = jnp.where(valid, s, NEG_INF)
            m_new = jnp.maximum(m, jnp.max(s, axis=0, keepdims=True))
            alpha = jnp.exp2(m - m_new)
            p = jnp.exp2(s - m_new)
            l_new = alpha * l + jnp.sum(p, axis=0, keepdims=True)
            pv = jnp.dot(v_t, p.astype(BF16), preferred_element_type=F32)
            acc_refs[c][...] = alpha * acc_refs[c][...] + pv
            new += [m_new, l_new]
        return tuple(new)

    def direct_probs(k, q_c, valid):
        p = jnp.exp2(jnp.dot(k, q_c, preferred_element_type=F32))
        if valid is not None:
            p = jnp.where(valid, p, 0.0)
        return p.astype(BF16), jnp.sum(p, axis=0, keepdims=True)

    def direct_blocks(blocks, l_sums, pv_lag):
        l_sums = list(l_sums)
        pv_sum = {}
        pending = []

        edges = sorted({col0 for _, _, col0, _ in blocks}) + [tq]

        def emit_pv():
            v_t, c, col0, p = pending.pop(0)
            pv = jnp.dot(v_t, p, preferred_element_type=F32)
            for lo, hi in zip(edges[:-1], edges[1:]):
                if lo >= col0:
                    part = pv[:, lo - col0:hi - col0]
                    pv_sum[(c, lo, hi)] = pv_sum[(c, lo, hi)] + part if (c, lo, hi) in pv_sum else part

        for k, v_t, col0, valid in blocks:
            for c in range(2):
                p, total = direct_probs(k, q_pads[c][:, col0:], valid)
                if col0:
                    total = jnp.concatenate([jnp.zeros((1, col0), F32), total], axis=1)
                l_sums[c] = l_sums[c] + total
                pending.append((v_t, c, col0, p))
                if len(pending) > pv_lag:
                    emit_pv()
        while pending:
            emit_pv()
        for (c, lo, hi), pv in pv_sum.items():
            acc_refs[c][:, lo:hi] += pv
        return tuple(l_sums)

    q0 = i * tq
    acc1_ref[...] = jnp.zeros_like(acc1_ref)
    acc2_ref[...] = jnp.zeros_like(acc2_ref)
    if running_max:
        init = (jnp.full((1, tq), NEG_INF, F32), jnp.zeros((1, tq), F32)) * 2
        carry = lax.fori_loop(0, q0 // tk,
                              lambda j, c: online_step(*load_kv(j * tk, tk), c, None), init)
        for key0 in range(0, tq, tk):
            carry = online_step(*load_kv(q0 + key0, tk), carry, diag_valid(key0, tk, tq))
        l1, l2 = carry[1], carry[3]
    else:
        def group(t, l_sums):
            return direct_blocks(
                [load_kv(t * tq + u * tk, tk) + (0, None) for u in range(tq // tk)], l_sums,
                pv_lag=DIFF_PV_LAG)

        l_sums = lax.fori_loop(0, i, group, (jnp.zeros((1, tq), F32),) * 2)
        td = DIFF_DIAG_K
        l1, l2 = direct_blocks(
            [load_kv(q0 + key0, td) + (key0, diag_valid(key0, td, tq - key0))
             for key0 in range(0, tq, td)], l_sums, pv_lag=1)

    lam = (jnp.exp(jnp.sum(lq1_ref[...] * lk1_ref[...], axis=-1, keepdims=True))
           - jnp.exp(jnp.sum(lq2_ref[...] * lk2_ref[...], axis=-1, keepdims=True))
           + lambda_init)
    o = acc1_ref[...] * (1.0 / l1) - lam * (acc2_ref[...] * (1.0 / l2))
    ms = jnp.mean(o * o, axis=0, keepdims=True)
    o = o * lax.rsqrt(ms + EPS) * subln_ref[...] * (1.0 - lambda_init)
    y_ref[0] = (o * gate_ref[0]).astype(BF16)


def _diff_attn(running_max, q_t, k3, v_t, gate_t, subln_tile, lq1, lk1, lq2, lk2, lambda_init):
    batch, seq = k3.shape[0], k3.shape[1]
    tq = DIFF_Q
    nq = seq // tq
    head_w = 2 * HEAD_DIM

    def blk(b, h, i):
        return (b, h, i)

    def const2(b, h, i):
        return (0, 0)

    return pl.pallas_call(
        functools.partial(_diff_attn_kernel, lambda_init, running_max),
        grid=(batch, C_HEADS, nq),
        in_specs=[
            pl.BlockSpec((1, head_w, tq), blk),
            pl.BlockSpec((1, seq, head_w), lambda b, h, i: (b, 0, h)),
            pl.BlockSpec((1, C_V_DIM, seq), lambda b, h, i: (b, h, 0)),
            pl.BlockSpec((1, C_V_DIM, tq), blk),
            pl.BlockSpec((C_V_DIM, tq), const2),
            pl.BlockSpec((1, HEAD_DIM), const2),
            pl.BlockSpec((1, HEAD_DIM), const2),
            pl.BlockSpec((1, HEAD_DIM), const2),
            pl.BlockSpec((1, HEAD_DIM), const2),
        ],
        out_specs=pl.BlockSpec((1, C_V_DIM, tq), blk),
        out_shape=jax.ShapeDtypeStruct((batch, C_W, seq), BF16),
        scratch_shapes=[pltpu.VMEM((C_V_DIM, tq), F32), pltpu.VMEM((C_V_DIM, tq), F32)],
        compiler_params=_params(("parallel", "parallel", "arbitrary")),
        name="diff_attn_online" if running_max else "diff_attn_direct",
    )(q_t, k3, v_t, gate_t, subln_tile, lq1, lk1, lq2, lk2)


def _qk_score_bound(q_gain, k_gain):
    bound = HEAD_DIM * QK_SCALE * jnp.max(jnp.abs(q_gain)) * jnp.max(jnp.abs(k_gain))
    return bound.astype(F32) * (1.0 + 2.0 ** -6)


def _rope_tables_t(seq):
    inv = 1.0 / (ROPE_THETA ** (jnp.arange(0, HEAD_DIM, 2, dtype=F32) / HEAD_DIM))
    ang = inv[:, None] * jnp.arange(seq, dtype=F32)[None, :]
    return jnp.cos(ang), jnp.sin(ang)


def _gain_tile(g, width):
    return jnp.broadcast_to(g.astype(F32)[:, None], (g.shape[0], width))


def _rel_bias_t(rel_table, n_q):
    n_k = B_PREV_ROWS + n_q
    width = n_k + n_q
    offset = np.arange(width) - (n_k - 1)
    rel = np.clip(B_PREV_ROWS + offset, -B_MAX_REL, B_MAX_REL) + B_MAX_REL
    by_offset = rel_table.astype(F32)[:, rel]
    heads = rel_table.shape[0]
    sub = 8

    def build(f_ref, o_ref):
        for h in range(heads):
            f = f_ref[h:h + 1, :]
            rows = jnp.concatenate(
                [pltpu.roll(f, (width + b - (sub - 1)) % width, axis=1) for b in range(sub)], axis=0)
            for blk in range(n_k // sub):
                lead = n_k - sub - sub * blk
                o_ref[h, sub * blk:sub * (blk + 1), :] = pltpu.roll(
                    rows, (width - lead) % width, axis=1)[:, :n_q]

    return pl.pallas_call(
        build,
        out_shape=jax.ShapeDtypeStruct((heads, n_k, n_q), F32),
        name="rel_bias_tile",
    )(by_offset)


def kernel(x, ev_norm, ev_w_in, ev_w_out, ev_a_q_norm, ev_a_k_norm, ev_a_sinks,
           ev_b_q_norm, ev_b_k_norm, ev_b_rel_bias, od_norm, od_w_in, od_w_out,
           od_q_norm, od_k_norm, od_lambda_q1, od_lambda_k1, od_lambda_q2,
           od_lambda_k2, od_subln):
    batch, seq, d = x.shape
    cos_t, sin_t = _rope_tables_t(seq)
    depth = ev_norm.shape[0] + od_norm.shape[0]
    assert ev_norm.shape[0] == od_norm.shape[0]
    x2d = x.reshape(batch * seq, d)
    for layer in range(depth):
        i = layer // 2
        if layer % 2 == 0:
            w_in_t = ev_w_in[i].T.astype(BF16)
            w_out_t = ev_w_out[i].T.astype(BF16)
            aq_t, ak, av_t, bq_t, bk, bv_t, gate_t = _even_proj(
                x2d, ev_norm[i][None, :], w_in_t, cos_t, sin_t,
                *[_gain_tile(g[i], EVEN_PROJ_TOKENS)
                  for g in (ev_a_q_norm, ev_a_k_norm, ev_b_q_norm, ev_b_k_norm)],
                batch, seq)
            mask_a = jnp.where(_band_mask_variants(A_PREV_ROWS, MIX_TOKENS), 0.0, NEG_INF).astype(F32)
            bias_b = jnp.where(_band_mask_variants(B_PREV_ROWS, MIX_TOKENS)[:, None],
                               _rel_bias_t(ev_b_rel_bias[i] * LOG2E, MIX_TOKENS)[None], NEG_INF)
            sinks = ev_a_sinks[i].astype(F32)
            mix_args = (sinks * LOG2E, aq_t, ak.reshape(batch, seq, A_KV), av_t,
                        bq_t, bk.reshape(batch, seq, B_W), bv_t, gate_t, mask_a, bias_b)
            bound_a = jnp.maximum(_qk_score_bound(ev_a_q_norm[i], ev_a_k_norm[i]),
                                  jnp.max(jnp.abs(sinks)))
            bound_b = (_qk_score_bound(ev_b_q_norm[i], ev_b_k_norm[i])
                       + jnp.max(jnp.abs(ev_b_rel_bias[i])))
            y_t = lax.cond(jnp.maximum(bound_a, bound_b) <= SAFE_SCORE,
                           functools.partial(_even_mix, False),
                           functools.partial(_even_mix, True), *mix_args)
            even_w_out_t = w_out_t
        else:
            lambda_init = 0.8 - 0.6 * math.exp(-0.3 * layer)
            w_in_t = od_w_in[i].T.astype(BF16)
            w_out_t = od_w_out[i].T.astype(BF16)
            x_t, q_t, k, v_t, gate_t = _odd_proj(
                x2d, y_t, even_w_out_t, od_norm[i][:, None], w_in_t, cos_t, sin_t,
                _gain_tile(od_q_norm[i], ODD_PROJ_TOKENS), _gain_tile(od_k_norm[i], ODD_PROJ_TOKENS))
            diff_args = (q_t, k.reshape(batch, seq, C_QK), v_t, gate_t,
                         _gain_tile(od_subln[i], DIFF_Q),
                         od_lambda_q1[i][None, :], od_lambda_k1[i][None, :],
                         od_lambda_q2[i][None, :], od_lambda_k2[i][None, :])
            y_t = lax.cond(
                _qk_score_bound(od_q_norm[i], od_k_norm[i]) <= SAFE_SCORE,
                lambda *a: _diff_attn(False, *a, lambda_init),
                lambda *a: _diff_attn(True, *a, lambda_init),
                *diff_args)
            x2d = _out_proj(x_t, y_t, w_out_t)
    return x2d.reshape(batch, seq, d)
```

```python
import functools
import math

import jax
import jax.numpy as jnp
import numpy as np
from jax import lax
from jax.experimental import pallas as pl
from jax.experimental.pallas import tpu as pltpu

F32 = jnp.float32
BF16 = jnp.bfloat16

D_MODEL = 1024
CHUNK = 64
HEAD_DIM = 64
HALF_DIM = HEAD_DIM // 2
ROPE_THETA = 10000.0
EPS = 1e-6
NEG_INF = -1e30
QK_SCALE = 1.0 / math.sqrt(HEAD_DIM)
LOG2E = math.log2(math.e)
Q_FOLD = QK_SCALE * LOG2E

A_Q_HEADS = 8
A_KV_HEADS = 2
A_PREV_CHUNKS = 2
B_HEADS = 8
B_PREV_CHUNKS = 8
B_MAX_REL = 128
C_HEADS = 8
C_V_DIM = 2 * HEAD_DIM

A_Q = A_Q_HEADS * HEAD_DIM
A_KV = A_KV_HEADS * HEAD_DIM
B_W = B_HEADS * HEAD_DIM
EVEN_MIX = A_Q + B_W
C_QK = C_HEADS * 2 * HEAD_DIM
C_W = C_HEADS * C_V_DIM
PV_SUM_ROWS = 16
BAND_V_ROWS = HEAD_DIM + PV_SUM_ROWS
A_V_ROWS = A_KV_HEADS * BAND_V_ROWS
B_V_ROWS = B_HEADS * BAND_V_ROWS

V7X_VMEM_LIMIT_BYTES = 56 * 1024 * 1024

EVEN_PROJ_TOKENS = 1024
ODD_PROJ_TOKENS = 512
OUT_PROJ_TOKENS = 2048
MIX_TOKENS = 256
MIX_PV_LAG = 4
A_PREV_ROWS = A_PREV_CHUNKS * CHUNK
B_PREV_ROWS = B_PREV_CHUNKS * CHUNK
DIFF_Q = 2048
DIFF_K = 512
DIFF_DIAG_K = 256
DIFF_PV_LAG = 3
SAFE_SCORE = 40.0


def _params(semantics):
    return pltpu.CompilerParams(dimension_semantics=semantics,
                                vmem_limit_bytes=V7X_VMEM_LIMIT_BYTES)


def _rms_rows(x, gain_row):
    ms = jnp.mean(x * x, axis=-1, keepdims=True)
    return x * lax.rsqrt(ms + EPS) * gain_row


def _proj_t(w_t_ref, lo, width, h):
    return lax.dot_general(w_t_ref[lo:lo + width, :], h,
                           (((1,), (1,)), ((), ())),
                           preferred_element_type=F32)


def _head_norm_t(p, gain_tile, cos=None, sin=None, scale=1.0):
    width, tm = p.shape
    y = p.reshape(width // HEAD_DIM, HEAD_DIM, tm)
    ms = jnp.mean(y * y, axis=1, keepdims=True)
    y = y * lax.rsqrt(ms + EPS) * gain_tile[None]
    if cos is not None:
        y1 = y[:, :HALF_DIM]
        y2 = y[:, HALF_DIM:]
        c = cos[None]
        s = sin[None]
        y = jnp.concatenate([y1 * c - y2 * s, y2 * c + y1 * s], axis=1)
    if scale != 1.0:
        y = y * scale
    return y.reshape(width, tm)


def _silu(x):
    return x * (1.0 / (1.0 + jnp.exp(-x)))


def _even_proj_kernel(x_ref, g_ref, w_t_ref, cos_ref, sin_ref,
                      aqn_ref, akn_ref, bqn_ref, bkn_ref,
                      aq_ref, ak_ref, av_ref, bq_ref, bk_ref, bv_ref, gate_ref):
    h = _rms_rows(x_ref[...], g_ref[...]).astype(BF16)
    cos = cos_ref[...]
    sin = sin_ref[...]
    a_w = 2 * A_Q + 2 * A_KV
    a_all = _proj_t(w_t_ref, 0, a_w, h)
    b_qk = _proj_t(w_t_ref, a_w, 2 * B_W, h)
    b_vg = _proj_t(w_t_ref, a_w + 2 * B_W, 2 * B_W, h)
    ak = _head_norm_t(a_all[A_Q:A_Q + A_KV], akn_ref[...], cos, sin)
    ak_ref[...] = ak.T.astype(BF16)
    aq_ref[0] = _head_norm_t(a_all[:A_Q], aqn_ref[...], cos, sin, Q_FOLD).astype(BF16)
    _store_v_with_ones(av_ref, a_all[A_Q + A_KV:A_Q + 2 * A_KV])
    gate_ref[0, :A_Q, :] = _silu(a_all[A_Q + 2 * A_KV:])
    bk = _head_norm_t(b_qk[B_W:], bkn_ref[...])
    bk_ref[...] = bk.T.astype(BF16)
    bq_ref[0] = _head_norm_t(b_qk[:B_W], bqn_ref[...], scale=Q_FOLD).astype(BF16)
    gate_ref[0, A_Q:, :] = _silu(b_vg[B_W:])
    _store_v_with_ones(bv_ref, b_vg[:B_W])


def _store_v_with_ones(v_ref, v):
    tm = v.shape[1]
    ones = jnp.ones((PV_SUM_ROWS, tm), BF16)
    for h in range(v.shape[0] // HEAD_DIM):
        v_ref[0, h * BAND_V_ROWS:h * BAND_V_ROWS + HEAD_DIM, :] = (
            v[h * HEAD_DIM:(h + 1) * HEAD_DIM].astype(BF16))
        v_ref[0, h * BAND_V_ROWS + HEAD_DIM:(h + 1) * BAND_V_ROWS, :] = ones


def _even_proj(x2d, gain, w_t, cos_t, sin_t, aqn, akn, bqn, bkn, batch, seq):
    tm = EVEN_PROJ_TOKENS
    nt = seq // tm
    tokens = batch * seq
    even_in = w_t.shape[0]

    def tok(b, i):
        return (b * nt + i, 0)

    def feat(b, i):
        return (b, 0, i)

    def const2(b, i):
        return (0, 0)

    def tab(b, i):
        return (0, i)

    in_specs = [
        pl.BlockSpec((tm, D_MODEL), tok),
        pl.BlockSpec((1, D_MODEL), const2),
        pl.BlockSpec((even_in, D_MODEL), const2, pipeline_mode=pl.Buffered(1)),
        pl.BlockSpec((HALF_DIM, tm), tab),
        pl.BlockSpec((HALF_DIM, tm), tab),
        pl.BlockSpec((HEAD_DIM, tm), const2),
        pl.BlockSpec((HEAD_DIM, tm), const2),
        pl.BlockSpec((HEAD_DIM, tm), const2),
        pl.BlockSpec((HEAD_DIM, tm), const2),
    ]
    out_shape = [
        jax.ShapeDtypeStruct((batch, A_Q, seq), BF16),
        jax.ShapeDtypeStruct((tokens, A_KV), BF16),
        jax.ShapeDtypeStruct((batch, A_V_ROWS, seq), BF16),
        jax.ShapeDtypeStruct((batch, B_W, seq), BF16),
        jax.ShapeDtypeStruct((tokens, B_W), BF16),
        jax.ShapeDtypeStruct((batch, B_V_ROWS, seq), BF16),
        jax.ShapeDtypeStruct((batch, EVEN_MIX, seq), F32),
    ]
    out_specs = [
        pl.BlockSpec((1, A_Q, tm), feat),
        pl.BlockSpec((tm, A_KV), tok),
        pl.BlockSpec((1, A_V_ROWS, tm), feat),
        pl.BlockSpec((1, B_W, tm), feat),
        pl.BlockSpec((tm, B_W), tok),
        pl.BlockSpec((1, B_V_ROWS, tm), feat),
        pl.BlockSpec((1, EVEN_MIX, tm), feat),
    ]
    return pl.pallas_call(
        _even_proj_kernel,
        grid=(batch, nt),
        in_specs=in_specs,
        out_specs=out_specs,
        out_shape=out_shape,
        compiler_params=_params(("parallel", "parallel")),
        name="even_proj",
    )(x2d, gain, w_t, cos_t, sin_t, aqn, akn, bqn, bkn)


def _band_mask_np(prev_rows, n_q, q_block):
    key = np.arange(prev_rows + n_q)[:, None]
    qry = np.arange(n_q)[None, :] + prev_rows
    return ((key // CHUNK <= qry // CHUNK)
            & (key // CHUNK >= qry // CHUNK - prev_rows // CHUNK)
            & (key >= prev_rows - q_block * n_q))


def _band_mask_variants(prev_rows, n_q):
    n = -(-prev_rows // n_q) + 1
    return np.stack([_band_mask_np(prev_rows, n_q, v) for v in range(n)])


def _band_probs(running_max, k_win, q_pad, add_tile, sink):
    s = jnp.dot(k_win, q_pad, preferred_element_type=F32) + add_tile
    if running_max:
        m = jnp.max(s, axis=0, keepdims=True)
        if sink is not None:
            m = jnp.maximum(m, sink)
        s = s - m
        sink = None if sink is None else sink - m
    sink_term = None if sink is None else jnp.exp2(jnp.broadcast_to(sink, (1, s.shape[1])))
    return jnp.exp2(s).astype(BF16), sink_term


def _pad_q(q_h, half):
    z = jnp.zeros_like(q_h)
    return jnp.concatenate([q_h, z] if half == 0 else [z, q_h], axis=0)


def _even_mix_kernel(running_max, sinks_ref, aq_ref, akp_ref, akc_ref, avp_ref, avc_ref,
                     bq_ref, bkp2_ref, bkp1_ref, bkc_ref, bvp2_ref, bvp1_ref, bvc_ref,
                     gate_ref, mask_ref, bias_ref, y_ref):
    ak = jnp.concatenate([akp_ref[0], akc_ref[0]], axis=0)
    av = jnp.concatenate([avp_ref[0], avc_ref[0]], axis=1)
    bk = jnp.concatenate([bkp2_ref[0], bkp1_ref[0], bkc_ref[0]], axis=0)
    bv = jnp.concatenate([bvp2_ref[0], bvp1_ref[0], bvc_ref[0]], axis=1)

    group = A_Q_HEADS // A_KV_HEADS
    heads = []
    for h in range(A_Q_HEADS):
        kv = h // group
        heads.append((ak, _pad_q(aq_ref[0, h * HEAD_DIM:(h + 1) * HEAD_DIM, :], kv),
                      av[kv * BAND_V_ROWS:(kv + 1) * BAND_V_ROWS, :], mask_ref[0], sinks_ref[h],
                      h * HEAD_DIM))
    for h in range(B_HEADS):
        pair = h // 2
        heads.append((bk[:, pair * 128:(pair + 1) * 128],
                      _pad_q(bq_ref[0, h * HEAD_DIM:(h + 1) * HEAD_DIM, :], h % 2),
                      bv[h * BAND_V_ROWS:(h + 1) * BAND_V_ROWS, :], bias_ref[0, h], None,
                      A_Q + h * HEAD_DIM))

    pending = []

    def emit_pv():
        v_t, p, sink_term, row0 = pending.pop(0)
        o = jnp.dot(v_t, p, preferred_element_type=F32)
        l = o[HEAD_DIM:HEAD_DIM + 1]
        if sink_term is not None:
            l = l + sink_term
        rows = slice(row0, row0 + HEAD_DIM)
        y_ref[0, rows, :] = (o[:HEAD_DIM] * (1.0 / l) * gate_ref[0, rows, :]).astype(BF16)

    for k_win, q_pad, v_t, add_tile, sink, row0 in heads:
        p, sink_term = _band_probs(running_max, k_win, q_pad, add_tile, sink)
        pending.append((v_t, p, sink_term, row0))
        if len(pending) > MIX_PV_LAG:
            emit_pv()
    while pending:
        emit_pv()


def _even_mix(running_max, sinks, aq_t, ak3, av_t, bq_t, bk3, bv_t, gate_t, mask_a, bias_b):
    batch, seq = ak3.shape[0], ak3.shape[1]
    tq = MIX_TOKENS
    nq = seq // tq
    assert tq % A_PREV_ROWS == 0 and B_PREV_ROWS == 2 * tq and seq % tq == 0
    a_ratio = tq // A_PREV_ROWS

    def cur_t(b, i, *_):
        return (b, 0, i)

    def cur_n(b, i, *_):
        return (b, i, 0)

    def a_prev_n(b, i, *_):
        return (b, jnp.maximum(i * a_ratio - 1, 0), 0)

    def a_prev_t(b, i, *_):
        return (b, 0, jnp.maximum(i * a_ratio - 1, 0))

    def b_prev_n(back):
        return lambda b, i, *_: (b, jnp.maximum(i - back, 0), 0)

    def b_prev_t(back):
        return lambda b, i, *_: (b, 0, jnp.maximum(i - back, 0))

    in_specs = [
        pl.BlockSpec((1, A_Q, tq), cur_t),
        pl.BlockSpec((1, A_PREV_ROWS, A_KV), a_prev_n),
        pl.BlockSpec((1, tq, A_KV), cur_n),
        pl.BlockSpec((1, A_V_ROWS, A_PREV_ROWS), a_prev_t),
        pl.BlockSpec((1, A_V_ROWS, tq), cur_t),
        pl.BlockSpec((1, B_W, tq), cur_t),
        pl.BlockSpec((1, tq, B_W), b_prev_n(2)),
        pl.BlockSpec((1, tq, B_W), b_prev_n(1)),
        pl.BlockSpec((1, tq, B_W), cur_n),
        pl.BlockSpec((1, B_V_ROWS, tq), b_prev_t(2)),
        pl.BlockSpec((1, B_V_ROWS, tq), b_prev_t(1)),
        pl.BlockSpec((1, B_V_ROWS, tq), cur_t),
        pl.BlockSpec((1, EVEN_MIX, tq), cur_t),
        pl.BlockSpec((1,) + mask_a.shape[1:],
                     lambda b, i, *_: (jnp.minimum(i, mask_a.shape[0] - 1), 0, 0)),
        pl.BlockSpec((1,) + bias_b.shape[1:],
                     lambda b, i, *_: (jnp.minimum(i, bias_b.shape[0] - 1), 0, 0, 0)),
    ]
    grid_spec = pltpu.PrefetchScalarGridSpec(
        num_scalar_prefetch=1,
        grid=(batch, nq),
        in_specs=in_specs,
        out_specs=pl.BlockSpec((1, EVEN_MIX, tq), cur_t),
    )
    return pl.pallas_call(
        functools.partial(_even_mix_kernel, running_max),
        grid_spec=grid_spec,
        out_shape=jax.ShapeDtypeStruct((batch, EVEN_MIX, seq), BF16),
        compiler_params=_params(("parallel", "parallel")),
        name="even_mix_online" if running_max else "even_mix_direct",
    )(sinks, aq_t, ak3, ak3, av_t, av_t, bq_t, bk3, bk3, bk3, bv_t, bv_t, bv_t,
      gate_t, mask_a, bias_b)


def _out_proj_kernel(x_ref, y_ref, w_t_ref, o_ref):
    out_t = jnp.dot(w_t_ref[...], y_ref[0], preferred_element_type=F32)
    o_ref[...] = (x_ref[0] + out_t).T


def _out_proj(x_t, y_t, w_t):
    batch, width, seq = y_t.shape
    tm = OUT_PROJ_TOKENS
    nt = seq // tm
    return pl.pallas_call(
        _out_proj_kernel,
        grid=(batch, nt),
        in_specs=[
            pl.BlockSpec((1, D_MODEL, tm), lambda b, i: (b, 0, i)),
            pl.BlockSpec((1, width, tm), lambda b, i: (b, 0, i)),
            pl.BlockSpec((D_MODEL, width), lambda b, i: (0, 0)),
        ],
        out_specs=pl.BlockSpec((tm, D_MODEL), lambda b, i: (b * nt + i, 0)),
        out_shape=jax.ShapeDtypeStruct((batch * seq, D_MODEL), F32),
        compiler_params=_params(("parallel", "parallel")),
        name="out_proj",
    )(x_t, y_t, w_t)


def _odd_proj_kernel(x_ref, y_ref, w_out_t_ref, g_ref, w_t_ref, cos_ref, sin_ref, qn_ref, kn_ref,
                     x1_ref, q_ref, k_ref, v_ref, gate_ref):
    out_t = jnp.dot(w_out_t_ref[...], y_ref[0], preferred_element_type=F32)
    x = x_ref[...].T + out_t
    x1_ref[0] = x
    ms = jnp.mean(x * x, axis=0, keepdims=True)
    h_t = (x * lax.rsqrt(ms + EPS) * g_ref[...]).astype(BF16)
    cos = cos_ref[...]
    sin = sin_ref[...]

    def proj(lo, width):
        return jnp.dot(w_t_ref[lo:lo + width, :], h_t, preferred_element_type=F32)

    k = proj(C_QK, C_QK)
    q = proj(0, C_QK)
    g = proj(2 * C_QK + C_W, C_W)
    v = proj(2 * C_QK, C_W)
    k = _head_norm_t(k, kn_ref[...], cos, sin)
    k_ref[...] = k.T.astype(BF16)
    q_ref[0] = _head_norm_t(q, qn_ref[...], cos, sin, Q_FOLD).astype(BF16)
    gate_ref[0] = _silu(g)
    v_ref[0] = v.astype(BF16)


def _odd_proj(x2d, y_t, w_out_t, gain_col, w_t, cos_t, sin_t, qn, kn):
    batch, mix_w, seq = y_t.shape
    tm = ODD_PROJ_TOKENS
    nt = seq // tm
    tokens = batch * seq
    odd_in = w_t.shape[0]
    resident = pl.Buffered(1)

    def tok(b, i):
        return (b * nt + i, 0)

    def feat(b, i):
        return (b, 0, i)

    def const2(b, i):
        return (0, 0)

    def tab(b, i):
        return (0, i)

    return pl.pallas_call(
        _odd_proj_kernel,
        grid=(batch, nt),
        in_specs=[
            pl.BlockSpec((tm, D_MODEL), tok),
            pl.BlockSpec((1, mix_w, tm), feat),
            pl.BlockSpec((D_MODEL, mix_w), const2, pipeline_mode=resident),
            pl.BlockSpec((D_MODEL, 1), const2),
            pl.BlockSpec((odd_in, D_MODEL), const2, pipeline_mode=resident),
            pl.BlockSpec((HALF_DIM, tm), tab),
            pl.BlockSpec((HALF_DIM, tm), tab),
            pl.BlockSpec((HEAD_DIM, tm), const2),
            pl.BlockSpec((HEAD_DIM, tm), const2),
        ],
        out_specs=[
            pl.BlockSpec((1, D_MODEL, tm), feat),
            pl.BlockSpec((1, C_QK, tm), feat),
            pl.BlockSpec((tm, C_QK), tok),
            pl.BlockSpec((1, C_W, tm), feat),
            pl.BlockSpec((1, C_W, tm), feat),
        ],
        out_shape=[
            jax.ShapeDtypeStruct((batch, D_MODEL, seq), F32),
            jax.ShapeDtypeStruct((batch, C_QK, seq), BF16),
            jax.ShapeDtypeStruct((tokens, C_QK), BF16),
            jax.ShapeDtypeStruct((batch, C_W, seq), BF16),
            jax.ShapeDtypeStruct((batch, C_W, seq), F32),
        ],
        compiler_params=_params(("parallel", "parallel")),
        name="odd_proj",
    )(x2d, y_t, w_out_t, gain_col, w_t, cos_t, sin_t, qn, kn)


def _diff_attn_kernel(lambda_init, running_max, q_ref, k_ref, v_ref, gate_ref, subln_ref,
                      lq1_ref, lk1_ref, lq2_ref, lk2_ref, y_ref, acc1_ref, acc2_ref):
    tq, tk = DIFF_Q, DIFF_K
    i = pl.program_id(2)
    q = q_ref[0]
    row = lax.broadcasted_iota(jnp.int32, q.shape, 0)
    zero = jnp.zeros_like(q)
    q_pads = (jnp.where(row < HEAD_DIM, q, zero), jnp.where(row >= HEAD_DIM, q, zero))
    acc_refs = (acc1_ref, acc2_ref)

    def load_kv(start, size):
        start = pl.multiple_of(start, size)
        return k_ref[0, pl.ds(start, size), :], v_ref[0, :, pl.ds(start, size)]

    def diag_valid(key0, size, n_cols):
        key_chunk = (lax.broadcasted_iota(jnp.int32, (size, n_cols), 0) + key0) // CHUNK
        qry_chunk = (lax.broadcasted_iota(jnp.int32, (size, n_cols), 1) + (tq - n_cols)) // CHUNK
        return key_chunk <= qry_chunk

    def online_step(k, v_t, carry, valid):
        new = []
        for c in range(2):
            m, l = carry[2 * c], carry[2 * c + 1]
            s = jnp.dot(k, q_pads[c], preferred_element_type=F32)
            if valid is not None:
                s = jnp.where(valid, s, NEG_INF)
            m_new = jnp.maximum(m, jnp.max(s, axis=0, keepdims=True))
            alpha = jnp.exp2(m - m_new)
            p = jnp.exp2(s - m_new)
            l_new = alpha * l + jnp.sum(p, axis=0, keepdims=True)
            pv = jnp.dot(v_t, p.astype(BF16), preferred_element_type=F32)
            acc_refs[c][...] = alpha * acc_refs[c][...] + pv
            new += [m_new, l_new]
        return tuple(new)

    def direct_probs(k, q_c, valid):
        p = jnp.exp2(jnp.dot(k, q_c, preferred_element_type=F32))
        if valid is not None:
            p = jnp.where(valid, p, 0.0)
        return p.astype(BF16), jnp.sum(p, axis=0, keepdims=True)

    def emit_lagged(tiles, pv_lag, on_pv):
        pending = []

        def emit_pv():
            tag, v_t, p, total = pending.pop(0)
            on_pv(tag, total, jnp.dot(v_t, p, preferred_element_type=F32))

        for tag, k, v_t, q_c, valid in tiles:
            p, total = direct_probs(k, q_c, valid)
            pending.append((tag, v_t, p, total))
            if len(pending) > pv_lag:
                emit_pv()
        while pending:
            emit_pv()

    lam = (jnp.exp(jnp.sum(lq1_ref[...] * lk1_ref[...], axis=-1, keepdims=True))
           - jnp.exp(jnp.sum(lq2_ref[...] * lk2_ref[...], axis=-1, keepdims=True))
           + lambda_init)

    def finish(cols, acc1, acc2, l1, l2):
        o = acc1 * (1.0 / l1) - lam * (acc2 * (1.0 / l2))
        ms = jnp.mean(o * o, axis=0, keepdims=True)
        o = o * lax.rsqrt(ms + EPS) * subln_ref[:, cols] * (1.0 - lambda_init)
        y_ref[0, :, cols] = (o * gate_ref[0, :, cols]).astype(BF16)

    q0 = i * tq
    acc1_ref[...] = jnp.zeros_like(acc1_ref)
    acc2_ref[...] = jnp.zeros_like(acc2_ref)
    if running_max:
        init = (jnp.full((1, tq), NEG_INF, F32), jnp.zeros((1, tq), F32)) * 2
        carry = lax.fori_loop(0, q0 // tk,
                              lambda j, c: online_step(*load_kv(j * tk, tk), c, None), init)
        for key0 in range(0, tq, tk):
            carry = online_step(*load_kv(q0 + key0, tk), carry, diag_valid(key0, tk, tq))
        finish(slice(0, tq), acc1_ref[...], acc2_ref[...], carry[1], carry[3])
    else:
        def group(t, l_sums):
            l_sums = list(l_sums)
            pv_sums = [0.0, 0.0]

            def on_pv(c, total, pv):
                l_sums[c] = l_sums[c] + total
                pv_sums[c] = pv_sums[c] + pv

            kv = [load_kv(t * tq + u * tk, tk) for u in range(tq // tk)]
            emit_lagged([(c, k, v_t, q_pads[c], None) for k, v_t in kv for c in range(2)],
                        DIFF_PV_LAG, on_pv)
            for c in range(2):
                acc_refs[c][...] += pv_sums[c]
            return tuple(l_sums)

        l_sums = lax.fori_loop(0, i, group, (jnp.zeros((1, tq), F32),) * 2)

        td = DIFF_DIAG_K
        n_seg = tq // td
        parts = [[[] for _ in range(n_seg)] for _ in range(2)]

        def on_diag_pv(tag, total, pv):
            b, c = tag
            for s in range(b, n_seg):
                seg = slice((s - b) * td, (s - b + 1) * td)
                parts[c][s].append((total[:, seg], pv[:, seg]))
            if c == 1:
                cols = slice(b * td, (b + 1) * td)
                sums = [(l_sums[cc][:, cols] + sum(t for t, _ in parts[cc][b]),
                         acc_refs[cc][:, cols] + sum(p for _, p in parts[cc][b]))
                        for cc in range(2)]
                finish(cols, sums[0][1], sums[1][1], sums[0][0], sums[1][0])

        diag_tiles = []
        for b in range(n_seg):
            k, v_t = load_kv(q0 + b * td, td)
            valid = diag_valid(b * td, td, tq - b * td)
            diag_tiles += [((b, c), k, v_t, q_pads[c][:, b * td:], valid) for c in range(2)]
        emit_lagged(diag_tiles, 1, on_diag_pv)


def _diff_attn(running_max, q_t, k3, v_t, gate_t, subln_tile, lq1, lk1, lq2, lk2, lambda_init):
    batch, seq = k3.shape[0], k3.shape[1]
    tq = DIFF_Q
    nq = seq // tq
    head_w = 2 * HEAD_DIM

    def blk(b, h, i):
        return (b, h, i)

    def const2(b, h, i):
        return (0, 0)

    return pl.pallas_call(
        functools.partial(_diff_attn_kernel, lambda_init, running_max),
        grid=(batch, C_HEADS, nq),
        in_specs=[
            pl.BlockSpec((1, head_w, tq), blk),
            pl.BlockSpec((1, seq, head_w), lambda b, h, i: (b, 0, h)),
            pl.BlockSpec((1, C_V_DIM, seq), lambda b, h, i: (b, h, 0)),
            pl.BlockSpec((1, C_V_DIM, tq), blk),
            pl.BlockSpec((C_V_DIM, tq), const2),
            pl.BlockSpec((1, HEAD_DIM), const2),
            pl.BlockSpec((1, HEAD_DIM), const2),
            pl.BlockSpec((1, HEAD_DIM), const2),
            pl.BlockSpec((1, HEAD_DIM), const2),
        ],
        out_specs=pl.BlockSpec((1, C_V_DIM, tq), blk),
        out_shape=jax.ShapeDtypeStruct((batch, C_W, seq), BF16),
        scratch_shapes=[pltpu.VMEM((C_V_DIM, tq), F32), pltpu.VMEM((C_V_DIM, tq), F32)],
        compiler_params=_params(("parallel", "parallel", "arbitrary")),
        name="diff_attn_online" if running_max else "diff_attn_direct",
    )(q_t, k3, v_t, gate_t, subln_tile, lq1, lk1, lq2, lk2)


def _qk_score_bound(q_gain, k_gain):
    bound = HEAD_DIM * QK_SCALE * jnp.max(jnp.abs(q_gain)) * jnp.max(jnp.abs(k_gain))
    return bound.astype(F32) * (1.0 + 2.0 ** -6)


def _rope_tables_t(seq):
    inv = 1.0 / (ROPE_THETA ** (jnp.arange(0, HEAD_DIM, 2, dtype=F32) / HEAD_DIM))
    ang = inv[:, None] * jnp.arange(seq, dtype=F32)[None, :]
    return jnp.cos(ang), jnp.sin(ang)


def _gain_tile(g, width):
    return jnp.broadcast_to(g.astype(F32)[:, None], (g.shape[0], width))


def _rel_bias_t(rel_table, n_q):
    n_k = B_PREV_ROWS + n_q
    width = n_k + n_q
    offset = np.arange(width) - (n_k - 1)
    rel = np.clip(B_PREV_ROWS + offset, -B_MAX_REL, B_MAX_REL) + B_MAX_REL
    by_offset = rel_table.astype(F32)[:, rel]
    heads = rel_table.shape[0]
    sub = 8

    def build(f_ref, o_ref):
        for h in range(heads):
            f = f_ref[h:h + 1, :]
            rows = jnp.concatenate(
                [pltpu.roll(f, (width + b - (sub - 1)) % width, axis=1) for b in range(sub)], axis=0)
            for blk in range(n_k // sub):
                lead = n_k - sub - sub * blk
                o_ref[h, sub * blk:sub * (blk + 1), :] = pltpu.roll(
                    rows, (width - lead) % width, axis=1)[:, :n_q]

    return pl.pallas_call(
        build,
        out_shape=jax.ShapeDtypeStruct((heads, n_k, n_q), F32),
        name="rel_bias_tile",
    )(by_offset)


def kernel(x, ev_norm, ev_w_in, ev_w_out, ev_a_q_norm, ev_a_k_norm, ev_a_sinks,
           ev_b_q_norm, ev_b_k_norm, ev_b_rel_bias, od_norm, od_w_in, od_w_out,
           od_q_norm, od_k_norm, od_lambda_q1, od_lambda_k1, od_lambda_q2,
           od_lambda_k2, od_subln):
    batch, seq, d = x.shape
    cos_t, sin_t = _rope_tables_t(seq)
    depth = ev_norm.shape[0] + od_norm.shape[0]
    assert ev_norm.shape[0] == od_norm.shape[0]
    x2d = x.reshape(batch * seq, d)
    for layer in range(depth):
        i = layer // 2
        if layer % 2 == 0:
            w_in_t = ev_w_in[i].T.astype(BF16)
            w_out_t = ev_w_out[i].T.astype(BF16)
            aq_t, ak, av_t, bq_t, bk, bv_t, gate_t = _even_proj(
                x2d, ev_norm[i][None, :], w_in_t, cos_t, sin_t,
                *[_gain_tile(g[i], EVEN_PROJ_TOKENS)
                  for g in (ev_a_q_norm, ev_a_k_norm, ev_b_q_norm, ev_b_k_norm)],
                batch, seq)
            mask_a = jnp.where(_band_mask_variants(A_PREV_ROWS, MIX_TOKENS), 0.0, NEG_INF).astype(F32)
            bias_b = jnp.where(_band_mask_variants(B_PREV_ROWS, MIX_TOKENS)[:, None],
                               _rel_bias_t(ev_b_rel_bias[i] * LOG2E, MIX_TOKENS)[None], NEG_INF)
            sinks = ev_a_sinks[i].astype(F32)
            mix_args = (sinks * LOG2E, aq_t, ak.reshape(batch, seq, A_KV), av_t,
                        bq_t, bk.reshape(batch, seq, B_W), bv_t, gate_t, mask_a, bias_b)
            bound_a = jnp.maximum(_qk_score_bound(ev_a_q_norm[i], ev_a_k_norm[i]),
                                  jnp.max(jnp.abs(sinks)))
            bound_b = (_qk_score_bound(ev_b_q_norm[i], ev_b_k_norm[i])
                       + jnp.max(jnp.abs(ev_b_rel_bias[i])))
            y_t = lax.cond(jnp.maximum(bound_a, bound_b) <= SAFE_SCORE,
                           functools.partial(_even_mix, False),
                           functools.partial(_even_mix, True), *mix_args)
            even_w_out_t = w_out_t
        else:
            lambda_init = 0.8 - 0.6 * math.exp(-0.3 * layer)
            w_in_t = od_w_in[i].T.astype(BF16)
            w_out_t = od_w_out[i].T.astype(BF16)
            x_t, q_t, k, v_t, gate_t = _odd_proj(
                x2d, y_t, even_w_out_t, od_norm[i][:, None], w_in_t, cos_t, sin_t,
                _gain_tile(od_q_norm[i], ODD_PROJ_TOKENS), _gain_tile(od_k_norm[i], ODD_PROJ_TOKENS))
            diff_args = (q_t, k.reshape(batch, seq, C_QK), v_t, gate_t,
                         _gain_tile(od_subln[i], DIFF_Q),
                         od_lambda_q1[i][None, :], od_lambda_k1[i][None, :],
                         od_lambda_q2[i][None, :], od_lambda_k2[i][None, :])
            y_t = lax.cond(
                _qk_score_bound(od_q_norm[i], od_k_norm[i]) <= SAFE_SCORE,
                lambda *a: _diff_attn(False, *a, lambda_init),
                lambda *a: _diff_attn(True, *a, lambda_init),
                *diff_args)
            x2d = _out_proj(x_t, y_t, w_out_t)
    return x2d.reshape(batch, seq, d)
```

```python
import functools
import math

import jax
import jax.numpy as jnp
import numpy as np
from jax import lax
from jax.experimental import pallas as pl
from jax.experimental.pallas import tpu as pltpu

F32 = jnp.float32
BF16 = jnp.bfloat16

D_MODEL = 1024
CHUNK = 64
HEAD_DIM = 64
HALF_DIM = HEAD_DIM // 2
ROPE_THETA = 10000.0
EPS = 1e-6
NEG_INF = -1e30
QK_SCALE = 1.0 / math.sqrt(HEAD_DIM)
LOG2E = math.log2(math.e)
Q_FOLD = QK_SCALE * LOG2E

A_Q_HEADS = 8
A_KV_HEADS = 2
A_PREV_CHUNKS = 2
B_HEADS = 8
B_PREV_CHUNKS = 8
B_MAX_REL = 128
C_HEADS = 8
C_V_DIM = 2 * HEAD_DIM

A_Q = A_Q_HEADS * HEAD_DIM
A_KV = A_KV_HEADS * HEAD_DIM
B_W = B_HEADS * HEAD_DIM
EVEN_MIX = A_Q + B_W
C_QK = C_HEADS * 2 * HEAD_DIM
C_W = C_HEADS * C_V_DIM
PV_SUM_ROWS = 16
BAND_V_ROWS = HEAD_DIM + PV_SUM_ROWS
A_V_ROWS = A_KV_HEADS * BAND_V_ROWS
B_V_ROWS = B_HEADS * BAND_V_ROWS

V7X_VMEM_LIMIT_BYTES = 56 * 1024 * 1024

EVEN_PROJ_TOKENS = 1024
ODD_PROJ_TOKENS = 512
OUT_PROJ_TOKENS = 2048
MIX_TOKENS = 256
MIX_PV_LAG = 4
A_PREV_ROWS = A_PREV_CHUNKS * CHUNK
B_PREV_ROWS = B_PREV_CHUNKS * CHUNK
DIFF_Q = 2048
DIFF_K = 512
DIFF_DIAG_K = 256
DIFF_PV_LAG = 4
SAFE_SCORE = 40.0


def _params(semantics):
    return pltpu.CompilerParams(dimension_semantics=semantics,
                                vmem_limit_bytes=V7X_VMEM_LIMIT_BYTES)


def _rms_rows(x, gain_row):
    ms = jnp.mean(x * x, axis=-1, keepdims=True)
    return x * lax.rsqrt(ms + EPS) * gain_row


def _proj_t(w_t_ref, lo, width, h):
    return lax.dot_general(w_t_ref[lo:lo + width, :], h,
                           (((1,), (1,)), ((), ())),
                           preferred_element_type=F32)


def _head_norm_t(p, gain_tile, cos=None, sin=None, scale=1.0):
    width, tm = p.shape
    y = p.reshape(width // HEAD_DIM, HEAD_DIM, tm)
    ms = jnp.mean(y * y, axis=1, keepdims=True)
    y = y * lax.rsqrt(ms + EPS) * gain_tile[None]
    if cos is not None:
        y1 = y[:, :HALF_DIM]
        y2 = y[:, HALF_DIM:]
        c = cos[None]
        s = sin[None]
        y = jnp.concatenate([y1 * c - y2 * s, y2 * c + y1 * s], axis=1)
    if scale != 1.0:
        y = y * scale
    return y.reshape(width, tm)


def _silu(x):
    return x * (1.0 / (1.0 + jnp.exp(-x)))


def _even_proj_kernel(x_ref, g_ref, w_t_ref, cos_ref, sin_ref,
                      aqn_ref, akn_ref, bqn_ref, bkn_ref,
                      aq_ref, ak_ref, av_ref, bq_ref, bk_ref, bv_ref, gate_ref):
    h = _rms_rows(x_ref[...], g_ref[...]).astype(BF16)
    cos = cos_ref[...]
    sin = sin_ref[...]
    a_w = 2 * A_Q + 2 * A_KV
    a_all = _proj_t(w_t_ref, 0, a_w, h)
    b_qk = _proj_t(w_t_ref, a_w, 2 * B_W, h)
    b_vg = _proj_t(w_t_ref, a_w + 2 * B_W, 2 * B_W, h)
    ak = _head_norm_t(a_all[A_Q:A_Q + A_KV], akn_ref[...], cos, sin)
    ak_ref[...] = ak.T.astype(BF16)
    aq_ref[0] = _head_norm_t(a_all[:A_Q], aqn_ref[...], cos, sin, Q_FOLD).astype(BF16)
    _store_v_with_ones(av_ref, a_all[A_Q + A_KV:A_Q + 2 * A_KV])
    gate_ref[0, :A_Q, :] = _silu(a_all[A_Q + 2 * A_KV:])
    bk = _head_norm_t(b_qk[B_W:], bkn_ref[...])
    bk_ref[...] = bk.T.astype(BF16)
    bq_ref[0] = _head_norm_t(b_qk[:B_W], bqn_ref[...], scale=Q_FOLD).astype(BF16)
    gate_ref[0, A_Q:, :] = _silu(b_vg[B_W:])
    _store_v_with_ones(bv_ref, b_vg[:B_W])


def _store_v_with_ones(v_ref, v):
    tm = v.shape[1]
    ones = jnp.ones((PV_SUM_ROWS, tm), BF16)
    for h in range(v.shape[0] // HEAD_DIM):
        v_ref[0, h * BAND_V_ROWS:h * BAND_V_ROWS + HEAD_DIM, :] = (
            v[h * HEAD_DIM:(h + 1) * HEAD_DIM].astype(BF16))
        v_ref[0, h * BAND_V_ROWS + HEAD_DIM:(h + 1) * BAND_V_ROWS, :] = ones


def _even_proj(x2d, gain, w_t, cos_t, sin_t, aqn, akn, bqn, bkn, batch, seq):
    tm = EVEN_PROJ_TOKENS
    nt = seq // tm
    tokens = batch * seq
    even_in = w_t.shape[0]

    def tok(b, i):
        return (b * nt + i, 0)

    def feat(b, i):
        return (b, 0, i)

    def const2(b, i):
        return (0, 0)

    def tab(b, i):
        return (0, i)

    in_specs = [
        pl.BlockSpec((tm, D_MODEL), tok),
        pl.BlockSpec((1, D_MODEL), const2),
        pl.BlockSpec((even_in, D_MODEL), const2, pipeline_mode=pl.Buffered(1)),
        pl.BlockSpec((HALF_DIM, tm), tab),
        pl.BlockSpec((HALF_DIM, tm), tab),
        pl.BlockSpec((HEAD_DIM, tm), const2),
        pl.BlockSpec((HEAD_DIM, tm), const2),
        pl.BlockSpec((HEAD_DIM, tm), const2),
        pl.BlockSpec((HEAD_DIM, tm), const2),
    ]
    out_shape = [
        jax.ShapeDtypeStruct((batch, A_Q, seq), BF16),
        jax.ShapeDtypeStruct((tokens, A_KV), BF16),
        jax.ShapeDtypeStruct((batch, A_V_ROWS, seq), BF16),
        jax.ShapeDtypeStruct((batch, B_W, seq), BF16),
        jax.ShapeDtypeStruct((tokens, B_W), BF16),
        jax.ShapeDtypeStruct((batch, B_V_ROWS, seq), BF16),
        jax.ShapeDtypeStruct((batch, EVEN_MIX, seq), F32),
    ]
    out_specs = [
        pl.BlockSpec((1, A_Q, tm), feat),
        pl.BlockSpec((tm, A_KV), tok),
        pl.BlockSpec((1, A_V_ROWS, tm), feat),
        pl.BlockSpec((1, B_W, tm), feat),
        pl.BlockSpec((tm, B_W), tok),
        pl.BlockSpec((1, B_V_ROWS, tm), feat),
        pl.BlockSpec((1, EVEN_MIX, tm), feat),
    ]
    return pl.pallas_call(
        _even_proj_kernel,
        grid=(batch, nt),
        in_specs=in_specs,
        out_specs=out_specs,
        out_shape=out_shape,
        compiler_params=_params(("parallel", "parallel")),
        name="even_proj",
    )(x2d, gain, w_t, cos_t, sin_t, aqn, akn, bqn, bkn)


def _band_mask_np(prev_rows, n_q, q_block):
    key = np.arange(prev_rows + n_q)[:, None]
    qry = np.arange(n_q)[None, :] + prev_rows
    return ((key // CHUNK <= qry // CHUNK)
            & (key // CHUNK >= qry // CHUNK - prev_rows // CHUNK)
            & (key >= prev_rows - q_block * n_q))


def _band_mask_variants(prev_rows, n_q):
    n = -(-prev_rows // n_q) + 1
    return np.stack([_band_mask_np(prev_rows, n_q, v) for v in range(n)])


def _band_probs(running_max, k_win, q_pad, add_tile, sink):
    s = jnp.dot(k_win, q_pad, preferred_element_type=F32) + add_tile
    if running_max:
        m = jnp.max(s, axis=0, keepdims=True)
        if sink is not None:
            m = jnp.maximum(m, sink)
        s = s - m
        sink = None if sink is None else sink - m
    sink_term = None if sink is None else jnp.exp2(jnp.broadcast_to(sink, (1, s.shape[1])))
    return jnp.exp2(s).astype(BF16), sink_term


def _pad_q(q_h, half):
    z = jnp.zeros_like(q_h)
    return jnp.concatenate([q_h, z] if half == 0 else [z, q_h], axis=0)


def _even_mix_kernel(running_max, sinks_ref, aq_ref, akp_ref, akc_ref, avp_ref, avc_ref,
                     bq_ref, bkp2_ref, bkp1_ref, bkc_ref, bvp2_ref, bvp1_ref, bvc_ref,
                     gate_ref, mask_ref, bias_ref, y_ref):
    ak = jnp.concatenate([akp_ref[0], akc_ref[0]], axis=0)
    av = jnp.concatenate([avp_ref[0], avc_ref[0]], axis=1)
    bk = jnp.concatenate([bkp2_ref[0], bkp1_ref[0], bkc_ref[0]], axis=0)
    bv = jnp.concatenate([bvp2_ref[0], bvp1_ref[0], bvc_ref[0]], axis=1)

    group = A_Q_HEADS // A_KV_HEADS
    heads = []
    for h in range(A_Q_HEADS):
        kv = h // group
        heads.append((ak, _pad_q(aq_ref[0, h * HEAD_DIM:(h + 1) * HEAD_DIM, :], kv),
                      av[kv * BAND_V_ROWS:(kv + 1) * BAND_V_ROWS, :], mask_ref[0], sinks_ref[h],
                      h * HEAD_DIM))
    for h in range(B_HEADS):
        pair = h // 2
        heads.append((bk[:, pair * 128:(pair + 1) * 128],
                      _pad_q(bq_ref[0, h * HEAD_DIM:(h + 1) * HEAD_DIM, :], h % 2),
                      bv[h * BAND_V_ROWS:(h + 1) * BAND_V_ROWS, :], bias_ref[0, h], None,
                      A_Q + h * HEAD_DIM))

    pending = []

    def emit_pv():
        v_t, p, sink_term, row0 = pending.pop(0)
        o = jnp.dot(v_t, p, preferred_element_type=F32)
        l = o[HEAD_DIM:HEAD_DIM + 1]
        if sink_term is not None:
            l = l + sink_term
        rows = slice(row0, row0 + HEAD_DIM)
        y_ref[0, rows, :] = (o[:HEAD_DIM] * (1.0 / l) * gate_ref[0, rows, :]).astype(BF16)

    for k_win, q_pad, v_t, add_tile, sink, row0 in heads:
        p, sink_term = _band_probs(running_max, k_win, q_pad, add_tile, sink)
        pending.append((v_t, p, sink_term, row0))
        if len(pending) > MIX_PV_LAG:
            emit_pv()
    while pending:
        emit_pv()


def _even_mix(running_max, sinks, aq_t, ak3, av_t, bq_t, bk3, bv_t, gate_t, mask_a, bias_b):
    batch, seq = ak3.shape[0], ak3.shape[1]
    tq = MIX_TOKENS
    nq = seq // tq
    assert tq % A_PREV_ROWS == 0 and B_PREV_ROWS == 2 * tq and seq % tq == 0
    a_ratio = tq // A_PREV_ROWS

    def cur_t(b, i, *_):
        return (b, 0, i)

    def cur_n(b, i, *_):
        return (b, i, 0)

    def a_prev_n(b, i, *_):
        return (b, jnp.maximum(i * a_ratio - 1, 0), 0)

    def a_prev_t(b, i, *_):
        return (b, 0, jnp.maximum(i * a_ratio - 1, 0))

    def b_prev_n(back):
        return lambda b, i, *_: (b, jnp.maximum(i - back, 0), 0)

    def b_prev_t(back):
        return lambda b, i, *_: (b, 0, jnp.maximum(i - back, 0))

    in_specs = [
        pl.BlockSpec((1, A_Q, tq), cur_t),
        pl.BlockSpec((1, A_PREV_ROWS, A_KV), a_prev_n),
        pl.BlockSpec((1, tq, A_KV), cur_n),
        pl.BlockSpec((1, A_V_ROWS, A_PREV_ROWS), a_prev_t),
        pl.BlockSpec((1, A_V_ROWS, tq), cur_t),
        pl.BlockSpec((1, B_W, tq), cur_t),
        pl.BlockSpec((1, tq, B_W), b_prev_n(2)),
        pl.BlockSpec((1, tq, B_W), b_prev_n(1)),
        pl.BlockSpec((1, tq, B_W), cur_n),
        pl.BlockSpec((1, B_V_ROWS, tq), b_prev_t(2)),
        pl.BlockSpec((1, B_V_ROWS, tq), b_prev_t(1)),
        pl.BlockSpec((1, B_V_ROWS, tq), cur_t),
        pl.BlockSpec((1, EVEN_MIX, tq), cur_t),
        pl.BlockSpec((1,) + mask_a.shape[1:],
                     lambda b, i, *_: (jnp.minimum(i, mask_a.shape[0] - 1), 0, 0)),
        pl.BlockSpec((1,) + bias_b.shape[1:],
                     lambda b, i, *_: (jnp.minimum(i, bias_b.shape[0] - 1), 0, 0, 0)),
    ]
    grid_spec = pltpu.PrefetchScalarGridSpec(
        num_scalar_prefetch=1,
        grid=(batch, nq),
        in_specs=in_specs,
        out_specs=pl.BlockSpec((1, EVEN_MIX, tq), cur_t),
    )
    return pl.pallas_call(
        functools.partial(_even_mix_kernel, running_max),
        grid_spec=grid_spec,
        out_shape=jax.ShapeDtypeStruct((batch, EVEN_MIX, seq), BF16),
        compiler_params=_params(("parallel", "parallel")),
        name="even_mix_online" if running_max else "even_mix_direct",
    )(sinks, aq_t, ak3, ak3, av_t, av_t, bq_t, bk3, bk3, bk3, bv_t, bv_t, bv_t,
      gate_t, mask_a, bias_b)


def _out_proj_kernel(x_ref, y_ref, w_t_ref, o_ref):
    out_t = jnp.dot(w_t_ref[...], y_ref[0], preferred_element_type=F32)
    o_ref[...] = (x_ref[0] + out_t).T


def _out_proj(x_t, y_t, w_t):
    batch, width, seq = y_t.shape
    tm = OUT_PROJ_TOKENS
    nt = seq // tm
    return pl.pallas_call(
        _out_proj_kernel,
        grid=(batch, nt),
        in_specs=[
            pl.BlockSpec((1, D_MODEL, tm), lambda b, i: (b, 0, i)),
            pl.BlockSpec((1, width, tm), lambda b, i: (b, 0, i)),
            pl.BlockSpec((D_MODEL, width), lambda b, i: (0, 0)),
        ],
        out_specs=pl.BlockSpec((tm, D_MODEL), lambda b, i: (b * nt + i, 0)),
        out_shape=jax.ShapeDtypeStruct((batch * seq, D_MODEL), F32),
        compiler_params=_params(("parallel", "parallel")),
        name="out_proj",
    )(x_t, y_t, w_t)


def _odd_proj_kernel(x_ref, y_ref, w_out_t_ref, g_ref, w_t_ref, cos_ref, sin_ref, qn_ref, kn_ref,
                     x1_ref, q_ref, k_ref, v_ref, gate_ref):
    out_t = jnp.dot(w_out_t_ref[...], y_ref[0], preferred_element_type=F32)
    x = x_ref[...].T + out_t
    x1_ref[0] = x
    ms = jnp.mean(x * x, axis=0, keepdims=True)
    h_t = (x * lax.rsqrt(ms + EPS) * g_ref[...]).astype(BF16)
    cos = cos_ref[...]
    sin = sin_ref[...]

    def proj(lo, width):
        return jnp.dot(w_t_ref[lo:lo + width, :], h_t, preferred_element_type=F32)

    k = proj(C_QK, C_QK)
    q = proj(0, C_QK)
    g = proj(2 * C_QK + C_W, C_W)
    v = proj(2 * C_QK, C_W)
    k = _head_norm_t(k, kn_ref[...], cos, sin)
    k_ref[...] = k.T.astype(BF16)
    q_ref[0] = _head_norm_t(q, qn_ref[...], cos, sin, Q_FOLD).astype(BF16)
    gate_ref[0] = _silu(g)
    v_ref[0] = v.astype(BF16)


def _odd_proj(x2d, y_t, w_out_t, gain_col, w_t, cos_t, sin_t, qn, kn):
    batch, mix_w, seq = y_t.shape
    tm = ODD_PROJ_TOKENS
    nt = seq // tm
    tokens = batch * seq
    odd_in = w_t.shape[0]
    resident = pl.Buffered(1)

    def tok(b, i):
        return (b * nt + i, 0)

    def feat(b, i):
        return (b, 0, i)

    def const2(b, i):
        return (0, 0)

    def tab(b, i):
        return (0, i)

    return pl.pallas_call(
        _odd_proj_kernel,
        grid=(batch, nt),
        in_specs=[
            pl.BlockSpec((tm, D_MODEL), tok),
            pl.BlockSpec((1, mix_w, tm), feat),
            pl.BlockSpec((D_MODEL, mix_w), const2, pipeline_mode=resident),
            pl.BlockSpec((D_MODEL, 1), const2),
            pl.BlockSpec((odd_in, D_MODEL), const2, pipeline_mode=resident),
            pl.BlockSpec((HALF_DIM, tm), tab),
            pl.BlockSpec((HALF_DIM, tm), tab),
            pl.BlockSpec((HEAD_DIM, tm), const2),
            pl.BlockSpec((HEAD_DIM, tm), const2),
        ],
        out_specs=[
            pl.BlockSpec((1, D_MODEL, tm), feat),
            pl.BlockSpec((1, C_QK, tm), feat),
            pl.BlockSpec((tm, C_QK), tok),
            pl.BlockSpec((1, C_W, tm), feat),
            pl.BlockSpec((1, C_W, tm), feat),
        ],
        out_shape=[
            jax.ShapeDtypeStruct((batch, D_MODEL, seq), F32),
            jax.ShapeDtypeStruct((batch, C_QK, seq), BF16),
            jax.ShapeDtypeStruct((tokens, C_QK), BF16),
            jax.ShapeDtypeStruct((batch, C_W, seq), BF16),
            jax.ShapeDtypeStruct((batch, C_W, seq), F32),
        ],
        compiler_params=_params(("parallel", "parallel")),
        name="odd_proj",
    )(x2d, y_t, w_out_t, gain_col, w_t, cos_t, sin_t, qn, kn)


def _diff_attn_kernel(lambda_init, running_max, q_ref, k_ref, v_ref, gate_ref, subln_ref,
                      lq1_ref, lk1_ref, lq2_ref, lk2_ref, y_ref, acc1_ref, acc2_ref):
    tq, tk = DIFF_Q, DIFF_K
    i = pl.program_id(2)
    q = q_ref[0]
    row = lax.broadcasted_iota(jnp.int32, q.shape, 0)
    zero = jnp.zeros_like(q)
    q_pads = (jnp.where(row < HEAD_DIM, q, zero), jnp.where(row >= HEAD_DIM, q, zero))
    acc_refs = (acc1_ref, acc2_ref)

    def load_kv(start, size):
        start = pl.multiple_of(start, size)
        return k_ref[0, pl.ds(start, size), :], v_ref[0, :, pl.ds(start, size)]

    def diag_valid(key0, size, n_cols):
        key_chunk = (lax.broadcasted_iota(jnp.int32, (size, n_cols), 0) + key0) // CHUNK
        qry_chunk = (lax.broadcasted_iota(jnp.int32, (size, n_cols), 1) + (tq - n_cols)) // CHUNK
        return key_chunk <= qry_chunk

    def online_step(k, v_t, carry, valid):
        new = []
        for c in range(2):
            m, l = carry[2 * c], carry[2 * c + 1]
            s = jnp.dot(k, q_pads[c], preferred_element_type=F32)
            if valid is not None:
                s = jnp.where(valid, s, NEG_INF)
            m_new = jnp.maximum(m, jnp.max(s, axis=0, keepdims=True))
            alpha = jnp.exp2(m - m_new)
            p = jnp.exp2(s - m_new)
            l_new = alpha * l + jnp.sum(p, axis=0, keepdims=True)
            pv = jnp.dot(v_t, p.astype(BF16), preferred_element_type=F32)
            acc_refs[c][...] = alpha * acc_refs[c][...] + pv
            new += [m_new, l_new]
        return tuple(new)

    def direct_probs(k, q_c, valid):
        p = jnp.exp2(jnp.dot(k, q_c, preferred_element_type=F32))
        if valid is not None:
            p = jnp.where(valid, p, 0.0)
        return p.astype(BF16), jnp.sum(p, axis=0, keepdims=True)

    def emit_lagged(tiles, pv_lag, on_pv):
        pending = []

        def emit_pv():
            tag, v_t, p, total = pending.pop(0)
            on_pv(tag, total, jnp.dot(v_t, p, preferred_element_type=F32))

        for tag, k, v_t, q_c, valid in tiles:
            p, total = direct_probs(k, q_c, valid)
            pending.append((tag, v_t, p, total))
            if len(pending) > pv_lag:
                emit_pv()
        while pending:
            emit_pv()

    lam = (jnp.exp(jnp.sum(lq1_ref[...] * lk1_ref[...], axis=-1, keepdims=True))
           - jnp.exp(jnp.sum(lq2_ref[...] * lk2_ref[...], axis=-1, keepdims=True))
           + lambda_init)

    def finish(cols, acc1, acc2, l1, l2):
        o = acc1 * (1.0 / l1) - lam * (acc2 * (1.0 / l2))
        ms = jnp.mean(o * o, axis=0, keepdims=True)
        o = o * lax.rsqrt(ms + EPS) * subln_ref[:, cols] * (1.0 - lambda_init)
        y_ref[0, :, cols] = (o * gate_ref[0, :, cols]).astype(BF16)

    q0 = i * tq
    acc1_ref[...] = jnp.zeros_like(acc1_ref)
    acc2_ref[...] = jnp.zeros_like(acc2_ref)
    if running_max:
        init = (jnp.full((1, tq), NEG_INF, F32), jnp.zeros((1, tq), F32)) * 2
        carry = lax.fori_loop(0, q0 // tk,
                              lambda j, c: online_step(*load_kv(j * tk, tk), c, None), init)
        for key0 in range(0, tq, tk):
            carry = online_step(*load_kv(q0 + key0, tk), carry, diag_valid(key0, tk, tq))
        finish(slice(0, tq), acc1_ref[...], acc2_ref[...], carry[1], carry[3])
    else:
        def group(t, l_sums):
            l_sums = list(l_sums)
            pv_sums = [0.0, 0.0]

            def on_pv(c, total, pv):
                l_sums[c] = l_sums[c] + total
                pv_sums[c] = pv_sums[c] + pv

            kv = [load_kv(t * tq + u * tk, tk) for u in range(tq // tk)]
            emit_lagged([(c, k, v_t, q_pads[c], None) for k, v_t in kv for c in range(2)],
                        DIFF_PV_LAG, on_pv)
            for c in range(2):
                acc_refs[c][...] += pv_sums[c]
            return tuple(l_sums)

        l_sums = lax.fori_loop(0, i, group, (jnp.zeros((1, tq), F32),) * 2)

        td = DIFF_DIAG_K
        n_seg = tq // td
        parts = [[[] for _ in range(n_seg)] for _ in range(2)]

        def on_diag_pv(tag, total, pv):
            b, c = tag
            for s in range(b, n_seg):
                seg = slice((s - b) * td, (s - b + 1) * td)
                parts[c][s].append((total[:, seg], pv[:, seg]))
            if c == 1:
                cols = slice(b * td, (b + 1) * td)
                sums = [(l_sums[cc][:, cols] + sum(t for t, _ in parts[cc][b]),
                         acc_refs[cc][:, cols] + sum(p for _, p in parts[cc][b]))
                        for cc in range(2)]
                finish(cols, sums[0][1], sums[1][1], sums[0][0], sums[1][0])

        diag_tiles = []
        for b in range(n_seg):
            k, v_t = load_kv(q0 + b * td, td)
            valid = diag_valid(b * td, td, tq - b * td)
            diag_tiles += [((b, c), k, v_t, q_pads[c][:, b * td:], valid) for c in range(2)]
        emit_lagged(diag_tiles, 1, on_diag_pv)


def _diff_attn(running_max, q_t, k3, v_t, gate_t, subln_tile, lq1, lk1, lq2, lk2, lambda_init):
    batch, seq = k3.shape[0], k3.shape[1]
    tq = DIFF_Q
    nq = seq // tq
    head_w = 2 * HEAD_DIM

    def blk(b, h, i):
        return (b, h, i)

    def const2(b, h, i):
        return (0, 0)

    return pl.pallas_call(
        functools.partial(_diff_attn_kernel, lambda_init, running_max),
        grid=(batch, C_HEADS, nq),
        in_specs=[
            pl.BlockSpec((1, head_w, tq), blk),
            pl.BlockSpec((1, seq, head_w), lambda b, h, i: (b, 0, h)),
            pl.BlockSpec((1, C_V_DIM, seq), lambda b, h, i: (b, h, 0)),
            pl.BlockSpec((1, C_V_DIM, tq), blk),
            pl.BlockSpec((C_V_DIM, tq), const2),
            pl.BlockSpec((1, HEAD_DIM), const2),
            pl.BlockSpec((1, HEAD_DIM), const2),
            pl.BlockSpec((1, HEAD_DIM), const2),
            pl.BlockSpec((1, HEAD_DIM), const2),
        ],
        out_specs=pl.BlockSpec((1, C_V_DIM, tq), blk),
        out_shape=jax.ShapeDtypeStruct((batch, C_W, seq), BF16),
        scratch_shapes=[pltpu.VMEM((C_V_DIM, tq), F32), pltpu.VMEM((C_V_DIM, tq), F32)],
        compiler_params=_params(("parallel", "parallel", "arbitrary")),
        name="diff_attn_online" if running_max else "diff_attn_direct",
    )(q_t, k3, v_t, gate_t, subln_tile, lq1, lk1, lq2, lk2)


def _qk_score_bound(q_gain, k_gain):
    bound = HEAD_DIM * QK_SCALE * jnp.max(jnp.abs(q_gain)) * jnp.max(jnp.abs(k_gain))
    return bound.astype(F32) * (1.0 + 2.0 ** -6)


def _rope_tables_t(seq):
    inv = 1.0 / (ROPE_THETA ** (jnp.arange(0, HEAD_DIM, 2, dtype=F32) / HEAD_DIM))
    ang = inv[:, None] * jnp.arange(seq, dtype=F32)[None, :]
    return jnp.cos(ang), jnp.sin(ang)


def _gain_tile(g, width):
    return jnp.broadcast_to(g.astype(F32)[:, None], (g.shape[0], width))


def _rel_bias_t(rel_table, n_q):
    n_k = B_PREV_ROWS + n_q
    width = n_k + n_q
    offset = np.arange(width) - (n_k - 1)
    rel = np.clip(B_PREV_ROWS + offset, -B_MAX_REL, B_MAX_REL) + B_MAX_REL
    by_offset = rel_table.astype(F32)[:, rel]
    heads = rel_table.shape[0]
    sub = 8

    def build(f_ref, o_ref):
        for h in range(heads):
            f = f_ref[h:h + 1, :]
            rows = jnp.concatenate(
                [pltpu.roll(f, (width + b - (sub - 1)) % width, axis=1) for b in range(sub)], axis=0)
            for blk in range(n_k // sub):
                lead = n_k - sub - sub * blk
                o_ref[h, sub * blk:sub * (blk + 1), :] = pltpu.roll(
                    rows, (width - lead) % width, axis=1)[:, :n_q]

    return pl.pallas_call(
        build,
        out_shape=jax.ShapeDtypeStruct((heads, n_k, n_q), F32),
        name="rel_bias_tile",
    )(by_offset)


def kernel(x, ev_norm, ev_w_in, ev_w_out, ev_a_q_norm, ev_a_k_norm, ev_a_sinks,
           ev_b_q_norm, ev_b_k_norm, ev_b_rel_bias, od_norm, od_w_in, od_w_out,
           od_q_norm, od_k_norm, od_lambda_q1, od_lambda_k1, od_lambda_q2,
           od_lambda_k2, od_subln):
    batch, seq, d = x.shape
    cos_t, sin_t = _rope_tables_t(seq)
    depth = ev_norm.shape[0] + od_norm.shape[0]
    assert ev_norm.shape[0] == od_norm.shape[0]
    x2d = x.reshape(batch * seq, d)
    for layer in range(depth):
        i = layer // 2
        if layer % 2 == 0:
            w_in_t = ev_w_in[i].T.astype(BF16)
            w_out_t = ev_w_out[i].T.astype(BF16)
            aq_t, ak, av_t, bq_t, bk, bv_t, gate_t = _even_proj(
                x2d, ev_norm[i][None, :], w_in_t, cos_t, sin_t,
                *[_gain_tile(g[i], EVEN_PROJ_TOKENS)
                  for g in (ev_a_q_norm, ev_a_k_norm, ev_b_q_norm, ev_b_k_norm)],
                batch, seq)
            mask_a = jnp.where(_band_mask_variants(A_PREV_ROWS, MIX_TOKENS), 0.0, NEG_INF).astype(F32)
            bias_b = jnp.where(_band_mask_variants(B_PREV_ROWS, MIX_TOKENS)[:, None],
                               _rel_bias_t(ev_b_rel_bias[i] * LOG2E, MIX_TOKENS)[None], NEG_INF)
            sinks = ev_a_sinks[i].astype(F32)
            mix_args = (sinks * LOG2E, aq_t, ak.reshape(batch, seq, A_KV), av_t,
                        bq_t, bk.reshape(batch, seq, B_W), bv_t, gate_t, mask_a, bias_b)
            bound_a = jnp.maximum(_qk_score_bound(ev_a_q_norm[i], ev_a_k_norm[i]),
                                  jnp.max(jnp.abs(sinks)))
            bound_b = (_qk_score_bound(ev_b_q_norm[i], ev_b_k_norm[i])
                       + jnp.max(jnp.abs(ev_b_rel_bias[i])))
            y_t = lax.cond(jnp.maximum(bound_a, bound_b) <= SAFE_SCORE,
                           functools.partial(_even_mix, False),
                           functools.partial(_even_mix, True), *mix_args)
            even_w_out_t = w_out_t
        else:
            lambda_init = 0.8 - 0.6 * math.exp(-0.3 * layer)
            w_in_t = od_w_in[i].T.astype(BF16)
            w_out_t = od_w_out[i].T.astype(BF16)
            x_t, q_t, k, v_t, gate_t = _odd_proj(
                x2d, y_t, even_w_out_t, od_norm[i][:, None], w_in_t, cos_t, sin_t,
                _gain_tile(od_q_norm[i], ODD_PROJ_TOKENS), _gain_tile(od_k_norm[i], ODD_PROJ_TOKENS))
            diff_args = (q_t, k.reshape(batch, seq, C_QK), v_t, gate_t,
                         _gain_tile(od_subln[i], DIFF_Q),
                         od_lambda_q1[i][None, :], od_lambda_k1[i][None, :],
                         od_lambda_q2[i][None, :], od_lambda_k2[i][None, :])
            y_t = lax.cond(
                _qk_score_bound(od_q_norm[i], od_k_norm[i]) <= SAFE_SCORE,
                lambda *a: _diff_attn(False, *a, lambda_init),
                lambda *a: _diff_attn(True, *a, lambda_init),
                *diff_args)
            x2d = _out_proj(x_t, y_t, w_out_t)
    return x2d.reshape(batch, seq, d)
```

```python
import functools
import math

import jax
import jax.numpy as jnp
import numpy as np
from jax import lax
from jax.experimental import pallas as pl
from jax.experimental.pallas import tpu as pltpu

F32 = jnp.float32
BF16 = jnp.bfloat16

D_MODEL = 1024
CHUNK = 64
HEAD_DIM = 64
HALF_DIM = HEAD_DIM // 2
ROPE_THETA = 10000.0
EPS = 1e-6
NEG_INF = -1e30
QK_SCALE = 1.0 / math.sqrt(HEAD_DIM)
LOG2E = math.log2(math.e)
Q_FOLD = QK_SCALE * LOG2E

A_Q_HEADS = 8
A_KV_HEADS = 2
A_PREV_CHUNKS = 2
B_HEADS = 8
B_PREV_CHUNKS = 8
B_MAX_REL = 128
C_HEADS = 8
C_V_DIM = 2 * HEAD_DIM

A_Q = A_Q_HEADS * HEAD_DIM
A_KV = A_KV_HEADS * HEAD_DIM
B_W = B_HEADS * HEAD_DIM
EVEN_MIX = A_Q + B_W
C_QK = C_HEADS * 2 * HEAD_DIM
C_W = C_HEADS * C_V_DIM
PV_SUM_ROWS = 16
BAND_V_ROWS = HEAD_DIM + PV_SUM_ROWS
A_V_ROWS = A_KV_HEADS * BAND_V_ROWS
B_V_ROWS = B_HEADS * BAND_V_ROWS

V7X_VMEM_LIMIT_BYTES = 56 * 1024 * 1024

EVEN_PROJ_TOKENS = 1024
ODD_PROJ_TOKENS = 512
OUT_PROJ_TOKENS = 1024
MIX_TOKENS = 256
MIX_PV_LAG = 4
A_PREV_ROWS = A_PREV_CHUNKS * CHUNK
B_PREV_ROWS = B_PREV_CHUNKS * CHUNK
DIFF_Q = 2048
DIFF_K = 512
DIFF_DIAG_K = 256
DIFF_PV_LAG = 4
SAFE_SCORE = 40.0


def _params(semantics):
    return pltpu.CompilerParams(dimension_semantics=semantics,
                                vmem_limit_bytes=V7X_VMEM_LIMIT_BYTES)


def _rms_rows(x, gain_row):
    ms = jnp.mean(x * x, axis=-1, keepdims=True)
    return x * lax.rsqrt(ms + EPS) * gain_row


def _proj_t(w_t_ref, lo, width, h):
    return lax.dot_general(w_t_ref[lo:lo + width, :], h,
                           (((1,), (1,)), ((), ())),
                           preferred_element_type=F32)


def _head_norm_t(p, gain_tile, cos=None, sin=None, scale=1.0):
    width, tm = p.shape
    y = p.reshape(width // HEAD_DIM, HEAD_DIM, tm)
    ms = jnp.mean(y * y, axis=1, keepdims=True)
    y = y * lax.rsqrt(ms + EPS) * gain_tile[None]
    if cos is not None:
        y1 = y[:, :HALF_DIM]
        y2 = y[:, HALF_DIM:]
        c = cos[None]
        s = sin[None]
        y = jnp.concatenate([y1 * c - y2 * s, y2 * c + y1 * s], axis=1)
    if scale != 1.0:
        y = y * scale
    return y.reshape(width, tm)


def _silu(x):
    return x * (1.0 / (1.0 + jnp.exp(-x)))


def _even_proj_kernel(x_ref, g_ref, w_t_ref, cos_ref, sin_ref,
                      aqn_ref, akn_ref, bqn_ref, bkn_ref,
                      aq_ref, ak_ref, av_ref, bq_ref, bk_ref, bv_ref, gate_ref):
    h = _rms_rows(x_ref[...], g_ref[...]).astype(BF16)
    cos = cos_ref[...]
    sin = sin_ref[...]
    a_w = 2 * A_Q + 2 * A_KV
    a_all = _proj_t(w_t_ref, 0, a_w, h)
    b_qk = _proj_t(w_t_ref, a_w, 2 * B_W, h)
    b_vg = _proj_t(w_t_ref, a_w + 2 * B_W, 2 * B_W, h)
    ak = _head_norm_t(a_all[A_Q:A_Q + A_KV], akn_ref[...], cos, sin)
    ak_ref[...] = ak.T.astype(BF16)
    aq_ref[0] = _head_norm_t(a_all[:A_Q], aqn_ref[...], cos, sin, Q_FOLD).astype(BF16)
    _store_v_with_ones(av_ref, a_all[A_Q + A_KV:A_Q + 2 * A_KV])
    gate_ref[0, :A_Q, :] = _silu(a_all[A_Q + 2 * A_KV:])
    bk = _head_norm_t(b_qk[B_W:], bkn_ref[...])
    bk_ref[...] = bk.T.astype(BF16)
    bq_ref[0] = _head_norm_t(b_qk[:B_W], bqn_ref[...], scale=Q_FOLD).astype(BF16)
    gate_ref[0, A_Q:, :] = _silu(b_vg[B_W:])
    _store_v_with_ones(bv_ref, b_vg[:B_W])


def _store_v_with_ones(v_ref, v):
    tm = v.shape[1]
    ones = jnp.ones((PV_SUM_ROWS, tm), BF16)
    for h in range(v.shape[0] // HEAD_DIM):
        v_ref[0, h * BAND_V_ROWS:h * BAND_V_ROWS + HEAD_DIM, :] = (
            v[h * HEAD_DIM:(h + 1) * HEAD_DIM].astype(BF16))
        v_ref[0, h * BAND_V_ROWS + HEAD_DIM:(h + 1) * BAND_V_ROWS, :] = ones


def _even_proj(x2d, gain, w_t, cos_t, sin_t, aqn, akn, bqn, bkn, batch, seq):
    tm = EVEN_PROJ_TOKENS
    nt = seq // tm
    tokens = batch * seq
    even_in = w_t.shape[0]

    def tok(b, i):
        return (b * nt + i, 0)

    def feat(b, i):
        return (b, 0, i)

    def const2(b, i):
        return (0, 0)

    def tab(b, i):
        return (0, i)

    in_specs = [
        pl.BlockSpec((tm, D_MODEL), tok),
        pl.BlockSpec((1, D_MODEL), const2),
        pl.BlockSpec((even_in, D_MODEL), const2, pipeline_mode=pl.Buffered(1)),
        pl.BlockSpec((HALF_DIM, tm), tab),
        pl.BlockSpec((HALF_DIM, tm), tab),
        pl.BlockSpec((HEAD_DIM, tm), const2),
        pl.BlockSpec((HEAD_DIM, tm), const2),
        pl.BlockSpec((HEAD_DIM, tm), const2),
        pl.BlockSpec((HEAD_DIM, tm), const2),
    ]
    out_shape = [
        jax.ShapeDtypeStruct((batch, A_Q, seq), BF16),
        jax.ShapeDtypeStruct((tokens, A_KV), BF16),
        jax.ShapeDtypeStruct((batch, A_V_ROWS, seq), BF16),
        jax.ShapeDtypeStruct((batch, B_W, seq), BF16),
        jax.ShapeDtypeStruct((tokens, B_W), BF16),
        jax.ShapeDtypeStruct((batch, B_V_ROWS, seq), BF16),
        jax.ShapeDtypeStruct((batch, EVEN_MIX, seq), F32),
    ]
    out_specs = [
        pl.BlockSpec((1, A_Q, tm), feat),
        pl.BlockSpec((tm, A_KV), tok),
        pl.BlockSpec((1, A_V_ROWS, tm), feat),
        pl.BlockSpec((1, B_W, tm), feat),
        pl.BlockSpec((tm, B_W), tok),
        pl.BlockSpec((1, B_V_ROWS, tm), feat),
        pl.BlockSpec((1, EVEN_MIX, tm), feat),
    ]
    return pl.pallas_call(
        _even_proj_kernel,
        grid=(batch, nt),
        in_specs=in_specs,
        out_specs=out_specs,
        out_shape=out_shape,
        compiler_params=_params(("parallel", "parallel")),
        name="even_proj",
    )(x2d, gain, w_t, cos_t, sin_t, aqn, akn, bqn, bkn)


def _band_mask_np(prev_rows, n_q, q_block):
    key = np.arange(prev_rows + n_q)[:, None]
    qry = np.arange(n_q)[None, :] + prev_rows
    return ((key // CHUNK <= qry // CHUNK)
            & (key // CHUNK >= qry // CHUNK - prev_rows // CHUNK)
            & (key >= prev_rows - q_block * n_q))


def _band_mask_variants(prev_rows, n_q):
    n = -(-prev_rows // n_q) + 1
    return np.stack([_band_mask_np(prev_rows, n_q, v) for v in range(n)])


def _band_probs(running_max, k_win, q_pad, add_tile, sink):
    s = jnp.dot(k_win, q_pad, preferred_element_type=F32) + add_tile
    if running_max:
        m = jnp.max(s, axis=0, keepdims=True)
        if sink is not None:
            m = jnp.maximum(m, sink)
        s = s - m
        sink = None if sink is None else sink - m
    sink_term = None if sink is None else jnp.exp2(jnp.broadcast_to(sink, (1, s.shape[1])))
    return jnp.exp2(s).astype(BF16), sink_term


def _pad_q(q_h, half):
    z = jnp.zeros_like(q_h)
    return jnp.concatenate([q_h, z] if half == 0 else [z, q_h], axis=0)


def _even_mix_kernel(running_max, sinks_ref, aq_ref, akp_ref, akc_ref, avp_ref, avc_ref,
                     bq_ref, bkp2_ref, bkp1_ref, bkc_ref, bvp2_ref, bvp1_ref, bvc_ref,
                     gate_ref, mask_ref, bias_ref, y_ref):
    ak = jnp.concatenate([akp_ref[0], akc_ref[0]], axis=0)
    av = jnp.concatenate([avp_ref[0], avc_ref[0]], axis=1)
    bk = jnp.concatenate([bkp2_ref[0], bkp1_ref[0], bkc_ref[0]], axis=0)
    bv = jnp.concatenate([bvp2_ref[0], bvp1_ref[0], bvc_ref[0]], axis=1)

    group = A_Q_HEADS // A_KV_HEADS
    heads = []
    for h in range(A_Q_HEADS):
        kv = h // group
        heads.append((ak, _pad_q(aq_ref[0, h * HEAD_DIM:(h + 1) * HEAD_DIM, :], kv),
                      av[kv * BAND_V_ROWS:(kv + 1) * BAND_V_ROWS, :], mask_ref[0], sinks_ref[h],
                      h * HEAD_DIM))
    for h in range(B_HEADS):
        pair = h // 2
        heads.append((bk[:, pair * 128:(pair + 1) * 128],
                      _pad_q(bq_ref[0, h * HEAD_DIM:(h + 1) * HEAD_DIM, :], h % 2),
                      bv[h * BAND_V_ROWS:(h + 1) * BAND_V_ROWS, :], bias_ref[0, h], None,
                      A_Q + h * HEAD_DIM))

    pending = []

    def emit_pv():
        v_t, p, sink_term, row0 = pending.pop(0)
        o = jnp.dot(v_t, p, preferred_element_type=F32)
        l = o[HEAD_DIM:HEAD_DIM + 1]
        if sink_term is not None:
            l = l + sink_term
        rows = slice(row0, row0 + HEAD_DIM)
        y_ref[0, rows, :] = (o[:HEAD_DIM] * (1.0 / l) * gate_ref[0, rows, :]).astype(BF16)

    for k_win, q_pad, v_t, add_tile, sink, row0 in heads:
        p, sink_term = _band_probs(running_max, k_win, q_pad, add_tile, sink)
        pending.append((v_t, p, sink_term, row0))
        if len(pending) > MIX_PV_LAG:
            emit_pv()
    while pending:
        emit_pv()


def _even_mix(running_max, sinks, aq_t, ak3, av_t, bq_t, bk3, bv_t, gate_t, mask_a, bias_b):
    batch, seq = ak3.shape[0], ak3.shape[1]
    tq = MIX_TOKENS
    nq = seq // tq
    assert tq % A_PREV_ROWS == 0 and B_PREV_ROWS == 2 * tq and seq % tq == 0
    a_ratio = tq // A_PREV_ROWS

    def cur_t(b, i, *_):
        return (b, 0, i)

    def cur_n(b, i, *_):
        return (b, i, 0)

    def a_prev_n(b, i, *_):
        return (b, jnp.maximum(i * a_ratio - 1, 0), 0)

    def a_prev_t(b, i, *_):
        return (b, 0, jnp.maximum(i * a_ratio - 1, 0))

    def b_prev_n(back):
        return lambda b, i, *_: (b, jnp.maximum(i - back, 0), 0)

    def b_prev_t(back):
        return lambda b, i, *_: (b, 0, jnp.maximum(i - back, 0))

    in_specs = [
        pl.BlockSpec((1, A_Q, tq), cur_t),
        pl.BlockSpec((1, A_PREV_ROWS, A_KV), a_prev_n),
        pl.BlockSpec((1, tq, A_KV), cur_n),
        pl.BlockSpec((1, A_V_ROWS, A_PREV_ROWS), a_prev_t),
        pl.BlockSpec((1, A_V_ROWS, tq), cur_t),
        pl.BlockSpec((1, B_W, tq), cur_t),
        pl.BlockSpec((1, tq, B_W), b_prev_n(2)),
        pl.BlockSpec((1, tq, B_W), b_prev_n(1)),
        pl.BlockSpec((1, tq, B_W), cur_n),
        pl.BlockSpec((1, B_V_ROWS, tq), b_prev_t(2)),
        pl.BlockSpec((1, B_V_ROWS, tq), b_prev_t(1)),
        pl.BlockSpec((1, B_V_ROWS, tq), cur_t),
        pl.BlockSpec((1, EVEN_MIX, tq), cur_t),
        pl.BlockSpec((1,) + mask_a.shape[1:],
                     lambda b, i, *_: (jnp.minimum(i, mask_a.shape[0] - 1), 0, 0)),
        pl.BlockSpec((1,) + bias_b.shape[1:],
                     lambda b, i, *_: (jnp.minimum(i, bias_b.shape[0] - 1), 0, 0, 0)),
    ]
    grid_spec = pltpu.PrefetchScalarGridSpec(
        num_scalar_prefetch=1,
        grid=(batch, nq),
        in_specs=in_specs,
        out_specs=pl.BlockSpec((1, EVEN_MIX, tq), cur_t),
    )
    return pl.pallas_call(
        functools.partial(_even_mix_kernel, running_max),
        grid_spec=grid_spec,
        out_shape=jax.ShapeDtypeStruct((batch, EVEN_MIX, seq), BF16),
        compiler_params=_params(("parallel", "parallel")),
        name="even_mix_online" if running_max else "even_mix_direct",
    )(sinks, aq_t, ak3, ak3, av_t, av_t, bq_t, bk3, bk3, bk3, bv_t, bv_t, bv_t,
      gate_t, mask_a, bias_b)


OUT_PROJ_BUFFERS = 3


def _out_proj_kernel(nt, n_tiles, x_hbm, y_hbm, w_t_ref, o_ref, x_buf, y_buf, sem):
    g = pl.program_id(0)
    tm = OUT_PROJ_TOKENS

    def tile_copies(t):
        slot = t % OUT_PROJ_BUFFERS
        col = pl.multiple_of((t % nt) * tm, tm)
        src = (t // nt, slice(None), pl.ds(col, tm))
        return (pltpu.make_async_copy(x_hbm.at[src], x_buf.at[slot], sem.at[0, slot]),
                pltpu.make_async_copy(y_hbm.at[src], y_buf.at[slot], sem.at[1, slot]))

    def start(t):
        for copy in tile_copies(t):
            copy.start()

    @pl.when(g == 0)
    def _():
        for t in range(min(OUT_PROJ_BUFFERS - 1, n_tiles)):
            start(t)

    @pl.when(g + (OUT_PROJ_BUFFERS - 1) < n_tiles)
    def _():
        start(g + (OUT_PROJ_BUFFERS - 1))

    for copy in tile_copies(g):
        copy.wait()
    slot = g % OUT_PROJ_BUFFERS
    out_t = jnp.dot(w_t_ref[...], y_buf[slot], preferred_element_type=F32)
    o_ref[...] = (x_buf[slot] + out_t).T


def _out_proj(x_t, y_t, w_t):
    batch, width, seq = y_t.shape
    tm = OUT_PROJ_TOKENS
    nt = seq // tm
    n_tiles = batch * nt
    return pl.pallas_call(
        functools.partial(_out_proj_kernel, nt, n_tiles),
        grid=(n_tiles,),
        in_specs=[
            pl.BlockSpec(memory_space=pl.ANY),
            pl.BlockSpec(memory_space=pl.ANY),
            pl.BlockSpec((D_MODEL, width), lambda g: (0, 0), pipeline_mode=pl.Buffered(1)),
        ],
        out_specs=pl.BlockSpec((tm, D_MODEL), lambda g: (g, 0)),
        out_shape=jax.ShapeDtypeStruct((batch * seq, D_MODEL), F32),
        scratch_shapes=[
            pltpu.VMEM((OUT_PROJ_BUFFERS, D_MODEL, tm), F32),
            pltpu.VMEM((OUT_PROJ_BUFFERS, width, tm), BF16),
            pltpu.SemaphoreType.DMA((2, OUT_PROJ_BUFFERS)),
        ],
        compiler_params=_params(("arbitrary",)),
        name="out_proj",
    )(x_t, y_t, w_t)


def _odd_proj_kernel(x_ref, y_ref, w_out_t_ref, g_ref, w_t_ref, cos_ref, sin_ref, qn_ref, kn_ref,
                     x1_ref, q_ref, k_ref, v_ref, gate_ref):
    out_t = jnp.dot(w_out_t_ref[...], y_ref[0], preferred_element_type=F32)
    x = x_ref[...].T + out_t
    x1_ref[0] = x
    ms = jnp.mean(x * x, axis=0, keepdims=True)
    h_t = (x * lax.rsqrt(ms + EPS) * g_ref[...]).astype(BF16)
    cos = cos_ref[...]
    sin = sin_ref[...]

    def proj(lo, width):
        return jnp.dot(w_t_ref[lo:lo + width, :], h_t, preferred_element_type=F32)

    k = proj(C_QK, C_QK)
    q = proj(0, C_QK)
    g = proj(2 * C_QK + C_W, C_W)
    v = proj(2 * C_QK, C_W)
    k = _head_norm_t(k, kn_ref[...], cos, sin)
    k_ref[...] = k.T.astype(BF16)
    q_ref[0] = _head_norm_t(q, qn_ref[...], cos, sin, Q_FOLD).astype(BF16)
    gate_ref[0] = _silu(g)
    v_ref[0] = v.astype(BF16)


def _odd_proj(x2d, y_t, w_out_t, gain_col, w_t, cos_t, sin_t, qn, kn):
    batch, mix_w, seq = y_t.shape
    tm = ODD_PROJ_TOKENS
    nt = seq // tm
    tokens = batch * seq
    odd_in = w_t.shape[0]
    resident = pl.Buffered(1)

    def tok(b, i):
        return (b * nt + i, 0)

    def feat(b, i):
        return (b, 0, i)

    def const2(b, i):
        return (0, 0)

    def tab(b, i):
        return (0, i)

    return pl.pallas_call(
        _odd_proj_kernel,
        grid=(batch, nt),
        in_specs=[
            pl.BlockSpec((tm, D_MODEL), tok),
            pl.BlockSpec((1, mix_w, tm), feat),
            pl.BlockSpec((D_MODEL, mix_w), const2, pipeline_mode=resident),
            pl.BlockSpec((D_MODEL, 1), const2),
            pl.BlockSpec((odd_in, D_MODEL), const2, pipeline_mode=resident),
            pl.BlockSpec((HALF_DIM, tm), tab),
            pl.BlockSpec((HALF_DIM, tm), tab),
            pl.BlockSpec((HEAD_DIM, tm), const2),
            pl.BlockSpec((HEAD_DIM, tm), const2),
        ],
        out_specs=[
            pl.BlockSpec((1, D_MODEL, tm), feat),
            pl.BlockSpec((1, C_QK, tm), feat),
            pl.BlockSpec((tm, C_QK), tok),
            pl.BlockSpec((1, C_W, tm), feat),
            pl.BlockSpec((1, C_W, tm), feat),
        ],
        out_shape=[
            jax.ShapeDtypeStruct((batch, D_MODEL, seq), F32),
            jax.ShapeDtypeStruct((batch, C_QK, seq), BF16),
            jax.ShapeDtypeStruct((tokens, C_QK), BF16),
            jax.ShapeDtypeStruct((batch, C_W, seq), BF16),
            jax.ShapeDtypeStruct((batch, C_W, seq), F32),
        ],
        compiler_params=_params(("parallel", "parallel")),
        name="odd_proj",
    )(x2d, y_t, w_out_t, gain_col, w_t, cos_t, sin_t, qn, kn)


def _diff_attn_kernel(lambda_init, running_max, q_ref, k_ref, v_ref, gate_ref, subln_ref,
                      lq1_ref, lk1_ref, lq2_ref, lk2_ref, y_ref, acc1_ref, acc2_ref):
    tq, tk = DIFF_Q, DIFF_K
    i = pl.program_id(2)
    q = q_ref[0]
    row = lax.broadcasted_iota(jnp.int32, q.shape, 0)
    zero = jnp.zeros_like(q)
    q_pads = (jnp.where(row < HEAD_DIM, q, zero), jnp.where(row >= HEAD_DIM, q, zero))
    acc_refs = (acc1_ref, acc2_ref)

    def load_kv(start, size):
        start = pl.multiple_of(start, size)
        return k_ref[0, pl.ds(start, size), :], v_ref[0, :, pl.ds(start, size)]

    def diag_valid(key0, size, n_cols):
        key_chunk = (lax.broadcasted_iota(jnp.int32, (size, n_cols), 0) + key0) // CHUNK
        qry_chunk = (lax.broadcasted_iota(jnp.int32, (size, n_cols), 1) + (tq - n_cols)) // CHUNK
        return key_chunk <= qry_chunk

    def online_step(k, v_t, carry, valid):
        new = []
        for c in range(2):
            m, l = carry[2 * c], carry[2 * c + 1]
            s = jnp.dot(k, q_pads[c], preferred_element_type=F32)
            if valid is not None:
                s = jnp.where(valid, s, NEG_INF)
            m_new = jnp.maximum(m, jnp.max(s, axis=0, keepdims=True))
            alpha = jnp.exp2(m - m_new)
            p = jnp.exp2(s - m_new)
            l_new = alpha * l + jnp.sum(p, axis=0, keepdims=True)
            pv = jnp.dot(v_t, p.astype(BF16), preferred_element_type=F32)
            acc_refs[c][...] = alpha * acc_refs[c][...] + pv
            new += [m_new, l_new]
        return tuple(new)

    def direct_probs(k, q_c, valid):
        p = jnp.exp2(jnp.dot(k, q_c, preferred_element_type=F32))
        if valid is not None:
            p = jnp.where(valid, p, 0.0)
        return p.astype(BF16), jnp.sum(p, axis=0, keepdims=True)

    def emit_lagged(tiles, pv_lag, on_pv):
        pending = []

        def emit_pv():
            tag, v_t, p, total = pending.pop(0)
            on_pv(tag, total, jnp.dot(v_t, p, preferred_element_type=F32))

        for tag, k, v_t, q_c, valid in tiles:
            p, total = direct_probs(k, q_c, valid)
            pending.append((tag, v_t, p, total))
            if len(pending) > pv_lag:
                emit_pv()
        while pending:
            emit_pv()

    lam = (jnp.exp(jnp.sum(lq1_ref[...] * lk1_ref[...], axis=-1, keepdims=True))
           - jnp.exp(jnp.sum(lq2_ref[...] * lk2_ref[...], axis=-1, keepdims=True))
           + lambda_init)

    def finish(cols, acc1, acc2, l1, l2):
        o = acc1 * (1.0 / l1) - lam * (acc2 * (1.0 / l2))
        ms = jnp.mean(o * o, axis=0, keepdims=True)
        o = o * lax.rsqrt(ms + EPS) * subln_ref[:, cols] * (1.0 - lambda_init)
        y_ref[0, :, cols] = (o * gate_ref[0, :, cols]).astype(BF16)

    q0 = i * tq
    acc1_ref[...] = jnp.zeros_like(acc1_ref)
    acc2_ref[...] = jnp.zeros_like(acc2_ref)
    if running_max:
        init = (jnp.full((1, tq), NEG_INF, F32), jnp.zeros((1, tq), F32)) * 2
        carry = lax.fori_loop(0, q0 // tk,
                              lambda j, c: online_step(*load_kv(j * tk, tk), c, None), init)
        for key0 in range(0, tq, tk):
            carry = online_step(*load_kv(q0 + key0, tk), carry, diag_valid(key0, tk, tq))
        finish(slice(0, tq), acc1_ref[...], acc2_ref[...], carry[1], carry[3])
    else:
        def group(t, l_sums):
            l_sums = list(l_sums)
            pv_sums = [0.0, 0.0]

            def on_pv(c, total, pv):
                l_sums[c] = l_sums[c] + total
                pv_sums[c] = pv_sums[c] + pv

            kv = [load_kv(t * tq + u * tk, tk) for u in range(tq // tk)]
            emit_lagged([(c, k, v_t, q_pads[c], None) for k, v_t in kv for c in range(2)],
                        DIFF_PV_LAG, on_pv)
            for c in range(2):
                acc_refs[c][...] += pv_sums[c]
            return tuple(l_sums)

        l_sums = lax.fori_loop(0, i, group, (jnp.zeros((1, tq), F32),) * 2)

        td = DIFF_DIAG_K
        n_seg = tq // td
        parts = [[[] for _ in range(n_seg)] for _ in range(2)]

        def on_diag_pv(tag, total, pv):
            b, c = tag
            for s in range(b, n_seg):
                seg = slice((s - b) * td, (s - b + 1) * td)
                parts[c][s].append((total[:, seg], pv[:, seg]))
            if c == 1:
                cols = slice(b * td, (b + 1) * td)
                sums = [(l_sums[cc][:, cols] + sum(t for t, _ in parts[cc][b]),
                         acc_refs[cc][:, cols] + sum(p for _, p in parts[cc][b]))
                        for cc in range(2)]
                finish(cols, sums[0][1], sums[1][1], sums[0][0], sums[1][0])

        diag_tiles = []
        for b in range(n_seg):
            k, v_t = load_kv(q0 + b * td, td)
            valid = diag_valid(b * td, td, tq - b * td)
            diag_tiles += [((b, c), k, v_t, q_pads[c][:, b * td:], valid) for c in range(2)]
        emit_lagged(diag_tiles, 1, on_diag_pv)


def _diff_attn(running_max, q_t, k3, v_t, gate_t, subln_tile, lq1, lk1, lq2, lk2, lambda_init):
    batch, seq = k3.shape[0], k3.shape[1]
    tq = DIFF_Q
    nq = seq // tq
    head_w = 2 * HEAD_DIM

    def blk(b, h, i):
        return (b, h, i)

    def const2(b, h, i):
        return (0, 0)

    return pl.pallas_call(
        functools.partial(_diff_attn_kernel, lambda_init, running_max),
        grid=(batch, C_HEADS, nq),
        in_specs=[
            pl.BlockSpec((1, head_w, tq), blk),
            pl.BlockSpec((1, seq, head_w), lambda b, h, i: (b, 0, h)),
            pl.BlockSpec((1, C_V_DIM, seq), lambda b, h, i: (b, h, 0)),
            pl.BlockSpec((1, C_V_DIM, tq), blk),
            pl.BlockSpec((C_V_DIM, tq), const2),
            pl.BlockSpec((1, HEAD_DIM), const2),
            pl.BlockSpec((1, HEAD_DIM), const2),
            pl.BlockSpec((1, HEAD_DIM), const2),
            pl.BlockSpec((1, HEAD_DIM), const2),
        ],
        out_specs=pl.BlockSpec((1, C_V_DIM, tq), blk),
        out_shape=jax.ShapeDtypeStruct((batch, C_W, seq), BF16),
        scratch_shapes=[pltpu.VMEM((C_V_DIM, tq), F32), pltpu.VMEM((C_V_DIM, tq), F32)],
        compiler_params=_params(("parallel", "parallel", "arbitrary")),
        name="diff_attn_online" if running_max else "diff_attn_direct",
    )(q_t, k3, v_t, gate_t, subln_tile, lq1, lk1, lq2, lk2)


def _qk_score_bound(q_gain, k_gain):
    bound = HEAD_DIM * QK_SCALE * jnp.max(jnp.abs(q_gain)) * jnp.max(jnp.abs(k_gain))
    return bound.astype(F32) * (1.0 + 2.0 ** -6)


def _rope_tables_t(seq):
    inv = 1.0 / (ROPE_THETA ** (jnp.arange(0, HEAD_DIM, 2, dtype=F32) / HEAD_DIM))
    ang = inv[:, None] * jnp.arange(seq, dtype=F32)[None, :]
    return jnp.cos(ang), jnp.sin(ang)


def _gain_tile(g, width):
    return jnp.broadcast_to(g.astype(F32)[:, None], (g.shape[0], width))


def _rel_bias_t(rel_table, n_q):
    n_k = B_PREV_ROWS + n_q
    width = n_k + n_q
    offset = np.arange(width) - (n_k - 1)
    rel = np.clip(B_PREV_ROWS + offset, -B_MAX_REL, B_MAX_REL) + B_MAX_REL
    by_offset = rel_table.astype(F32)[:, rel]
    heads = rel_table.shape[0]
    sub = 8

    def build(f_ref, o_ref):
        for h in range(heads):
            f = f_ref[h:h + 1, :]
            rows = jnp.concatenate(
                [pltpu.roll(f, (width + b - (sub - 1)) % width, axis=1) for b in range(sub)], axis=0)
            for blk in range(n_k // sub):
                lead = n_k - sub - sub * blk
                o_ref[h, sub * blk:sub * (blk + 1), :] = pltpu.roll(
                    rows, (width - lead) % width, axis=1)[:, :n_q]

    return pl.pallas_call(
        build,
        out_shape=jax.ShapeDtypeStruct((heads, n_k, n_q), F32),
        name="rel_bias_tile",
    )(by_offset)


def kernel(x, ev_norm, ev_w_in, ev_w_out, ev_a_q_norm, ev_a_k_norm, ev_a_sinks,
           ev_b_q_norm, ev_b_k_norm, ev_b_rel_bias, od_norm, od_w_in, od_w_out,
           od_q_norm, od_k_norm, od_lambda_q1, od_lambda_k1, od_lambda_q2,
           od_lambda_k2, od_subln):
    batch, seq, d = x.shape
    cos_t, sin_t = _rope_tables_t(seq)
    depth = ev_norm.shape[0] + od_norm.shape[0]
    assert ev_norm.shape[0] == od_norm.shape[0]
    x2d = x.reshape(batch * seq, d)
    for layer in range(depth):
        i = layer // 2
        if layer % 2 == 0:
            w_in_t = ev_w_in[i].T.astype(BF16)
            w_out_t = ev_w_out[i].T.astype(BF16)
            aq_t, ak, av_t, bq_t, bk, bv_t, gate_t = _even_proj(
                x2d, ev_norm[i][None, :], w_in_t, cos_t, sin_t,
                *[_gain_tile(g[i], EVEN_PROJ_TOKENS)
                  for g in (ev_a_q_norm, ev_a_k_norm, ev_b_q_norm, ev_b_k_norm)],
                batch, seq)
            mask_a = jnp.where(_band_mask_variants(A_PREV_ROWS, MIX_TOKENS), 0.0, NEG_INF).astype(F32)
            bias_b = jnp.where(_band_mask_variants(B_PREV_ROWS, MIX_TOKENS)[:, None],
                               _rel_bias_t(ev_b_rel_bias[i] * LOG2E, MIX_TOKENS)[None], NEG_INF)
            sinks = ev_a_sinks[i].astype(F32)
            mix_args = (sinks * LOG2E, aq_t, ak.reshape(batch, seq, A_KV), av_t,
                        bq_t, bk.reshape(batch, seq, B_W), bv_t, gate_t, mask_a, bias_b)
            bound_a = jnp.maximum(_qk_score_bound(ev_a_q_norm[i], ev_a_k_norm[i]),
                                  jnp.max(jnp.abs(sinks)))
            bound_b = (_qk_score_bound(ev_b_q_norm[i], ev_b_k_norm[i])
                       + jnp.max(jnp.abs(ev_b_rel_bias[i])))
            y_t = lax.cond(jnp.maximum(bound_a, bound_b) <= SAFE_SCORE,
                           functools.partial(_even_mix, False),
                           functools.partial(_even_mix, True), *mix_args)
            even_w_out_t = w_out_t
        else:
            lambda_init = 0.8 - 0.6 * math.exp(-0.3 * layer)
            w_in_t = od_w_in[i].T.astype(BF16)
            w_out_t = od_w_out[i].T.astype(BF16)
            x_t, q_t, k, v_t, gate_t = _odd_proj(
                x2d, y_t, even_w_out_t, od_norm[i][:, None], w_in_t, cos_t, sin_t,
                _gain_tile(od_q_norm[i], ODD_PROJ_TOKENS), _gain_tile(od_k_norm[i], ODD_PROJ_TOKENS))
            diff_args = (q_t, k.reshape(batch, seq, C_QK), v_t, gate_t,
                         _gain_tile(od_subln[i], DIFF_Q),
                         od_lambda_q1[i][None, :], od_lambda_k1[i][None, :],
                         od_lambda_q2[i][None, :], od_lambda_k2[i][None, :])
            y_t = lax.cond(
                _qk_score_bound(od_q_norm[i], od_k_norm[i]) <= SAFE_SCORE,
                lambda *a: _diff_attn(False, *a, lambda_init),
                lambda *a: _diff_attn(True, *a, lambda_init),
                *diff_args)
            x2d = _out_proj(x_t, y_t, w_out_t)
    return x2d.reshape(batch, seq, d)
```
